```python
import math
import jax, jax.numpy as jnp
from jax import lax
import numpy as np

D_MODEL = 2048
BATCH = 16
SEQ = 256
DEPTH = 2
DEC_BATCH = 2
DEC_SEQ = 1024
PAST_LEN = 512

GRID_W = 64
HEAD_DIM = 128
N_Q_HEADS = D_MODEL // 2 // HEAD_DIM
N_KV_HEADS = 2
Q_PER_KV = N_Q_HEADS // N_KV_HEADS
ATTN_W = N_Q_HEADS * HEAD_DIM
KV_W = N_KV_HEADS * HEAD_DIM
SSM_W = D_MODEL // 4
SSM_GROUP = 16
SSM_GROUPS = SSM_W // SSM_GROUP
SSM_STATE = 64
RET_HEADS = D_MODEL // 4 // HEAD_DIM
RET_W = RET_HEADS * HEAD_DIM
MIX_W = ATTN_W + SSM_W + RET_W
IN_SIZES = (ATTN_W, KV_W, KV_W, SSM_W, RET_W, RET_W, RET_W, RET_W)
IN_W = ATTN_W + 2 * KV_W + SSM_W + 4 * RET_W
D_FF = 256 * ((8 * D_MODEL // 3 + 255) // 256)
N_SUB = 3
Q_BLOCK = 128
RET_CHUNK = 128
ROPE_BASE = 10000.0
ROPE_FREQS = HEAD_DIM // 4
EPS = 1e-6

kernel_name = "hybrid_diffusion_trunk_step"


def rmsnorm(x, g):
    x32 = x.astype(jnp.float32)
    y = x32 * lax.rsqrt(jnp.mean(x32 * x32, axis=-1, keepdims=True) + EPS)
    return (y * g.astype(jnp.float32)).astype(x.dtype)


def grid_rope(length):
    rows = length // GRID_W
    r = jnp.repeat(jnp.arange(rows), GRID_W).astype(jnp.float32)
    col = jnp.tile(jnp.arange(GRID_W), rows).astype(jnp.float32)
    inv = ROPE_BASE ** (-jnp.arange(ROPE_FREQS, dtype=jnp.float32) / ROPE_FREQS)
    ang = jnp.stack([r, col], axis=-1)[:, :, None] * inv
    return jnp.cos(ang), jnp.sin(ang)


def apply_rope(x, rope):
    cos, sin = rope
    b, l, h, d = x.shape
    x4 = x.astype(jnp.float32).reshape(b, l, h, 2, 2, ROPE_FREQS)
    rot = jnp.stack([-x4[..., 1, :], x4[..., 0, :]], axis=-2)
    c4 = cos[None, :, None, :, None, :]
    s4 = sin[None, :, None, :, None, :]
    return (x4 * c4 + rot * s4).reshape(b, l, h, d).astype(x.dtype)


def adaln(cond, w, b):
    m = (jax.nn.silu(cond) @ w + b).reshape(cond.shape[0], N_SUB, 3, D_MODEL)
    return m[:, :, 0], m[:, :, 1], m[:, :, 2]


def swiglu(x, w_in, w_out):
    a, u = jnp.split(x @ w_in, 2, axis=-1)
    return (jax.nn.silu(a) * u) @ w_out


def block_attention(q, k, v):
    b, lq = q.shape[:2]
    nb = lq // Q_BLOCK
    qb = jnp.moveaxis(q.reshape(b, nb, Q_BLOCK, N_KV_HEADS, Q_PER_KV, HEAD_DIM), 1, 0)
    scale = HEAD_DIM ** -0.5

    def one(qc):
        s = jnp.einsum("bqkgd,bskd->bkgqs", qc, k, preferred_element_type=jnp.float32) * scale
        p = jax.nn.softmax(s, axis=-1).astype(v.dtype)
        return jnp.einsum("bkgqs,bskd->bqkgd", p, v)

    out = lax.map(one, qb)
    return jnp.moveaxis(out, 0, 1).reshape(b, lq, ATTN_W)


def _lin_rec(e1, e2):
    a1, b1 = e1
    a2, b2 = e2
    return a1 * a2, a2 * b1 + b2


def s5_scan(u, a_re, a_im, log_dt, b_re, b_im, c_re, c_im, h0):
    lam = lax.complex(a_re.astype(jnp.float32), a_im.astype(jnp.float32))
    lam_bar = jnp.exp(lam * jnp.exp(log_dt.astype(jnp.float32))[:, None])
    bmat = lax.complex(b_re.astype(jnp.float32), b_im.astype(jnp.float32))
    b_bar = ((lam_bar - 1.0) / lam)[..., None] * bmat
    bu = jnp.einsum("gph,blgh->blgp", b_bar, u.astype(jnp.complex64))
    bu = bu.at[:, 0].add(lam_bar[None] * h0)
    a = jnp.broadcast_to(lam_bar, bu.shape)
    _, xs = lax.associative_scan(_lin_rec, (a, bu), axis=1)
    cmat = lax.complex(c_re.astype(jnp.float32), c_im.astype(jnp.float32))
    y = jnp.real(jnp.einsum("ghp,blgp->blgh", cmat, xs))
    return y, xs[:, -1]


def s5_mixer(u, p, h0):
    b, l, _ = u.shape
    u32 = u.astype(jnp.float32)
    ug = u32.reshape(b, l, SSM_GROUPS, SSM_GROUP)

    def run(d, seq):
        return s5_scan(seq, p["ssm_a_re"][d], p["ssm_a_im"][d], p["ssm_log_dt"][d],
                       p["ssm_b_re"][d], p["ssm_b_im"][d], p["ssm_c_re"][d], p["ssm_c_im"][d], h0[:, d])

    y_f, h_f = run(0, ug)
    y_b, h_b = run(1, ug[:, ::-1])
    y = (y_f + y_b[:, ::-1]).reshape(b, l, SSM_W) + p["ssm_d"].astype(jnp.float32) * u32
    z = jax.nn.gelu(y).astype(u.dtype) @ p["w_ssm_glu"]
    return z[..., :SSM_W] * jax.nn.sigmoid(z[..., SSM_W:]), jnp.stack([h_f, h_b], axis=1)


def retention_scan(q, k, v, log_g, s0):
    b, l = q.shape[:2]
    n = l // RET_CHUNK

    def chunks(t):
        return jnp.moveaxis(t.reshape(b, n, RET_CHUNK, RET_HEADS, HEAD_DIM), 1, 0)

    idx = jnp.arange(RET_CHUNK, dtype=jnp.float32)
    diff = idx[:, None] - idx[None, :]
    inner = jnp.where(diff >= 0, jnp.exp(jnp.maximum(diff, 0.0)[None] * log_g[:, None, None]), 0.0)
    q_dec = jnp.exp((idx[:, None] + 1.0) * log_g[None])
    k_dec = jnp.exp((RET_CHUNK - 1.0 - idx)[:, None] * log_g[None])
    c_dec = jnp.exp(RET_CHUNK * log_g)

    def step(s, qkv):
        qc, kc, vc = qkv
        att = jnp.einsum("bihd,bjhd->bhij", qc, kc) * inner
        o = (jnp.einsum("bhij,bjhe->bihe", att, vc)
             + jnp.einsum("bihd,bhde->bihe", qc, s) * q_dec[None, :, :, None])
        s = s * c_dec[None, :, None, None] + jnp.einsum("bjhd,bjhe->bhde", kc * k_dec[None, :, :, None], vc)
        return s, o

    s_fin, o = lax.scan(step, s0, (chunks(q), chunks(k), chunks(v)))
    return jnp.moveaxis(o, 0, 1).reshape(b, l, RET_HEADS, HEAD_DIM), s_fin


def retention(q, k, v, g, decay_logit, s0):
    b, l = q.shape[:2]
    log_g = jax.nn.log_sigmoid(decay_logit.astype(jnp.float32))
    q32, k32, v32 = (t.astype(jnp.float32) for t in (q, k, v))
    o_f, s_f = retention_scan(q32, k32, v32, log_g[0], s0[:, 0])
    o_b, s_b = retention_scan(q32[:, ::-1], k32[:, ::-1], v32[:, ::-1], log_g[1], s0[:, 1])
    o = o_f + o_b[:, ::-1]
    o = o - jnp.mean(o, axis=-1, keepdims=True)
    o = o * lax.rsqrt(jnp.mean(o * o, axis=-1, keepdims=True) + EPS)
    out = o.reshape(b, l, RET_W) * jax.nn.silu(g.astype(jnp.float32))
    return out.astype(g.dtype), jnp.stack([s_f, s_b], axis=1)


def mixer(h, p, rope, ctx):
    b, l, _ = h.shape
    cuts = [int(i) for i in np.cumsum(IN_SIZES)[:-1]]
    qa, ka, va, us, qr, kr, vr, gr = jnp.split(h @ p["w_in"], cuts, axis=-1)
    qa = rmsnorm(qa.reshape(b, l, N_Q_HEADS, HEAD_DIM), p["q_norm_g"])
    ka = rmsnorm(ka.reshape(b, l, N_KV_HEADS, HEAD_DIM), p["k_norm_g"])
    va = va.reshape(b, l, N_KV_HEADS, HEAD_DIM)
    qr = qr.reshape(b, l, RET_HEADS, HEAD_DIM)
    kr = kr.reshape(b, l, RET_HEADS, HEAD_DIM) * HEAD_DIM ** -0.5
    vr = vr.reshape(b, l, RET_HEADS, HEAD_DIM)
    if ctx is None:
        k_all, v_all = ka, va
        h0 = jnp.zeros((b, 2, SSM_GROUPS, SSM_STATE), jnp.complex64)
        s0 = jnp.zeros((b, 2, RET_HEADS, HEAD_DIM, HEAD_DIM), jnp.float32)
    else:
        k_ctx, v_ctx, h0, s0 = ctx
        qa, ka, qr, kr = (apply_rope(t, rope) for t in (qa, ka, qr, kr))
        k_all = jnp.concatenate([k_ctx.astype(ka.dtype), ka], axis=1)
        v_all = jnp.concatenate([v_ctx.astype(va.dtype), va], axis=1)
    attn = block_attention(qa.reshape(b, l, N_KV_HEADS, Q_PER_KV, HEAD_DIM), k_all, v_all)
    ssm, h_fin = s5_mixer(us, p, h0)
    ret, s_fin = retention(qr, kr, vr, gr, p["ret_decay_logit"], s0)
    out = jnp.concatenate([attn, ssm.astype(attn.dtype), ret.astype(attn.dtype)], axis=-1) @ p["w_out"]
    return out, ((ka, va, h_fin, s_fin) if ctx is None else None)


def layer(x, mod, p, rope, ctx):
    shift, scale, gate = mod

    def pre(x, i):
        return rmsnorm(x, p["norm_g"][i]) * (1.0 + scale[:, i, None]) + shift[:, i, None]

    x = x + 0.5 * gate[:, 0, None] * swiglu(pre(x, 0), p["w_ffn_in"][0], p["w_ffn_out"][0])
    y, ctx_out = mixer(pre(x, 1), p, rope, ctx)
    x = x + gate[:, 1, None] * y
    x = x + 0.5 * gate[:, 2, None] * swiglu(pre(x, 2), p["w_ffn_in"][1], p["w_ffn_out"][1])
    return x, ctx_out


def setup_inputs(seed: int = 0) -> dict:
    key = jax.random.key(seed)
    ks = jax.random.split(key, 32)
    f32 = jnp.float32

    def nrm(k, shape, s):
        return jax.random.normal(k, shape, f32) * s

    a_im0 = jnp.pi * jnp.arange(SSM_STATE, dtype=f32)
    ret_logit0 = jnp.log(2.0 ** (5.0 + jnp.arange(RET_HEADS, dtype=f32)) - 1.0)
    return {
        "x_prompt": nrm(ks[0], (BATCH, SEQ, D_MODEL), 1.0),
        "x_sample": nrm(ks[1], (DEC_BATCH, DEC_SEQ, D_MODEL), 1.0),
        "cache_k": nrm(ks[2], (DEC_BATCH, DEPTH, PAST_LEN, N_KV_HEADS, HEAD_DIM), 1.0),
        "cache_v": nrm(ks[3], (DEC_BATCH, DEPTH, PAST_LEN, N_KV_HEADS, HEAD_DIM), 1.0),
        "state_ssm": nrm(ks[4], (DEC_BATCH, DEPTH, 2, SSM_GROUPS, SSM_STATE, 2), 0.1),
        "state_ret": nrm(ks[5], (DEC_BATCH, DEPTH, 2, RET_HEADS, HEAD_DIM, HEAD_DIM), 0.5),
        "c": nrm(ks[6], (DEC_BATCH, D_MODEL), 1.0),
        "c_ctx": nrm(ks[7], (D_MODEL,), 1.0),
        "w_mod": nrm(ks[8], (DEPTH, D_MODEL, N_SUB * 3 * D_MODEL), 0.5 * D_MODEL ** -0.5),
        "b_mod": nrm(ks[9], (DEPTH, N_SUB * 3 * D_MODEL), 0.02),
        "norm_g": 1.0 + nrm(ks[10], (DEPTH, N_SUB, D_MODEL), 0.02),
        "w_ffn_in": nrm(ks[11], (DEPTH, 2, D_MODEL, 2 * D_FF), D_MODEL ** -0.5),
        "w_ffn_out": nrm(ks[12], (DEPTH, 2, D_FF, D_MODEL), D_FF ** -0.5),
        "w_in": nrm(ks[13], (DEPTH, D_MODEL, IN_W), D_MODEL ** -0.5),
        "w_out": nrm(ks[14], (DEPTH, MIX_W, D_MODEL), MIX_W ** -0.5),
        "q_norm_g": 1.0 + nrm(ks[15], (DEPTH, HEAD_DIM), 0.02),
        "k_norm_g": 1.0 + nrm(ks[16], (DEPTH, HEAD_DIM), 0.02),
        "ssm_a_re": -0.5 + nrm(ks[17], (DEPTH, 2, SSM_GROUPS, SSM_STATE), 0.01),
        "ssm_a_im": a_im0 + nrm(ks[18], (DEPTH, 2, SSM_GROUPS, SSM_STATE), 0.01),
        "ssm_log_dt": jax.random.uniform(ks[19], (DEPTH, 2, SSM_GROUPS), f32, math.log(1e-3), math.log(1e-1)),
        "ssm_b_re": nrm(ks[20], (DEPTH, 2, SSM_GROUPS, SSM_STATE, SSM_GROUP), (2 * SSM_GROUP) ** -0.5),
        "ssm_b_im": nrm(ks[21], (DEPTH, 2, SSM_GROUPS, SSM_STATE, SSM_GROUP), (2 * SSM_GROUP) ** -0.5),
        "ssm_c_re": nrm(ks[22], (DEPTH, 2, SSM_GROUPS, SSM_GROUP, SSM_STATE), SSM_STATE ** -0.5),
        "ssm_c_im": nrm(ks[23], (DEPTH, 2, SSM_GROUPS, SSM_GROUP, SSM_STATE), SSM_STATE ** -0.5),
        "ssm_d": nrm(ks[24], (DEPTH, SSM_W), 1.0),
        "w_ssm_glu": nrm(ks[25], (DEPTH, SSM_W, 2 * SSM_W), SSM_W ** -0.5),
        "ret_decay_logit": ret_logit0 + nrm(ks[26], (DEPTH, 2, RET_HEADS), 0.01),
        "final_norm_g": 1.0 + nrm(ks[27], (D_MODEL,), 0.02),
    }


def reference(x_prompt, x_sample, cache_k, cache_v, state_ssm, state_ret, c, c_ctx,
              w_mod, b_mod, norm_g, w_ffn_in, w_ffn_out, w_in, w_out, q_norm_g, k_norm_g,
              ssm_a_re, ssm_a_im, ssm_log_dt, ssm_b_re, ssm_b_im, ssm_c_re, ssm_c_im, ssm_d,
              w_ssm_glu, ret_decay_logit, final_norm_g):
    rope = grid_rope(x_sample.shape[1])
    xp, xs = x_prompt, x_sample
    ks_, vs_, hs_, ss_ = [], [], [], []
    for l in range(DEPTH):
        p = {
            "norm_g": norm_g[l], "w_ffn_in": w_ffn_in[l], "w_ffn_out": w_ffn_out[l],
            "w_in": w_in[l], "w_out": w_out[l], "q_norm_g": q_norm_g[l], "k_norm_g": k_norm_g[l],
            "ssm_a_re": ssm_a_re[l], "ssm_a_im": ssm_a_im[l], "ssm_log_dt": ssm_log_dt[l],
            "ssm_b_re": ssm_b_re[l], "ssm_b_im": ssm_b_im[l], "ssm_c_re": ssm_c_re[l],
            "ssm_c_im": ssm_c_im[l], "ssm_d": ssm_d[l], "w_ssm_glu": w_ssm_glu[l],
            "ret_decay_logit": ret_decay_logit[l],
        }
        xp, (k_l, v_l, h_l, s_l) = layer(xp, adaln(c_ctx[None], w_mod[l], b_mod[l]), p, None, None)
        ks_.append(k_l)
        vs_.append(v_l)
        hs_.append(jnp.stack([jnp.real(h_l), jnp.imag(h_l)], axis=-1))
        ss_.append(s_l)
        st = state_ssm[:, l]
        ctx_l = (cache_k[:, l], cache_v[:, l],
                 lax.complex(st[..., 0].astype(jnp.float32), st[..., 1].astype(jnp.float32)),
                 state_ret[:, l].astype(jnp.float32))
        xs, _ = layer(xs, adaln(c, w_mod[l], b_mod[l]), p, rope, ctx_l)
    y_prompt = rmsnorm(xp, final_norm_g)
    y_sample = rmsnorm(xs, final_norm_g)
    new_cache_k = jnp.stack(ks_, axis=1)
    new_cache_v = jnp.stack(vs_, axis=1)
    new_state_ssm = jnp.stack(hs_, axis=1)
    new_state_ret = jnp.stack(ss_, axis=1)
    return (y_prompt, y_sample, new_cache_k, new_cache_v, new_state_ssm, new_state_ret)
```

```python
import functools
import math

import jax
import jax.numpy as jnp
from jax import lax
from jax.experimental import pallas as pl
from jax.experimental.pallas import tpu as pltpu

D_MODEL = 2048
BATCH = 16
SEQ = 256
DEPTH = 2
DEC_BATCH = 2
DEC_SEQ = 1024
PAST_LEN = 512
GRID_W = 64
HEAD_DIM = 128
N_Q_HEADS = 8
N_KV_HEADS = 2
Q_PER_KV = 4
ATTN_W = 1024
KV_W = 256
SSM_W = 512
SSM_GROUP = 16
SSM_GROUPS = 32
SSM_STATE = 64
RET_HEADS = 4
RET_W = 512
IN_W = 4096
D_FF = 5632
N_SUB = 3
RET_CHUNK = 128
ROPE_BASE = 10000.0
ROPE_FREQS = 32
EPS = 1e-6

N_PROMPT = BATCH * SEQ
N_SAMPLE = DEC_BATCH * DEC_SEQ
N_TOK = N_PROMPT + N_SAMPLE
TM = 1024
N_MT = N_TOK // TM
TILE_MOD = (0, 0, 0, 0, 1, 2)
TF = 256
STATE_W = 2 * SSM_GROUPS * SSM_STATE
N_STILE = STATE_W // 256
SSM_ROWS = 16
SSM_SEGS = 8
SSM_TC = 32
VMEM_LIMIT = 56 * 1024 * 1024

BF16 = jnp.bfloat16
F32 = jnp.float32


def _cparams(sem):
    return pltpu.CompilerParams(dimension_semantics=sem, vmem_limit_bytes=VMEM_LIMIT)


def _dot(a, b):
    return jnp.dot(a, b, preferred_element_type=F32)


def _dot_nt(a, b):
    return lax.dot_general(a, b, (((1,), (1,)), ((), ())), preferred_element_type=F32)


def _sigmoid(x):
    return 1.0 / (1.0 + jnp.exp(-x))


def _silu(x):
    return x * _sigmoid(x)


def _adaln_kernel(c_ref, w_ref, b_ref, o_ref):
    cond = c_ref[...]
    a = _silu(cond).astype(BF16)
    o_ref[0] = _dot(a, w_ref[0].astype(BF16)) + b_ref[0]


def _adaln(cond8, w_mod, b_mod):
    n = w_mod.shape[-1]
    tn = 1024
    return pl.pallas_call(
        _adaln_kernel,
        grid=(DEPTH, n // tn),
        in_specs=[
            pl.BlockSpec((8, D_MODEL), lambda l, j: (0, 0)),
            pl.BlockSpec((1, D_MODEL, tn), lambda l, j: (l, 0, j)),
            pl.BlockSpec((1, 1, tn), lambda l, j: (l, 0, j)),
        ],
        out_specs=pl.BlockSpec((1, 8, tn), lambda l, j: (l, 0, j)),
        out_shape=jax.ShapeDtypeStruct((DEPTH, 8, n), F32),
        compiler_params=_cparams(("arbitrary", "arbitrary")),
        name="adaln",
    )(cond8, w_mod, b_mod.reshape(DEPTH, 1, n))


def _norm_mod(x, g, shift, scale):
    ms = jnp.mean(x * x, axis=-1, keepdims=True)
    y = x * lax.rsqrt(ms + EPS) * g
    return y * (1.0 + scale) + shift


def _ffn_kernel(x_ref, g_ref, sh_ref, sc_ref, gt_ref, wa_ref, wu_ref, wo_ref, *rest, final):
    if final:
        fg_ref, o_ref, h_scr = rest
    else:
        o_ref, h_scr = rest
    j = pl.program_id(1)

    @pl.when(j == 0)
    def _():
        h = _norm_mod(x_ref[...], g_ref[...], sh_ref[0], sc_ref[0])
        h_scr[...] = h.astype(BF16)
        o_ref[...] = jnp.zeros_like(o_ref)

    h = h_scr[...]
    a = _dot(h, wa_ref[...].astype(BF16))
    u = _dot(h, wu_ref[...].astype(BF16))
    mid = (_silu(a) * u).astype(BF16)
    o_ref[...] += _dot(mid, wo_ref[...].astype(BF16))

    @pl.when(j == pl.num_programs(1) - 1)
    def _():
        out = x_ref[...] + (0.5 * gt_ref[0]) * o_ref[...]
        if final:
            ms = jnp.mean(out * out, axis=-1, keepdims=True)
            out = out * lax.rsqrt(ms + EPS) * fg_ref[...]
        o_ref[...] = out


def _ffn(x, g, shift, scale, gate, w_in, w_out, final_g=None):
    final = final_g is not None
    nf = D_FF // TF
    tok = pl.BlockSpec((TM, D_MODEL), lambda i, j: (i, 0))
    vec = pl.BlockSpec((1, D_MODEL), lambda i, j: (0, 0))
    mod = pl.BlockSpec((1, 1, D_MODEL), lambda i, j: (i, 0, 0))
    tok_in = pl.BlockSpec((TM, D_MODEL), lambda i, j: (i, 0), pipeline_mode=pl.Buffered(1))
    in_specs = [
        tok_in, vec, mod, mod, mod,
        pl.BlockSpec((D_MODEL, TF), lambda i, j: (0, j)),
        pl.BlockSpec((D_MODEL, TF), lambda i, j: (0, j + nf)),
        pl.BlockSpec((TF, D_MODEL), lambda i, j: (j, 0)),
    ]
    args = [x, g.reshape(1, D_MODEL), shift, scale, gate, w_in, w_in, w_out]
    if final:
        in_specs.append(vec)
        args.append(final_g.reshape(1, D_MODEL))
    return pl.pallas_call(
        functools.partial(_ffn_kernel, final=final),
        grid=(N_MT, nf),
        in_specs=in_specs,
        out_specs=tok,
        out_shape=jax.ShapeDtypeStruct((N_TOK, D_MODEL), F32),
        scratch_shapes=[pltpu.VMEM((TM, D_MODEL), BF16)],
        compiler_params=_cparams(("arbitrary", "arbitrary")),
        name="ffn_final" if final else "ffn",
    )(*args)


def _inproj_kernel(x_ref, g_ref, sh_ref, sc_ref, w_ref, o_ref, h_scr):
    @pl.when(pl.program_id(1) == 0)
    def _():
        h = _norm_mod(x_ref[...], g_ref[...], sh_ref[0], sc_ref[0])
        h_scr[...] = h.astype(BF16)

    o_ref[...] = _dot(h_scr[...], w_ref[...].astype(BF16))


def _inproj(x, g, shift, scale, w):
    tn = 512
    mod = pl.BlockSpec((1, 1, D_MODEL), lambda i, j: (i, 0, 0))
    return pl.pallas_call(
        _inproj_kernel,
        grid=(N_MT, IN_W // tn),
        in_specs=[
            pl.BlockSpec((TM, D_MODEL), lambda i, j: (i, 0)),
            pl.BlockSpec((1, D_MODEL), lambda i, j: (0, 0)),
            mod, mod,
            pl.BlockSpec((D_MODEL, tn), lambda i, j: (0, j)),
        ],
        out_specs=pl.BlockSpec((TM, tn), lambda i, j: (i, j)),
        out_shape=jax.ShapeDtypeStruct((N_TOK, IN_W), F32),
        scratch_shapes=[pltpu.VMEM((TM, D_MODEL), BF16)],
        compiler_params=_cparams(("arbitrary", "arbitrary")),
        name="inproj",
    )(x, g.reshape(1, D_MODEL), shift, scale, w)


def _outproj_kernel(x_ref, a_ref, s_ref, r_ref, gt_ref, w_ref, o_ref):
    y = _dot(a_ref[...], w_ref[0:ATTN_W, :].astype(BF16))
    y += _dot(s_ref[...], w_ref[ATTN_W:ATTN_W + SSM_W, :].astype(BF16))
    y += _dot(r_ref[...], w_ref[ATTN_W + SSM_W:, :].astype(BF16))
    o_ref[...] = x_ref[...] + gt_ref[0] * y


def _outproj(x, attn, ssm, ret, gate, w):
    tn = 512
    return pl.pallas_call(
        _outproj_kernel,
        grid=(N_MT, D_MODEL // tn),
        in_specs=[
            pl.BlockSpec((TM, tn), lambda i, j: (i, j)),
            pl.BlockSpec((TM, ATTN_W), lambda i, j: (i, 0)),
            pl.BlockSpec((TM, SSM_W), lambda i, j: (i, 0)),
            pl.BlockSpec((TM, RET_W), lambda i, j: (i, 0)),
            pl.BlockSpec((1, 1, tn), lambda i, j: (i, 0, j)),
            pl.BlockSpec((D_MODEL, tn), lambda i, j: (0, j)),
        ],
        out_specs=pl.BlockSpec((TM, tn), lambda i, j: (i, j)),
        out_shape=jax.ShapeDtypeStruct((N_TOK, D_MODEL), F32),
        compiler_params=_cparams(("arbitrary", "arbitrary")),
        name="outproj",
    )(x, attn, ssm, ret, gate, w)


def _head_rms(x, g):
    ms = jnp.mean(x * x, axis=-1, keepdims=True)
    return x * lax.rsqrt(ms + EPS) * g


def _rope(x, cos, sin_signed):
    lane = lax.broadcasted_iota(jnp.int32, x.shape, 1)
    partner = jnp.where((lane % 64) < 32, pltpu.roll(x, 96, 1), pltpu.roll(x, 32, 1))
    return x * cos + partner * sin_signed


def _attend(qs, k, v):
    s = _dot_nt(qs.astype(BF16), k) * (HEAD_DIM ** -0.5)
    p = jnp.exp(s - jnp.max(s, axis=-1, keepdims=True))
    l = jnp.sum(p, axis=-1, keepdims=True)
    return _dot(p.astype(BF16), v) / l


def _attn_prompt_kernel(q_ref, k_ref, v_ref, qg_ref, kg_ref, o_ref, kc_ref, vc_ref):
    qg = qg_ref[...]
    kg = kg_ref[...]
    v = v_ref[...]
    vc_ref[0] = v
    for kv in range(N_KV_HEADS):
        ksl = slice(kv * HEAD_DIM, (kv + 1) * HEAD_DIM)
        kn = _head_rms(k_ref[:, ksl], kg)
        kc_ref[0, :, ksl] = kn
        qs = jnp.concatenate(
            [_head_rms(q_ref[:, (kv * Q_PER_KV + g) * HEAD_DIM:(kv * Q_PER_KV + g + 1) * HEAD_DIM], qg)
             for g in range(Q_PER_KV)], axis=0)
        o = _attend(qs, kn.astype(BF16), v[:, ksl].astype(BF16))
        for g in range(Q_PER_KV):
            h = kv * Q_PER_KV + g
            o_ref[:, h * HEAD_DIM:(h + 1) * HEAD_DIM] = o[g * SEQ:(g + 1) * SEQ].astype(BF16)


def _attn_prompt(p, qg, kg):
    return pl.pallas_call(
        _attn_prompt_kernel,
        grid=(BATCH,),
        in_specs=[
            pl.BlockSpec((SEQ, ATTN_W), lambda b: (b, 0)),
            pl.BlockSpec((SEQ, KV_W), lambda b: (b, ATTN_W // KV_W)),
            pl.BlockSpec((SEQ, KV_W), lambda b: (b, ATTN_W // KV_W + 1)),
            pl.BlockSpec((1, HEAD_DIM), lambda b: (0, 0)),
            pl.BlockSpec((1, HEAD_DIM), lambda b: (0, 0)),
        ],
        out_specs=[
            pl.BlockSpec((SEQ, ATTN_W), lambda b: (b, 0)),
            pl.BlockSpec((1, SEQ, KV_W), lambda b: (b, 0, 0)),
            pl.BlockSpec((1, SEQ, KV_W), lambda b: (b, 0, 0)),
        ],
        out_shape=[
            jax.ShapeDtypeStruct((N_PROMPT, ATTN_W), BF16),
            jax.ShapeDtypeStruct((BATCH, SEQ, KV_W), F32),
            jax.ShapeDtypeStruct((BATCH, SEQ, KV_W), F32),
        ],
        compiler_params=_cparams(("arbitrary",)),
        name="attn_prompt",
    )(p, p, p, qg.reshape(1, HEAD_DIM), kg.reshape(1, HEAD_DIM))


ATT_TQ = 256
ATT_S = PAST_LEN + DEC_SEQ


def _attn_sample_kernel(q_ref, k_ref, v_ref, ck_ref, cv_ref, qg_ref, kg_ref, cosq_ref, sinq_ref,
                        cosk_ref, sink_ref, o_ref, k_scr, v_scr):
    @pl.when(pl.program_id(1) == 0)
    def _():
        k_scr[0:PAST_LEN, :] = ck_ref[0].astype(BF16)
        v_scr[0:PAST_LEN, :] = cv_ref[0].astype(BF16)
        v_scr[PAST_LEN:, :] = v_ref[...].astype(BF16)
        for kv in range(N_KV_HEADS):
            ksl = slice(kv * HEAD_DIM, (kv + 1) * HEAD_DIM)
            kn = _rope(_head_rms(k_ref[:, ksl], kg_ref[...]), cosk_ref[...], sink_ref[...])
            k_scr[PAST_LEN:, ksl] = kn.astype(BF16)

    qg = qg_ref[...]
    cos = cosq_ref[...]
    sin = sinq_ref[...]
    for kv in range(N_KV_HEADS):
        ksl = slice(kv * HEAD_DIM, (kv + 1) * HEAD_DIM)
        qs = jnp.concatenate(
            [_rope(_head_rms(q_ref[:, (kv * Q_PER_KV + g) * HEAD_DIM:(kv * Q_PER_KV + g + 1) * HEAD_DIM], qg),
                   cos, sin) for g in range(Q_PER_KV)], axis=0)
        o = _attend(qs, k_scr[:, ksl], v_scr[:, ksl])
        for g in range(Q_PER_KV):
            h = kv * Q_PER_KV + g
            o_ref[:, h * HEAD_DIM:(h + 1) * HEAD_DIM] = o[g * ATT_TQ:(g + 1) * ATT_TQ].astype(BF16)


def _attn_sample(p, ck, cv, qg, kg, cos_t, sin_t):
    nq = DEC_SEQ // ATT_TQ
    q0 = N_PROMPT // ATT_TQ
    s0 = N_PROMPT // DEC_SEQ
    return pl.pallas_call(
        _attn_sample_kernel,
        grid=(DEC_BATCH, nq),
        in_specs=[
            pl.BlockSpec((ATT_TQ, ATTN_W), lambda b, i: (q0 + b * nq + i, 0)),
            pl.BlockSpec((DEC_SEQ, KV_W), lambda b, i: (s0 + b, ATTN_W // KV_W)),
            pl.BlockSpec((DEC_SEQ, KV_W), lambda b, i: (s0 + b, ATTN_W // KV_W + 1)),
            pl.BlockSpec((1, PAST_LEN, KV_W), lambda b, i: (b, 0, 0)),
            pl.BlockSpec((1, PAST_LEN, KV_W), lambda b, i: (b, 0, 0)),
            pl.BlockSpec((1, HEAD_DIM), lambda b, i: (0, 0)),
            pl.BlockSpec((1, HEAD_DIM), lambda b, i: (0, 0)),
            pl.BlockSpec((ATT_TQ, HEAD_DIM), lambda b, i: (i, 0)),
            pl.BlockSpec((ATT_TQ, HEAD_DIM), lambda b, i: (i, 0)),
            pl.BlockSpec((DEC_SEQ, HEAD_DIM), lambda b, i: (0, 0)),
            pl.BlockSpec((DEC_SEQ, HEAD_DIM), lambda b, i: (0, 0)),
        ],
        out_specs=pl.BlockSpec((ATT_TQ, ATTN_W), lambda b, i: (b * nq + i, 0)),
        out_shape=jax.ShapeDtypeStruct((N_SAMPLE, ATTN_W), BF16),
        scratch_shapes=[pltpu.VMEM((ATT_S, KV_W), BF16), pltpu.VMEM((ATT_S, KV_W), BF16)],
        compiler_params=_cparams(("arbitrary", "arbitrary")),
        name="attn_sample",
    )(p, p, p, ck, cv, qg.reshape(1, HEAD_DIM), kg.reshape(1, HEAD_DIM), cos_t, sin_t, cos_t, sin_t)


def _log_sigmoid(x):
    return -(jnp.maximum(-x, 0.0) + jnp.log(1.0 + jnp.exp(-jnp.abs(x))))


def _retention_kernel(q_ref, k_ref, v_ref, g_ref, dl_ref, *rest, length, rope, init):
    rest = list(rest)
    if rope:
        cos_ref, sin_ref = rest[:2]
        rest = rest[2:]
    if init:
        s0_ref = rest.pop(0)
    if init:
        o_ref, acc = rest
        sf_ref = None
    else:
        o_ref, sf_ref, acc = rest
    n = length // RET_CHUNK
    C = RET_CHUNK
    ii = lax.broadcasted_iota(jnp.int32, (C, C), 0).astype(F32)
    jj = lax.broadcasted_iota(jnp.int32, (C, C), 1).astype(F32)

    for h in range(RET_HEADS):
        hs = slice(h * HEAD_DIM, (h + 1) * HEAD_DIM)
        for d in range(2):
            lg = _log_sigmoid(dl_ref[d, h][0:1, :])
            if d == 0:
                diff = ii - jj
                q_dec = jnp.exp((ii + 1.0) * lg)
                k_dec = jnp.exp((C - 1.0 - ii) * lg)
            else:
                diff = jj - ii
                q_dec = jnp.exp((C - ii) * lg)
                k_dec = jnp.exp(ii * lg)
            inner = jnp.where(diff >= 0, jnp.exp(jnp.maximum(diff, 0.0) * lg), 0.0)
            c_dec = jnp.exp(C * lg)

            def step(t, s, d=d, hs=hs, inner=inner, q_dec=q_dec, k_dec=k_dec, c_dec=c_dec):
                c = t if d == 0 else n - 1 - t
                rows = pl.ds(pl.multiple_of(c * C, C), C)
                qc = q_ref[rows, hs]
                kc = k_ref[rows, hs] * (HEAD_DIM ** -0.5)
                vc = v_ref[rows, hs].astype(BF16)
                if rope:
                    qc = _rope(qc, cos_ref[rows, :], sin_ref[rows, :])
                    kc = _rope(kc, cos_ref[rows, :], sin_ref[rows, :])
                att = (_dot_nt(qc.astype(BF16), kc.astype(BF16)) * inner).astype(BF16)
                o = _dot(att, vc) + _dot(qc.astype(BF16), s.astype(BF16)) * q_dec
                if d == 0:
                    acc[rows, hs] = o
                else:
                    acc[rows, hs] += o
                kd = (kc * k_dec).T.astype(BF16)
                return s * c_dec + _dot(kd, vc)

            if init:
                s_init = s0_ref[0, d, h]
            else:
                s_init = jnp.zeros((HEAD_DIM, HEAD_DIM), F32)
            s_fin = lax.fori_loop(0, n, step, s_init)
            if sf_ref is not None:
                sf_ref[0, d, h] = s_fin

    for h in range(RET_HEADS):
        hs = slice(h * HEAD_DIM, (h + 1) * HEAD_DIM)
        o = acc[:, hs]
        o = o - jnp.mean(o, axis=-1, keepdims=True)
        o = o * lax.rsqrt(jnp.mean(o * o, axis=-1, keepdims=True) + EPS)
        o_ref[:, hs] = (o * _silu(g_ref[:, hs])).astype(BF16)


def _retention(p, dl, *, nseq, length, row0, rope_tabs=None, s0=None):
    rope = rope_tabs is not None
    init = s0 is not None
    b0 = row0 // length
    c0 = (ATTN_W + 2 * KV_W + SSM_W) // RET_W
    in_specs = [pl.BlockSpec((length, RET_W), lambda b, k=k: (b0 + b, c0 + k)) for k in range(4)]
    in_specs.append(pl.BlockSpec((2, RET_HEADS, 8, HEAD_DIM), lambda b: (0, 0, 0, 0)))
    args = [p, p, p, p, dl]
    if rope:
        in_specs += [pl.BlockSpec((length, HEAD_DIM), lambda b: (0, 0))] * 2
        args += list(rope_tabs)
    if init:
        in_specs.append(pl.BlockSpec((1, 2, RET_HEADS, HEAD_DIM, HEAD_DIM), lambda b: (b, 0, 0, 0, 0)))
        args.append(s0)
    out_specs = [pl.BlockSpec((length, RET_W), lambda b: (b, 0))]
    out_shape = [jax.ShapeDtypeStruct((nseq * length, RET_W), BF16)]
    if not init:
        out_specs.append(pl.BlockSpec((1, 2, RET_HEADS, HEAD_DIM, HEAD_DIM), lambda b: (b, 0, 0, 0, 0)))
        out_shape.append(jax.ShapeDtypeStruct((nseq, 2, RET_HEADS, HEAD_DIM, HEAD_DIM), F32))
    return pl.pallas_call(
        functools.partial(_retention_kernel, length=length, rope=rope, init=init),
        grid=(nseq,),
        in_specs=in_specs,
        out_specs=out_specs,
        out_shape=out_shape,
        scratch_shapes=[pltpu.VMEM((length, RET_W), F32)],
        compiler_params=_cparams(("arbitrary",)),
        name="retention_sample" if init else "retention_prompt",
    )(*args)


def _s5_disc_kernel(are_ref, aim_ref, ldt_ref, bre_ref, bim_ref, cim_ref,
                    lre_ref, lim_ref, bbre_ref, bbim_ref, ncim_ref):
    ar = are_ref[...]
    ai = aim_ref[...]
    dt = jnp.exp(ldt_ref[...])
    mag = jnp.exp(ar * dt)
    lr = mag * jnp.cos(ai * dt)
    li = mag * jnp.sin(ai * dt)
    den = ar * ar + ai * ai
    nr = lr - 1.0
    cr = (nr * ar + li * ai) / den
    ci = (li * ar - nr * ai) / den
    br = bre_ref[...]
    bi = bim_ref[...]
    lre_ref[...] = lr
    lim_ref[...] = li
    bbre_ref[...] = cr * br - ci * bi
    bbim_ref[...] = cr * bi + ci * br
    ncim_ref[...] = -cim_ref[...]


def _s5_discretize(a_re, a_im, log_dt, b_re, b_im, c_im):
    rows = 2 * SSM_GROUPS
    cols = SSM_STATE * SSM_GROUP
    shp = (2, SSM_GROUPS, SSM_STATE, SSM_GROUP)
    args = [
        jnp.broadcast_to(a_re[..., None], shp).reshape(rows, cols),
        jnp.broadcast_to(a_im[..., None], shp).reshape(rows, cols),
        jnp.broadcast_to(log_dt[..., None, None], shp).reshape(rows, cols),
        b_re.reshape(rows, cols), b_im.reshape(rows, cols), c_im.reshape(rows, cols),
    ]
    spec = pl.BlockSpec((rows, cols), lambda: (0, 0))
    outs = pl.pallas_call(
        _s5_disc_kernel,
        in_specs=[spec] * 6,
        out_specs=[spec] * 5,
        out_shape=[jax.ShapeDtypeStruct((rows, cols), F32)] * 5,
        name="s5_discretize",
    )(*args)
    lre, lim, bbre, bbim, ncim = outs
    lre = lre.reshape(shp)[..., 0]
    lim = lim.reshape(shp)[..., 0]
    return lre, lim, bbre.reshape(shp), bbim.reshape(shp), ncim.reshape(2, SSM_GROUPS, SSM_GROUP, SSM_STATE)


def _state_cols(x):
    lead = x.shape[:-3]
    x = x.reshape(lead + (N_STILE, 2, SSM_STATE, 2))
    x = jnp.moveaxis(x, -1, -3)
    return x.reshape(lead + (STATE_W,))


def _state_uncols(x):
    lead = x.shape[:-1]
    x = x.reshape(lead + (N_STILE, 2, 2, SSM_STATE))
    x = jnp.moveaxis(x, -3, -1)
    return x.reshape(lead + (SSM_GROUPS, SSM_STATE, 2))


def _s5_mats(lre, lim, bbre, bbim, c_re, ncim):
    eye_t = jnp.eye(N_STILE, dtype=F32)
    eye_s = jnp.eye(2, dtype=F32)
    bb = jnp.stack([bbre, bbim], axis=-1)
    bb = bb.reshape(2, N_STILE, 2, SSM_STATE, SSM_GROUP, 2)
    bb = jnp.transpose(bb, (0, 1, 2, 4, 5, 3))
    bm = (bb[:, :, :, :, None, :, None, :]
          * eye_t[None, :, None, None, :, None, None, None]
          * eye_s[None, None, :, None, None, None, :, None])
    bm = bm.reshape(2, SSM_W, STATE_W).astype(BF16)
    cc = jnp.stack([c_re, ncim], axis=-1)
    cc = cc.reshape(2, N_STILE, 2, SSM_GROUP, SSM_STATE, 2)
    cc = jnp.transpose(cc, (0, 1, 5, 2, 4, 3))
    cm = (cc[:, :, :, :, :, None, None, :]
          * eye_t[None, :, None, None, None, :, None, None]
          * eye_s[None, None, None, :, None, None, :, None])
    cm = cm.reshape(2, STATE_W, SSM_W).astype(BF16)
    lam = jnp.concatenate([lre.reshape(2, N_STILE, 128), lim.reshape(2, N_STILE, 128)], axis=1)
    return bm, cm, lam


def _s5_scan_kernel(u_ref, bm_ref, cm_ref, lam_ref, h0_ref, y_ref, hf_ref, hbuf, state, ends, *, segments):
    npass = 1 if segments == 1 else 2
    ps = pl.program_id(1)
    ck = pl.program_id(2)
    nck = pl.num_programs(2)
    rows = SSM_TC * SSM_ROWS
    lam = lam_ref[0]

    @pl.when(ck == 0)
    def _():
        if npass == 1:
            state[...] = h0_ref[0]
        else:
            @pl.when(ps == 0)
            def _():
                state[...] = jnp.zeros_like(state)

            @pl.when(ps == 1)
            def _():
                ends[...] = state[...]
                state[...] = h0_ref[0]
                lr = lam[0:N_STILE]
                li = lam[N_STILE:]
                for _ in range(int(math.log2(DEC_SEQ // SSM_SEGS))):
                    lr, li = lr * lr - li * li, 2.0 * lr * li
                for sg in range(1, segments):
                    for sq in range(SSM_ROWS // segments):
                        r = sq * segments + sg
                        for j in range(N_STILE):
                            re = slice(j * 256, j * 256 + 128)
                            im = slice(j * 256 + 128, (j + 1) * 256)
                            pr = state[r - 1:r, re]
                            pi = state[r - 1:r, im]
                            ar = lr[j:j + 1]
                            ai = li[j:j + 1]
                            state[r:r + 1, re] = ends[r - 1:r, re] + ar * pr - ai * pi
                            state[r:r + 1, im] = ends[r - 1:r, im] + ar * pi + ai * pr

    u = u_ref[0].reshape(rows, SSM_W).astype(BF16)
    for j in range(N_STILE):
        k0 = (j // 8) * 256
        hbuf[:, j * 256:(j + 1) * 256] = _dot(u[:, k0:k0 + 256], bm_ref[0, k0:k0 + 256, j * 256:(j + 1) * 256])

    JT = 4
    for jb in range(N_STILE // JT):
        tiles = list(range(jb * JT, (jb + 1) * JT))
        lrs = [jnp.broadcast_to(lam[j:j + 1], (SSM_ROWS, 128)) for j in tiles]
        lis = [jnp.broadcast_to(lam[N_STILE + j:N_STILE + j + 1], (SSM_ROWS, 128)) for j in tiles]
        init = tuple(state[:, j * 256:j * 256 + 128] for j in tiles) + \
            tuple(state[:, j * 256 + 128:(j + 1) * 256] for j in tiles)

        def body(t, carry, tiles=tiles, lrs=lrs, lis=lis):
            r = pl.ds(pl.multiple_of(t * SSM_ROWS, SSM_ROWS), SSM_ROWS)
            out_r, out_i = [], []
            for n_, j in enumerate(tiles):
                hr, hi = carry[n_], carry[JT + n_]
                re = slice(j * 256, j * 256 + 128)
                im = slice(j * 256 + 128, (j + 1) * 256)
                nr = lrs[n_] * hr - lis[n_] * hi + hbuf[r, re]
                ni = lrs[n_] * hi + lis[n_] * hr + hbuf[r, im]
                hbuf[r, re] = nr
                hbuf[r, im] = ni
                out_r.append(nr)
                out_i.append(ni)
            return tuple(out_r) + tuple(out_i)

        fin = lax.fori_loop(0, SSM_TC, body, init, unroll=2)
        for n_, j in enumerate(tiles):
            state[:, j * 256:j * 256 + 128] = fin[n_]
            state[:, j * 256 + 128:(j + 1) * 256] = fin[JT + n_]

    @pl.when(ps == npass - 1)
    def _():
        hb = hbuf[...].astype(BF16)
        half = STATE_W // 2
        y = jnp.concatenate(
            [_dot(hb[:, n_ * half:(n_ + 1) * half], cm_ref[0, n_ * half:(n_ + 1) * half, n_ * 256:(n_ + 1) * 256])
             for n_ in range(2)], axis=1)
        y_ref[0] = y.reshape(SSM_TC, SSM_ROWS, SSM_W)

    @pl.when(jnp.logical_and(ps == npass - 1, ck == nck - 1))
    def _():
        hf_ref[0] = state[...]


def _s5_scan(u, bm, cm, lam, h0, *, segments):
    steps = u.shape[1]
    nck = steps // SSM_TC
    npass = 1 if segments == 1 else 2
    return pl.pallas_call(
        functools.partial(_s5_scan_kernel, segments=segments),
        grid=(2, npass, nck),
        in_specs=[
            pl.BlockSpec((1, SSM_TC, SSM_ROWS, SSM_W), lambda d, p, c: (d, c, 0, 0)),
            pl.BlockSpec((1, SSM_W, STATE_W), lambda d, p, c: (d, 0, 0)),
            pl.BlockSpec((1, STATE_W, SSM_W), lambda d, p, c: (d, 0, 0)),
            pl.BlockSpec((1, 2 * N_STILE, 128), lambda d, p, c: (d, 0, 0)),
            pl.BlockSpec((1, SSM_ROWS, STATE_W), lambda d, p, c: (d, 0, 0)),
        ],
        out_specs=[
            pl.BlockSpec((1, SSM_TC, SSM_ROWS, SSM_W), lambda d, p, c: (d, c * p if npass == 2 else c, 0, 0)),
            pl.BlockSpec((1, SSM_ROWS, STATE_W), lambda d, p, c: (d, 0, 0)),
        ],
        out_shape=[
            jax.ShapeDtypeStruct((2, steps, SSM_ROWS, SSM_W), F32),
            jax.ShapeDtypeStruct((2, SSM_ROWS, STATE_W), F32),
        ],
        scratch_shapes=[
            pltpu.VMEM((SSM_TC * SSM_ROWS, STATE_W), F32),
            pltpu.VMEM((SSM_ROWS, STATE_W), F32),
            pltpu.VMEM((SSM_ROWS, STATE_W), F32),
        ],
        compiler_params=_cparams(("arbitrary", "arbitrary", "arbitrary")),
        name="s5_scan_seg%d" % segments,
    )(u, bm, cm, lam, h0)


def _s5_glu_kernel(yf_ref, yb_ref, u_ref, d_ref, w_ref, o_ref):
    y = yf_ref[...] + yb_ref[...] + d_ref[...] * u_ref[...]
    z = _dot(jax.nn.gelu(y).astype(BF16), w_ref[...].astype(BF16))
    o_ref[...] = (z[:, :SSM_W] * _sigmoid(z[:, SSM_W:])).astype(BF16)


def _s5_glu(yf, yb, p, d, w):
    c0 = (ATTN_W + 2 * KV_W) // SSM_W
    tok = pl.BlockSpec((TM, SSM_W), lambda i: (i, 0))
    return pl.pallas_call(
        _s5_glu_kernel,
        grid=(N_MT,),
        in_specs=[
            tok, tok,
            pl.BlockSpec((TM, SSM_W), lambda i: (i, c0)),
            pl.BlockSpec((1, SSM_W), lambda i: (0, 0)),
            pl.BlockSpec((SSM_W, 2 * SSM_W), lambda i: (0, 0)),
        ],
        out_specs=tok,
        out_shape=jax.ShapeDtypeStruct((N_TOK, SSM_W), BF16),
        compiler_params=_cparams(("arbitrary",)),
        name="s5_glu",
    )(yf, yb, p, d.reshape(1, SSM_W), w)


def _s5_layer(p, prm, st_l):
    lre, lim, bbre, bbim, ncim = _s5_discretize(prm["a_re"], prm["a_im"], prm["log_dt"],
                                                prm["b_re"], prm["b_im"], prm["c_im"])
    bm, cm, lam = _s5_mats(lre, lim, bbre, bbim, prm["c_re"], ncim)
    c0 = ATTN_W + 2 * KV_W
    u = p[:, c0:c0 + SSM_W]
    up = jnp.transpose(u[:N_PROMPT].reshape(BATCH, SEQ, SSM_W), (1, 0, 2))
    up = jnp.stack([up, up[::-1]])
    yp, hfin = _s5_scan(up, bm, cm, lam, jnp.zeros((2, SSM_ROWS, STATE_W), F32), segments=1)
    ypf = jnp.transpose(yp[0], (1, 0, 2)).reshape(N_PROMPT, SSM_W)
    ypb = jnp.transpose(yp[1, ::-1], (1, 0, 2)).reshape(N_PROMPT, SSM_W)
    seg = DEC_SEQ // SSM_SEGS
    us = u[N_PROMPT:].reshape(DEC_BATCH, DEC_SEQ, SSM_W)

    def to_rows(x):
        x = x.reshape(DEC_BATCH, SSM_SEGS, seg, SSM_W)
        return jnp.transpose(x, (2, 0, 1, 3)).reshape(seg, SSM_ROWS, SSM_W)

    def from_rows(x):
        x = x.reshape(seg, DEC_BATCH, SSM_SEGS, SSM_W)
        return jnp.transpose(x, (1, 2, 0, 3)).reshape(DEC_BATCH, DEC_SEQ, SSM_W)

    us2 = jnp.stack([to_rows(us), to_rows(us[:, ::-1])])
    h0 = _state_cols(jnp.moveaxis(st_l, 0, 1))
    h0 = jnp.zeros((2, DEC_BATCH, SSM_SEGS, STATE_W), F32).at[:, :, 0].set(h0).reshape(2, SSM_ROWS, STATE_W)
    ys, _ = _s5_scan(us2, bm, cm, lam, h0, segments=SSM_SEGS)
    ysf = from_rows(ys[0]).reshape(N_SAMPLE, SSM_W)
    ysb = from_rows(ys[1])[:, ::-1].reshape(N_SAMPLE, SSM_W)
    yf = jnp.concatenate([ypf, ysf], axis=0)
    yb = jnp.concatenate([ypb, ysb], axis=0)
    out = _s5_glu(yf, yb, p, prm["d"], prm["w_glu"])
    new_state = jnp.moveaxis(_state_uncols(hfin), 0, 1)
    return out, new_state


def _rope_tables():
    rows = DEC_SEQ // GRID_W
    r = jnp.repeat(jnp.arange(rows), GRID_W).astype(F32)
    col = jnp.tile(jnp.arange(GRID_W), rows).astype(F32)
    inv = ROPE_BASE ** (-jnp.arange(ROPE_FREQS, dtype=F32) / ROPE_FREQS)
    ang = jnp.stack([r, col], axis=-1)[:, :, None] * inv
    cos = jnp.cos(ang)
    sin = jnp.sin(ang)
    cos_t = jnp.concatenate([cos, cos], axis=-1).reshape(DEC_SEQ, HEAD_DIM)
    sin_t = jnp.concatenate([-sin, sin], axis=-1).reshape(DEC_SEQ, HEAD_DIM)
    return cos_t, sin_t


def kernel(x_prompt, x_sample, cache_k, cache_v, state_ssm, state_ret, c, c_ctx, w_mod, b_mod, norm_g,
           w_ffn_in, w_ffn_out, w_in, w_out, q_norm_g, k_norm_g, ssm_a_re, ssm_a_im, ssm_log_dt,
           ssm_b_re, ssm_b_im, ssm_c_re, ssm_c_im, ssm_d, w_ssm_glu, ret_decay_logit, final_norm_g):
    x = jnp.concatenate([x_prompt.reshape(N_PROMPT, D_MODEL), x_sample.reshape(N_SAMPLE, D_MODEL)], axis=0)
    cond8 = jnp.concatenate([c_ctx[None], c, jnp.zeros((5, D_MODEL), F32)], axis=0)
    mods = _adaln(cond8, w_mod, b_mod).reshape(DEPTH, 8, N_SUB, 3, D_MODEL)
    tile_mod = jnp.array(TILE_MOD, jnp.int32)
    rope_tabs = _rope_tables()
    ks_, vs_, hs_, ss_ = [], [], [], []
    for l in range(DEPTH):
        def mod(sub, kind, l=l):
            return mods[l, :, sub, kind][tile_mod][:, None, :]

        x = _ffn(x, norm_g[l, 0], mod(0, 0), mod(0, 1), mod(0, 2), w_ffn_in[l, 0], w_ffn_out[l, 0])
        p = _inproj(x, norm_g[l, 1], mod(1, 0), mod(1, 1), w_in[l])
        attn_p, k_l, v_l = _attn_prompt(p, q_norm_g[l], k_norm_g[l])
        attn_s = _attn_sample(p, cache_k[:, l].reshape(DEC_BATCH, PAST_LEN, KV_W),
                              cache_v[:, l].reshape(DEC_BATCH, PAST_LEN, KV_W),
                              q_norm_g[l], k_norm_g[l], *rope_tabs)
        prm = {"a_re": ssm_a_re[l], "a_im": ssm_a_im[l], "log_dt": ssm_log_dt[l], "b_re": ssm_b_re[l],
               "b_im": ssm_b_im[l], "c_re": ssm_c_re[l], "c_im": ssm_c_im[l], "d": ssm_d[l],
               "w_glu": w_ssm_glu[l]}
        ssm, h_l = _s5_layer(p, prm, state_ssm[:, l])
        dl = jnp.broadcast_to(ret_decay_logit[l][:, :, None, None], (2, RET_HEADS, 8, HEAD_DIM))
        ret_p, s_l = _retention(p, dl, nseq=BATCH, length=SEQ, row0=0)
        (ret_s,) = _retention(p, dl, nseq=DEC_BATCH, length=DEC_SEQ, row0=N_PROMPT, rope_tabs=rope_tabs,
                              s0=state_ret[:, l])
        attn = jnp.concatenate([attn_p, attn_s], axis=0)
        ret = jnp.concatenate([ret_p, ret_s], axis=0)
        x = _outproj(x, attn, ssm, ret, mod(1, 2), w_out[l])
        x = _ffn(x, norm_g[l, 2], mod(2, 0), mod(2, 1), mod(2, 2), w_ffn_in[l, 1], w_ffn_out[l, 1],
                 final_g=final_norm_g if l == DEPTH - 1 else None)
        ks_.append(k_l.reshape(BATCH, SEQ, N_KV_HEADS, HEAD_DIM))
        vs_.append(v_l.reshape(BATCH, SEQ, N_KV_HEADS, HEAD_DIM))
        hs_.append(h_l)
        ss_.append(s_l)
    y_prompt = x[:N_PROMPT].reshape(BATCH, SEQ, D_MODEL)
    y_sample = x[N_PROMPT:].reshape(DEC_BATCH, DEC_SEQ, D_MODEL)
    return (y_prompt, y_sample, jnp.stack(ks_, axis=1), jnp.stack(vs_, axis=1),
            jnp.stack(hs_, axis=1), jnp.stack(ss_, axis=1))
```

```python
import functools
import math

import jax
import jax.numpy as jnp
from jax import lax
from jax.experimental import pallas as pl
from jax.experimental.pallas import tpu as pltpu

D_MODEL = 2048
BATCH = 16
SEQ = 256
DEPTH = 2
DEC_BATCH = 2
DEC_SEQ = 1024
PAST_LEN = 512
GRID_W = 64
HEAD_DIM = 128
N_Q_HEADS = 8
N_KV_HEADS = 2
Q_PER_KV = 4
ATTN_W = 1024
KV_W = 256
SSM_W = 512
SSM_GROUP = 16
SSM_GROUPS = 32
SSM_STATE = 64
RET_HEADS = 4
RET_W = 512
IN_W = 4096
D_FF = 5632
N_SUB = 3
RET_CHUNK = 128
ROPE_BASE = 10000.0
ROPE_FREQS = 32
EPS = 1e-6

N_PROMPT = BATCH * SEQ
N_SAMPLE = DEC_BATCH * DEC_SEQ
N_TOK = N_PROMPT + N_SAMPLE
TM = 1024
N_MT = N_TOK // TM
N_PT = N_PROMPT // TM
SEQ_PER_TILE = TM // SEQ
TILE_MOD = (0, 0, 0, 0, 1, 2)
TF = 256
TN = 512
STATE_W = 2 * SSM_GROUPS * SSM_STATE
N_STILE = STATE_W // 256
SSM_ROWS = 16
SSM_SEGS = 8
SSM_SEG_LEN = DEC_SEQ // SSM_SEGS
SSM_TC = 32
ATT_TQ = 256
ATT_S = PAST_LEN + DEC_SEQ
VMEM_LIMIT = 56 * 1024 * 1024

COL_K = ATTN_W // KV_W
COL_V = COL_K + 1
COL_U = (ATTN_W + 2 * KV_W) // SSM_W
COL_R = (ATTN_W + 2 * KV_W + SSM_W) // RET_W

BF16 = jnp.bfloat16
F32 = jnp.float32


def _cparams(sem):
    return pltpu.CompilerParams(dimension_semantics=sem, vmem_limit_bytes=VMEM_LIMIT)


def _dot(a, b):
    return jnp.dot(a, b, preferred_element_type=F32)


def _dot_nt(a, b):
    return lax.dot_general(a, b, (((1,), (1,)), ((), ())), preferred_element_type=F32)


def _sigmoid(x):
    return 1.0 / (1.0 + jnp.exp(-x))


def _silu(x):
    return x * _sigmoid(x)


def _sample_idx(i):
    return jnp.clip(i - N_PT, 0, DEC_BATCH - 1)


def _prompt_idx(i):
    return jnp.minimum(i, N_PT - 1)


def _adaln_kernel(c_ref, w_ref, b_ref, o_ref):
    a = _silu(c_ref[...]).astype(BF16)
    o_ref[0] = _dot(a, w_ref[0].astype(BF16)) + b_ref[0]


def _adaln(cond8, w_mod, b_mod):
    n = w_mod.shape[-1]
    tn = 1024
    return pl.pallas_call(
        _adaln_kernel,
        grid=(DEPTH, n // tn),
        in_specs=[
            pl.BlockSpec((8, D_MODEL), lambda l, j: (0, 0)),
            pl.BlockSpec((1, D_MODEL, tn), lambda l, j: (l, 0, j)),
            pl.BlockSpec((1, 1, tn), lambda l, j: (l, 0, j)),
        ],
        out_specs=pl.BlockSpec((1, 8, tn), lambda l, j: (l, 0, j)),
        out_shape=jax.ShapeDtypeStruct((DEPTH, 8, n), F32),
        compiler_params=_cparams(("arbitrary", "arbitrary")),
        name="adaln",
    )(cond8, w_mod, b_mod.reshape(DEPTH, 1, n))


def _norm_mod(x, g, shift, scale):
    ms = jnp.mean(x * x, axis=-1, keepdims=True)
    y = x * lax.rsqrt(ms + EPS) * g
    return y * (1.0 + scale) + shift


def _ffn_kernel(x_ref, g_ref, sh_ref, sc_ref, gt_ref, wa_ref, wu_ref, wo_ref, *rest, final):
    if final:
        fg_ref, o_ref, h_scr = rest
    else:
        o_ref, h_scr = rest
    j = pl.program_id(1)

    @pl.when(j == 0)
    def _():
        h = _norm_mod(x_ref[...], g_ref[...], sh_ref[0], sc_ref[0])
        h_scr[...] = h.astype(BF16)
        o_ref[...] = jnp.zeros_like(o_ref)

    h = h_scr[...]
    a = _dot(h, wa_ref[...].astype(BF16))
    u = _dot(h, wu_ref[...].astype(BF16))
    mid = (_silu(a) * u).astype(BF16)
    o_ref[...] += _dot(mid, wo_ref[...].astype(BF16))

    @pl.when(j == pl.num_programs(1) - 1)
    def _():
        out = x_ref[...] + (0.5 * gt_ref[0]) * o_ref[...]
        if final:
            ms = jnp.mean(out * out, axis=-1, keepdims=True)
            out = out * lax.rsqrt(ms + EPS) * fg_ref[...]
        o_ref[...] = out


def _ffn(x, g, shift, scale, gate, w_in, w_out, l, f, final_g=None):
    final = final_g is not None
    nf = D_FF // TF
    tok = pl.BlockSpec((TM, D_MODEL), lambda i, j: (i, 0))
    tok_in = pl.BlockSpec((TM, D_MODEL), lambda i, j: (i, 0), pipeline_mode=pl.Buffered(1))
    vec = pl.BlockSpec((1, D_MODEL), lambda i, j: (0, 0))
    mod = pl.BlockSpec((1, 1, D_MODEL), lambda i, j: (i, 0, 0))
    in_specs = [
        tok_in, vec, mod, mod, mod,
        pl.BlockSpec((None, None, D_MODEL, TF), lambda i, j: (l, f, 0, j)),
        pl.BlockSpec((None, None, D_MODEL, TF), lambda i, j: (l, f, 0, j + nf)),
        pl.BlockSpec((None, None, TF, D_MODEL), lambda i, j: (l, f, j, 0)),
    ]
    args = [x, g.reshape(1, D_MODEL), shift, scale, gate, w_in, w_in, w_out]
    if final:
        in_specs.append(vec)
        args.append(final_g.reshape(1, D_MODEL))
    return pl.pallas_call(
        functools.partial(_ffn_kernel, final=final),
        grid=(N_MT, nf),
        in_specs=in_specs,
        out_specs=tok,
        out_shape=jax.ShapeDtypeStruct((N_TOK, D_MODEL), F32),
        scratch_shapes=[pltpu.VMEM((TM, D_MODEL), BF16)],
        compiler_params=_cparams(("arbitrary", "arbitrary")),
        name="ffn_final" if final else "ffn",
    )(*args)


def _resident_weight_spec(rows, l, ncol):
    return pl.BlockSpec((None, rows, TN), lambda i, j: (l, 0, jnp.where(i == 0, j, ncol - 1)))


def _inproj_kernel(x_ref, g_ref, sh_ref, sc_ref, w_ref, o_ref, h_scr, w_scr):
    i = pl.program_id(0)
    j = pl.program_id(1)

    @pl.when(j == 0)
    def _():
        h = _norm_mod(x_ref[...], g_ref[...], sh_ref[0], sc_ref[0])
        h_scr[...] = h.astype(BF16)

    @pl.when(i == 0)
    def _():
        w_scr[j] = w_ref[...].astype(BF16)

    o_ref[...] = _dot(h_scr[...], w_scr[j])


def _inproj(x, g, shift, scale, w, l):
    ncol = IN_W // TN
    mod = pl.BlockSpec((1, 1, D_MODEL), lambda i, j: (i, 0, 0))
    return pl.pallas_call(
        _inproj_kernel,
        grid=(N_MT, ncol),
        in_specs=[
            pl.BlockSpec((TM, D_MODEL), lambda i, j: (i, 0)),
            pl.BlockSpec((1, D_MODEL), lambda i, j: (0, 0)),
            mod, mod,
            _resident_weight_spec(D_MODEL, l, ncol),
        ],
        out_specs=pl.BlockSpec((TM, TN), lambda i, j: (i, j)),
        out_shape=jax.ShapeDtypeStruct((N_TOK, IN_W), F32),
        scratch_shapes=[pltpu.VMEM((TM, D_MODEL), BF16), pltpu.VMEM((ncol, D_MODEL, TN), BF16)],
        compiler_params=_cparams(("arbitrary", "arbitrary")),
        name="inproj",
    )(x, g.reshape(1, D_MODEL), shift, scale, w)


def _outproj_kernel(x_ref, a_ref, s_ref, r_ref, gt_ref, w_ref, o_ref, w_scr):
    i = pl.program_id(0)
    j = pl.program_id(1)

    @pl.when(i == 0)
    def _():
        w_scr[j] = w_ref[...].astype(BF16)

    y = _dot(a_ref[...], w_scr[j, 0:ATTN_W, :])
    y += _dot(s_ref[...], w_scr[j, ATTN_W:ATTN_W + SSM_W, :])
    y += _dot(r_ref[...], w_scr[j, ATTN_W + SSM_W:, :])
    o_ref[...] = x_ref[...] + gt_ref[0] * y


def _outproj(x, attn, ssm, ret, gate, w, l):
    ncol = D_MODEL // TN
    return pl.pallas_call(
        _outproj_kernel,
        grid=(N_MT, ncol),
        in_specs=[
            pl.BlockSpec((TM, TN), lambda i, j: (i, j)),
            pl.BlockSpec((TM, ATTN_W), lambda i, j: (i, 0)),
            pl.BlockSpec((TM, SSM_W), lambda i, j: (i, 0)),
            pl.BlockSpec((TM, RET_W), lambda i, j: (i, 0)),
            pl.BlockSpec((1, 1, TN), lambda i, j: (i, 0, j)),
            _resident_weight_spec(D_MODEL, l, ncol),
        ],
        out_specs=pl.BlockSpec((TM, TN), lambda i, j: (i, j)),
        out_shape=jax.ShapeDtypeStruct((N_TOK, D_MODEL), F32),
        scratch_shapes=[pltpu.VMEM((ncol, D_MODEL, TN), BF16)],
        compiler_params=_cparams(("arbitrary", "arbitrary")),
        name="outproj",
    )(x, attn, ssm, ret, gate, w)


def _head_rms(x, g):
    ms = jnp.mean(x * x, axis=-1, keepdims=True)
    return x * lax.rsqrt(ms + EPS) * g


def _rope(x, cos, sin_signed):
    lane = lax.broadcasted_iota(jnp.int32, x.shape, 1)
    partner = jnp.where((lane % 64) < 32, pltpu.roll(x, 96, 1), pltpu.roll(x, 32, 1))
    return x * cos + partner * sin_signed


def _attend(qs, k, v):
    s = _dot_nt(qs.astype(BF16), k) * (HEAD_DIM ** -0.5)
    p = jnp.exp(s - jnp.max(s, axis=-1, keepdims=True))
    l = jnp.sum(p, axis=-1, keepdims=True)
    return _dot(p.astype(BF16), v) / l


def _attn_kernel(q_ref, k_ref, v_ref, ck_ref, cv_ref, qg_ref, kg_ref, cos_ref, sin_ref,
                 o_ref, kc_ref, vc_ref, k_scr, v_scr):
    i = pl.program_id(0)
    qg = qg_ref[...]
    kg = kg_ref[...]

    def heads(kv):
        return [slice((kv * Q_PER_KV + g) * HEAD_DIM, (kv * Q_PER_KV + g + 1) * HEAD_DIM) for g in range(Q_PER_KV)]

    @pl.when(i < N_PT)
    def _():
        def seq_body(sq, carry):
            rows = pl.ds(pl.multiple_of(sq * SEQ, SEQ), SEQ)
            v = v_ref[rows, :]
            vc_ref[sq] = v
            for kv in range(N_KV_HEADS):
                ksl = slice(kv * HEAD_DIM, (kv + 1) * HEAD_DIM)
                kn = _head_rms(k_ref[rows, ksl], kg)
                kc_ref[sq, :, ksl] = kn
                qs = jnp.concatenate([_head_rms(q_ref[rows, hs], qg) for hs in heads(kv)], axis=0)
                o = _attend(qs, kn.astype(BF16), v[:, ksl].astype(BF16))
                for g, hs in enumerate(heads(kv)):
                    o_ref[rows, hs] = o[g * SEQ:(g + 1) * SEQ].astype(BF16)
            return carry

        lax.fori_loop(0, SEQ_PER_TILE, seq_body, 0)

    @pl.when(i >= N_PT)
    def _():
        k_scr[0:PAST_LEN, :] = ck_ref[...].astype(BF16)
        v_scr[0:PAST_LEN, :] = cv_ref[...].astype(BF16)
        v_scr[PAST_LEN:, :] = v_ref[...].astype(BF16)
        for kv in range(N_KV_HEADS):
            ksl = slice(kv * HEAD_DIM, (kv + 1) * HEAD_DIM)
            kn = _rope(_head_rms(k_ref[:, ksl], kg), cos_ref[...], sin_ref[...])
            k_scr[PAST_LEN:, ksl] = kn.astype(BF16)
        def q_body(qb, carry):
            rows = pl.ds(pl.multiple_of(qb * ATT_TQ, ATT_TQ), ATT_TQ)
            cos = cos_ref[rows, :]
            sin = sin_ref[rows, :]
            for kv in range(N_KV_HEADS):
                ksl = slice(kv * HEAD_DIM, (kv + 1) * HEAD_DIM)
                qs = jnp.concatenate([_rope(_head_rms(q_ref[rows, hs], qg), cos, sin) for hs in heads(kv)], axis=0)
                o = _attend(qs, k_scr[:, ksl], v_scr[:, ksl])
                for g, hs in enumerate(heads(kv)):
                    o_ref[rows, hs] = o[g * ATT_TQ:(g + 1) * ATT_TQ].astype(BF16)
            return carry

        lax.fori_loop(0, DEC_SEQ // ATT_TQ, q_body, 0)


def _attention(p, cache_k, cache_v, qg, kg, cos_t, sin_t, l):
    cache_spec = pl.BlockSpec((None, None, PAST_LEN, KV_W), lambda i: (_sample_idx(i), l, 0, 0))
    vec = pl.BlockSpec((1, HEAD_DIM), lambda i: (0, 0))
    tab = pl.BlockSpec((DEC_SEQ, HEAD_DIM), lambda i: (0, 0))
    new_cache = pl.BlockSpec((SEQ_PER_TILE, SEQ, KV_W), lambda i: (_prompt_idx(i), 0, 0))
    return pl.pallas_call(
        _attn_kernel,
        grid=(N_MT,),
        in_specs=[
            pl.BlockSpec((TM, ATTN_W), lambda i: (i, 0)),
            pl.BlockSpec((TM, KV_W), lambda i: (i, COL_K)),
            pl.BlockSpec((TM, KV_W), lambda i: (i, COL_V)),
            cache_spec, cache_spec, vec, vec, tab, tab,
        ],
        out_specs=[pl.BlockSpec((TM, ATTN_W), lambda i: (i, 0)), new_cache, new_cache],
        out_shape=[
            jax.ShapeDtypeStruct((N_TOK, ATTN_W), BF16),
            jax.ShapeDtypeStruct((BATCH, SEQ, KV_W), F32),
            jax.ShapeDtypeStruct((BATCH, SEQ, KV_W), F32),
        ],
        scratch_shapes=[pltpu.VMEM((ATT_S, KV_W), BF16), pltpu.VMEM((ATT_S, KV_W), BF16)],
        compiler_params=_cparams(("arbitrary",)),
        name="attention",
    )(p, p, p, cache_k.reshape(DEC_BATCH, DEPTH, PAST_LEN, KV_W), cache_v.reshape(DEC_BATCH, DEPTH, PAST_LEN, KV_W),
      qg.reshape(1, HEAD_DIM), kg.reshape(1, HEAD_DIM), cos_t, sin_t)


def _log_sigmoid(x):
    return -(jnp.maximum(-x, 0.0) + jnp.log(1.0 + jnp.exp(-jnp.abs(x))))


def _retention_kernel(q_ref, k_ref, v_ref, g_ref, dl_ref, cos_ref, sin_ref, s0_ref, o_ref, sf_ref, acc):
    i = pl.program_id(0)
    C = RET_CHUNK
    ii = lax.broadcasted_iota(jnp.int32, (C, C), 0).astype(F32)
    jj = lax.broadcasted_iota(jnp.int32, (C, C), 1).astype(F32)

    decs = {}
    for d in range(2):
        for h in range(RET_HEADS):
            lg = _log_sigmoid(dl_ref[d, h][0:1, :])
            if d == 0:
                diff = ii - jj
                q_dec = jnp.exp((ii + 1.0) * lg)
                k_dec = jnp.exp((C - 1.0 - ii) * lg)
            else:
                diff = jj - ii
                q_dec = jnp.exp((C - ii) * lg)
                k_dec = jnp.exp(ii * lg)
            inner = jnp.where(diff >= 0, jnp.exp(jnp.maximum(diff, 0.0) * lg), 0.0)
            decs[d, h] = (inner, q_dec, k_dec, jnp.exp(C * lg))

    def scan_seq(row0, n, rope, init):
        states = {(d, h): init(d, h) for d in range(2) for h in range(RET_HEADS)}
        for t in range(n):
            for d in range(2):
                c = t if d == 0 else n - 1 - t
                rows = slice(row0 + c * C, row0 + (c + 1) * C)
                for h in range(RET_HEADS):
                    hs = slice(h * HEAD_DIM, (h + 1) * HEAD_DIM)
                    inner, q_dec, k_dec, c_dec = decs[d, h]
                    qc = q_ref[rows, hs]
                    kc = k_ref[rows, hs] * (HEAD_DIM ** -0.5)
                    vc = v_ref[rows, hs].astype(BF16)
                    if rope:
                        qc = _rope(qc, cos_ref[rows, :], sin_ref[rows, :])
                        kc = _rope(kc, cos_ref[rows, :], sin_ref[rows, :])
                    s = states[d, h]
                    att = (_dot_nt(qc.astype(BF16), kc.astype(BF16)) * inner).astype(BF16)
                    o = _dot(att, vc) + _dot(qc.astype(BF16), s.astype(BF16)) * q_dec
                    if 2 * t < n:
                        acc[rows, hs] = o
                    else:
                        acc[rows, hs] += o
                    kd = (kc * k_dec).T.astype(BF16)
                    states[d, h] = s * c_dec + _dot(kd, vc)
        return states

    @pl.when(i < N_PT)
    def _():
        for sq in range(SEQ_PER_TILE):
            fin = scan_seq(sq * SEQ, SEQ // C, False, lambda d, h: jnp.zeros((HEAD_DIM, HEAD_DIM), F32))
            for (d, h), s in fin.items():
                sf_ref[sq, d, h] = s

    @pl.when(i >= N_PT)
    def _():
        scan_seq(0, DEC_SEQ // C, True, lambda d, h: s0_ref[d, h])

    for h in range(RET_HEADS):
        hs = slice(h * HEAD_DIM, (h + 1) * HEAD_DIM)
        o = acc[:, hs]
        o = o - jnp.mean(o, axis=-1, keepdims=True)
        o = o * lax.rsqrt(jnp.mean(o * o, axis=-1, keepdims=True) + EPS)
        o_ref[:, hs] = (o * _silu(g_ref[:, hs])).astype(BF16)


def _retention(p, dl, cos_t, sin_t, state_ret, l):
    tab = pl.BlockSpec((DEC_SEQ, HEAD_DIM), lambda i: (0, 0))
    st = (2, RET_HEADS, HEAD_DIM, HEAD_DIM)
    return pl.pallas_call(
        _retention_kernel,
        grid=(N_MT,),
        in_specs=[pl.BlockSpec((TM, RET_W), lambda i, k=k: (i, COL_R + k)) for k in range(4)] + [
            pl.BlockSpec((2, RET_HEADS, 8, HEAD_DIM), lambda i: (0, 0, 0, 0)),
            tab, tab,
            pl.BlockSpec((None, None) + st, lambda i: (_sample_idx(i), l, 0, 0, 0, 0)),
        ],
        out_specs=[
            pl.BlockSpec((TM, RET_W), lambda i: (i, 0)),
            pl.BlockSpec((SEQ_PER_TILE,) + st, lambda i: (_prompt_idx(i), 0, 0, 0, 0)),
        ],
        out_shape=[
            jax.ShapeDtypeStruct((N_TOK, RET_W), BF16),
            jax.ShapeDtypeStruct((BATCH,) + st, F32),
        ],
        scratch_shapes=[pltpu.VMEM((TM, RET_W), F32)],
        compiler_params=_cparams(("arbitrary",)),
        name="retention",
    )(p, p, p, p, dl, cos_t, sin_t, state_ret)


def _s5_disc_kernel(are_ref, aim_ref, ldt_ref, bre_ref, bim_ref, cim_ref,
                    lre_ref, lim_ref, bbre_ref, bbim_ref, ncim_ref):
    ar = are_ref[...]
    ai = aim_ref[...]
    dt = jnp.exp(ldt_ref[...])
    mag = jnp.exp(ar * dt)
    lr = mag * jnp.cos(ai * dt)
    li = mag * jnp.sin(ai * dt)
    den = ar * ar + ai * ai
    nr = lr - 1.0
    cr = (nr * ar + li * ai) / den
    ci = (li * ar - nr * ai) / den
    br = bre_ref[...]
    bi = bim_ref[...]
    lre_ref[...] = lr
    lim_ref[...] = li
    bbre_ref[...] = cr * br - ci * bi
    bbim_ref[...] = cr * bi + ci * br
    ncim_ref[...] = -cim_ref[...]


def _s5_discretize(a_re, a_im, log_dt, b_re, b_im, c_im):
    rows = 2 * SSM_GROUPS
    cols = SSM_STATE * SSM_GROUP
    shp = (2, SSM_GROUPS, SSM_STATE, SSM_GROUP)
    args = [
        jnp.broadcast_to(a_re[..., None], shp).reshape(rows, cols),
        jnp.broadcast_to(a_im[..., None], shp).reshape(rows, cols),
        jnp.broadcast_to(log_dt[..., None, None], shp).reshape(rows, cols),
        b_re.reshape(rows, cols), b_im.reshape(rows, cols), c_im.reshape(rows, cols),
    ]
    spec = pl.BlockSpec((rows, cols), lambda: (0, 0))
    outs = pl.pallas_call(
        _s5_disc_kernel,
        in_specs=[spec] * 6,
        out_specs=[spec] * 5,
        out_shape=[jax.ShapeDtypeStruct((rows, cols), F32)] * 5,
        name="s5_discretize",
    )(*args)
    lre, lim, bbre, bbim, ncim = outs
    lre = lre.reshape(shp)[..., 0]
    lim = lim.reshape(shp)[..., 0]
    return lre, lim, bbre.reshape(shp), bbim.reshape(shp), ncim.reshape(2, SSM_GROUPS, SSM_GROUP, SSM_STATE)


def _state_cols(x):
    lead = x.shape[:-3]
    x = x.reshape(lead + (N_STILE, 2, SSM_STATE, 2))
    x = jnp.moveaxis(x, -1, -3)
    return x.reshape(lead + (STATE_W,))


def _state_uncols(x):
    lead = x.shape[:-1]
    x = x.reshape(lead + (N_STILE, 2, 2, SSM_STATE))
    x = jnp.moveaxis(x, -3, -1)
    return x.reshape(lead + (SSM_GROUPS, SSM_STATE, 2))


def _s5_compact(lre, lim, bbre, bbim, c_re, ncim):
    bb = jnp.stack([bbre, bbim], axis=-1)
    bb = bb.reshape(2, N_STILE, 2, SSM_STATE, SSM_GROUP, 2)
    wb = jnp.transpose(bb, (0, 4, 1, 5, 2, 3)).reshape(2, SSM_GROUP, STATE_W)
    cc = jnp.stack([c_re, ncim], axis=-1)
    cc = cc.reshape(2, N_STILE, 2, SSM_GROUP, SSM_STATE, 2)
    wc = jnp.transpose(cc, (0, 3, 1, 5, 2, 4)).reshape(2, SSM_GROUP, STATE_W)
    lam = jnp.concatenate([lre.reshape(2, N_STILE, 128), lim.reshape(2, N_STILE, 128)], axis=1)
    return wb, wc, lam


def _group_mask(cols):
    row_g = lax.broadcasted_iota(jnp.int32, (256, cols), 0) // SSM_GROUP
    col = lax.broadcasted_iota(jnp.int32, (256, cols), 1)
    col_g = ((col // 256) % 8) * 2 + (col % 128) // SSM_STATE
    return row_g == col_g


def _s5_scan_kernel(u_ref, wb_ref, wc_ref, lam_ref, h0_ref, y_ref, hf_ref,
                    bm_scr, cm_scr, hbuf, state, ends, *, segments):
    npass = 1 if segments == 1 else 2
    d = pl.program_id(0)
    ps = pl.program_id(1)
    ck = pl.program_id(2)
    nck = pl.num_programs(2)
    rows = SSM_TC * SSM_ROWS
    half = STATE_W // 2
    lam = lam_ref[0]

    @pl.when(jnp.logical_and(ps == 0, ck == 0))
    def _():
        wb = jnp.tile(wb_ref[0], (256 // SSM_GROUP, 1))
        bm_scr[...] = jnp.where(_group_mask(STATE_W), wb, 0.0).astype(BF16)
        for n_ in range(2):
            wc = jnp.tile(wc_ref[0, :, n_ * half:(n_ + 1) * half], (256 // SSM_GROUP, 1))
            cm_scr[n_] = jnp.where(_group_mask(half), wc, 0.0).astype(BF16)

    @pl.when(ck == 0)
    def _():
        if npass == 1:
            state[...] = h0_ref[0]
        else:
            @pl.when(ps == 0)
            def _():
                state[...] = jnp.zeros_like(state)

            @pl.when(ps == 1)
            def _():
                ends[...] = state[...]
                state[...] = h0_ref[0]
                lr = lam[0:N_STILE]
                li = lam[N_STILE:]
                for _ in range(int(math.log2(SSM_SEG_LEN))):
                    lr, li = lr * lr - li * li, 2.0 * lr * li

                def carry(order, prev):
                    for sg in order:
                        for sq in range(SSM_ROWS // segments):
                            r = sq * segments + sg
                            q = r + prev
                            for j in range(N_STILE):
                                re = slice(j * 256, j * 256 + 128)
                                im = slice(j * 256 + 128, (j + 1) * 256)
                                pr = state[q:q + 1, re]
                                pi = state[q:q + 1, im]
                                ar = lr[j:j + 1]
                                ai = li[j:j + 1]
                                state[r:r + 1, re] = ends[q:q + 1, re] + ar * pr - ai * pi
                                state[r:r + 1, im] = ends[q:q + 1, im] + ar * pi + ai * pr

                @pl.when(d == 0)
                def _():
                    carry(range(1, segments), -1)

                @pl.when(d == 1)
                def _():
                    carry(range(segments - 2, -1, -1), 1)

    u = u_ref[...].reshape(rows, SSM_W).astype(BF16)
    for j in range(N_STILE):
        k0 = (j // 8) * 256
        hbuf[:, j * 256:(j + 1) * 256] = _dot(u[:, k0:k0 + 256], bm_scr[:, j * 256:(j + 1) * 256])

    JT = 4
    for jb in range(N_STILE // JT):
        tiles = list(range(jb * JT, (jb + 1) * JT))
        lrs = [jnp.broadcast_to(lam[j:j + 1], (SSM_ROWS, 128)) for j in tiles]
        lis = [jnp.broadcast_to(lam[N_STILE + j:N_STILE + j + 1], (SSM_ROWS, 128)) for j in tiles]
        init = tuple(state[:, j * 256:j * 256 + 128] for j in tiles) + \
            tuple(state[:, j * 256 + 128:(j + 1) * 256] for j in tiles)

        def body(t, carry_, tiles=tiles, lrs=lrs, lis=lis):
            tt = t + d * (SSM_TC - 1 - 2 * t)
            r = pl.ds(pl.multiple_of(tt * SSM_ROWS, SSM_ROWS), SSM_ROWS)
            out_r, out_i = [], []
            for n_, j in enumerate(tiles):
                hr, hi = carry_[n_], carry_[JT + n_]
                re = slice(j * 256, j * 256 + 128)
                im = slice(j * 256 + 128, (j + 1) * 256)
                nr = lrs[n_] * hr - lis[n_] * hi + hbuf[r, re]
                ni = lrs[n_] * hi + lis[n_] * hr + hbuf[r, im]
                hbuf[r, re] = nr
                hbuf[r, im] = ni
                out_r.append(nr)
                out_i.append(ni)
            return tuple(out_r) + tuple(out_i)

        fin = lax.fori_loop(0, SSM_TC, body, init, unroll=2)
        for n_, j in enumerate(tiles):
            state[:, j * 256:j * 256 + 128] = fin[n_]
            state[:, j * 256 + 128:(j + 1) * 256] = fin[JT + n_]

    @pl.when(ps == npass - 1)
    def _():
        hb = hbuf[...].astype(BF16)
        y = jnp.concatenate([_dot_nt(hb[:, n_ * half:(n_ + 1) * half], cm_scr[n_]) for n_ in range(2)], axis=1)
        y_ref[...] = y.reshape(SSM_TC, SSM_ROWS, SSM_W)

    @pl.when(jnp.logical_and(ps == npass - 1, ck == nck - 1))
    def _():
        hf_ref[0] = state[...]


def _s5_scan(u, wb, wc, lam, h0, *, segments):
    steps = u.shape[0]
    nck = steps // SSM_TC
    npass = 1 if segments == 1 else 2

    def chunk(d, c):
        return c + d * (nck - 1 - 2 * c)

    def y_chunk(d, p, c):
        return jnp.where(p == npass - 1, chunk(d, c), chunk(d, 0))

    par = pl.BlockSpec((1, SSM_GROUP, STATE_W), lambda d, p, c: (d, 0, 0))
    st = pl.BlockSpec((1, SSM_ROWS, STATE_W), lambda d, p, c: (d, 0, 0))
    return pl.pallas_call(
        functools.partial(_s5_scan_kernel, segments=segments),
        grid=(2, npass, nck),
        in_specs=[
            pl.BlockSpec((SSM_TC, SSM_ROWS, SSM_W), lambda d, p, c: (chunk(d, c), 0, 0)),
            par, par,
            pl.BlockSpec((1, 2 * N_STILE, 128), lambda d, p, c: (d, 0, 0)),
            st,
        ],
        out_specs=[
            pl.BlockSpec((None, SSM_TC, SSM_ROWS, SSM_W), lambda d, p, c: (d, y_chunk(d, p, c), 0, 0)),
            st,
        ],
        out_shape=[
            jax.ShapeDtypeStruct((2, steps, SSM_ROWS, SSM_W), F32),
            jax.ShapeDtypeStruct((2, SSM_ROWS, STATE_W), F32),
        ],
        scratch_shapes=[
            pltpu.VMEM((256, STATE_W), BF16),
            pltpu.VMEM((2, 256, STATE_W // 2), BF16),
            pltpu.VMEM((SSM_TC * SSM_ROWS, STATE_W), F32),
            pltpu.VMEM((SSM_ROWS, STATE_W), F32),
            pltpu.VMEM((SSM_ROWS, STATE_W), F32),
        ],
        compiler_params=_cparams(("arbitrary", "arbitrary", "arbitrary")),
        name="s5_scan_seg%d" % segments,
    )(u, wb, wc, lam, h0)


def _s5_glu_kernel(ypf_ref, ypb_ref, ysf_ref, ysb_ref, u_ref, d_ref, w_ref, o_ref):
    i = pl.program_id(0)
    w = w_ref[...].astype(BF16)
    dd = d_ref[...]

    def glu(y, rows):
        y = y + dd * u_ref[rows, :]
        z = _dot(jax.nn.gelu(y).astype(BF16), w)
        o_ref[rows, :] = (z[:, :SSM_W] * _sigmoid(z[:, SSM_W:])).astype(BF16)

    @pl.when(i < N_PT)
    def _():
        for sq in range(SEQ_PER_TILE):
            cs = slice(sq * SSM_W, (sq + 1) * SSM_W)
            glu(ypf_ref[:, cs] + ypb_ref[:, cs], slice(sq * SEQ, (sq + 1) * SEQ))

    @pl.when(i >= N_PT)
    def _():
        for sg in range(SSM_SEGS):
            cs = slice(sg * SSM_W, (sg + 1) * SSM_W)
            glu(ysf_ref[:, cs] + ysb_ref[:, cs], slice(sg * SSM_SEG_LEN, (sg + 1) * SSM_SEG_LEN))


def _s5_glu(yp, ys, p, d, w, l):
    wp = SEQ_PER_TILE * SSM_W
    ws = SSM_SEGS * SSM_W
    yp = yp.reshape(2, SEQ, BATCH * SSM_W)
    ys = ys.reshape(2, SSM_SEG_LEN, SSM_ROWS * SSM_W)
    return pl.pallas_call(
        _s5_glu_kernel,
        grid=(N_MT,),
        in_specs=[
            pl.BlockSpec((None, SEQ, wp), lambda i: (0, 0, _prompt_idx(i))),
            pl.BlockSpec((None, SEQ, wp), lambda i: (1, 0, _prompt_idx(i))),
            pl.BlockSpec((None, SSM_SEG_LEN, ws), lambda i: (0, 0, _sample_idx(i))),
            pl.BlockSpec((None, SSM_SEG_LEN, ws), lambda i: (1, 0, _sample_idx(i))),
            pl.BlockSpec((TM, SSM_W), lambda i: (i, COL_U)),
            pl.BlockSpec((None, 1, SSM_W), lambda i: (l, 0, 0)),
            pl.BlockSpec((None, SSM_W, 2 * SSM_W), lambda i: (l, 0, 0)),
        ],
        out_specs=pl.BlockSpec((TM, SSM_W), lambda i: (i, 0)),
        out_shape=jax.ShapeDtypeStruct((N_TOK, SSM_W), BF16),
        compiler_params=_cparams(("arbitrary",)),
        name="s5_glu",
    )(yp, yp, ys, ys, p, d.reshape(DEPTH, 1, SSM_W), w)


def _s5_layer(p, prm, st_l, ssm_d, w_ssm_glu, l):
    lre, lim, bbre, bbim, ncim = _s5_discretize(prm["a_re"], prm["a_im"], prm["log_dt"],
                                                prm["b_re"], prm["b_im"], prm["c_im"])
    wb, wc, lam = _s5_compact(lre, lim, bbre, bbim, prm["c_re"], ncim)
    u = p[:, COL_U * SSM_W:(COL_U + 1) * SSM_W]
    up = jnp.transpose(u[:N_PROMPT].reshape(BATCH, SEQ, SSM_W), (1, 0, 2))
    yp, hfin = _s5_scan(up, wb, wc, lam, jnp.zeros((2, SSM_ROWS, STATE_W), F32), segments=1)
    us = u[N_PROMPT:].reshape(DEC_BATCH, SSM_SEGS, SSM_SEG_LEN, SSM_W)
    us = jnp.transpose(us, (2, 0, 1, 3)).reshape(SSM_SEG_LEN, SSM_ROWS, SSM_W)
    h0 = _state_cols(jnp.moveaxis(st_l, 0, 1))
    h0 = jnp.stack([
        jnp.zeros((DEC_BATCH, SSM_SEGS, STATE_W), F32).at[:, 0].set(h0[0]),
        jnp.zeros((DEC_BATCH, SSM_SEGS, STATE_W), F32).at[:, SSM_SEGS - 1].set(h0[1]),
    ]).reshape(2, SSM_ROWS, STATE_W)
    ys, _ = _s5_scan(us, wb, wc, lam, h0, segments=SSM_SEGS)
    out = _s5_glu(yp, ys, p, ssm_d, w_ssm_glu, l)
    new_state = jnp.moveaxis(_state_uncols(hfin), 0, 1)
    return out, new_state


def _rope_tables():
    rows = DEC_SEQ // GRID_W
    r = jnp.repeat(jnp.arange(rows), GRID_W).astype(F32)
    col = jnp.tile(jnp.arange(GRID_W), rows).astype(F32)
    inv = ROPE_BASE ** (-jnp.arange(ROPE_FREQS, dtype=F32) / ROPE_FREQS)
    ang = jnp.stack([r, col], axis=-1)[:, :, None] * inv
    cos = jnp.cos(ang)
    sin = jnp.sin(ang)
    cos_t = jnp.concatenate([cos, cos], axis=-1).reshape(DEC_SEQ, HEAD_DIM)
    sin_t = jnp.concatenate([-sin, sin], axis=-1).reshape(DEC_SEQ, HEAD_DIM)
    return cos_t, sin_t


def kernel(x_prompt, x_sample, cache_k, cache_v, state_ssm, state_ret, c, c_ctx, w_mod, b_mod, norm_g,
           w_ffn_in, w_ffn_out, w_in, w_out, q_norm_g, k_norm_g, ssm_a_re, ssm_a_im, ssm_log_dt,
           ssm_b_re, ssm_b_im, ssm_c_re, ssm_c_im, ssm_d, w_ssm_glu, ret_decay_logit, final_norm_g):
    x = jnp.concatenate([x_prompt.reshape(N_PROMPT, D_MODEL), x_sample.reshape(N_SAMPLE, D_MODEL)], axis=0)
    cond8 = jnp.concatenate([c_ctx[None], c, jnp.zeros((5, D_MODEL), F32)], axis=0)
    mods = _adaln(cond8, w_mod, b_mod).reshape(DEPTH, 8, N_SUB, 3, D_MODEL)
    tile_mod = jnp.array(TILE_MOD, jnp.int32)
    cos_t, sin_t = _rope_tables()
    ks_, vs_, hs_, ss_ = [], [], [], []
    for l in range(DEPTH):
        def mod(sub, kind, l=l):
            return mods[l, :, sub, kind][tile_mod][:, None, :]

        x = _ffn(x, norm_g[l, 0], mod(0, 0), mod(0, 1), mod(0, 2), w_ffn_in, w_ffn_out, l, 0)
        p = _inproj(x, norm_g[l, 1], mod(1, 0), mod(1, 1), w_in, l)
        attn, k_l, v_l = _attention(p, cache_k, cache_v, q_norm_g[l], k_norm_g[l], cos_t, sin_t, l)
        prm = {"a_re": ssm_a_re[l], "a_im": ssm_a_im[l], "log_dt": ssm_log_dt[l], "b_re": ssm_b_re[l],
               "b_im": ssm_b_im[l], "c_re": ssm_c_re[l], "c_im": ssm_c_im[l]}
        ssm, h_l = _s5_layer(p, prm, state_ssm[:, l], ssm_d, w_ssm_glu, l)
        dl = jnp.broadcast_to(ret_decay_logit[l][:, :, None, None], (2, RET_HEADS, 8, HEAD_DIM))
        ret, s_l = _retention(p, dl, cos_t, sin_t, state_ret, l)
        x = _outproj(x, attn, ssm, ret, mod(1, 2), w_out, l)
        x = _ffn(x, norm_g[l, 2], mod(2, 0), mod(2, 1), mod(2, 2), w_ffn_in, w_ffn_out, l, 1,
                 final_g=final_norm_g if l == DEPTH - 1 else None)
        ks_.append(k_l.reshape(BATCH, SEQ, N_KV_HEADS, HEAD_DIM))
        vs_.append(v_l.reshape(BATCH, SEQ, N_KV_HEADS, HEAD_DIM))
        hs_.append(h_l)
        ss_.append(s_l)
    y_prompt = x[:N_PROMPT].reshape(BATCH, SEQ, D_MODEL)
    y_sample = x[N_PROMPT:].reshape(DEC_BATCH, DEC_SEQ, D_MODEL)
    return (y_prompt, y_sample, jnp.stack(ks_, axis=1), jnp.stack(vs_, axis=1),
            jnp.stack(hs_, axis=1), jnp.stack(ss_, axis=1))
```

```python
import functools
import math

import jax
import jax.numpy as jnp
from jax import lax
from jax.experimental import pallas as pl
from jax.experimental.pallas import tpu as pltpu

D_MODEL = 2048
BATCH = 16
SEQ = 256
DEPTH = 2
DEC_BATCH = 2
DEC_SEQ = 1024
PAST_LEN = 512
GRID_W = 64
HEAD_DIM = 128
N_Q_HEADS = 8
N_KV_HEADS = 2
Q_PER_KV = 4
ATTN_W = 1024
KV_W = 256
SSM_W = 512
SSM_GROUP = 16
SSM_GROUPS = 32
SSM_STATE = 64
RET_HEADS = 4
RET_W = 512
IN_W = 4096
D_FF = 5632
N_SUB = 3
RET_CHUNK = 128
ROPE_BASE = 10000.0
ROPE_FREQS = 32
EPS = 1e-6

N_PROMPT = BATCH * SEQ
N_SAMPLE = DEC_BATCH * DEC_SEQ
N_TOK = N_PROMPT + N_SAMPLE
TM = 1024
N_MT = N_TOK // TM
N_PT = N_PROMPT // TM
SEQ_PER_TILE = TM // SEQ
TILE_MOD = (0, 0, 0, 0, 1, 2)
ROW_CHUNK = 128
TF = 256
FFN_NSPLIT = 4
TN = 512
STATE_W = 2 * SSM_GROUPS * SSM_STATE
N_STILE = STATE_W // 256
SSM_ROWS = 16
SSM_SEGS = 8
SSM_SEG_LEN = DEC_SEQ // SSM_SEGS
SSM_TC = 32
ATT_TQ = 256
ATT_S = PAST_LEN + DEC_SEQ
VMEM_LIMIT = 56 * 1024 * 1024
VMEM_LIMIT_FFN = 60 * 1024 * 1024

COL_K = ATTN_W // KV_W
COL_V = COL_K + 1
COL_U = (ATTN_W + 2 * KV_W) // SSM_W
COL_R = (ATTN_W + 2 * KV_W + SSM_W) // RET_W

BF16 = jnp.bfloat16
F32 = jnp.float32


def _cparams(sem, limit=VMEM_LIMIT):
    return pltpu.CompilerParams(dimension_semantics=sem, vmem_limit_bytes=limit)


def _dot(a, b):
    return jnp.dot(a, b, preferred_element_type=F32)


def _dot_nt(a, b):
    return lax.dot_general(a, b, (((1,), (1,)), ((), ())), preferred_element_type=F32)


def _sigmoid(x):
    return 1.0 / (1.0 + jnp.exp(-x))


def _silu(x):
    return x * _sigmoid(x)


def _sample_idx(i):
    return jnp.clip(i - N_PT, 0, DEC_BATCH - 1)


def _prompt_idx(i):
    return jnp.minimum(i, N_PT - 1)


def _row_chunks(n_rows, body):
    def step(c, carry):
        body(pl.ds(pl.multiple_of(c * ROW_CHUNK, ROW_CHUNK), ROW_CHUNK))
        return carry

    lax.fori_loop(0, n_rows // ROW_CHUNK, step, 0)


def _mod_spec(l, k, width=D_MODEL, col=lambda j: 0):
    return pl.BlockSpec((None, None, 1, 1, width), lambda i, j: (l, k, i, 0, col(j)))


def _gain_spec(l, sub):
    return pl.BlockSpec((None, 1, D_MODEL), lambda i, j: (l * N_SUB + sub, 0, 0))


def _adaln_kernel(c_ref, w_ref, b_ref, o_ref):
    a = _silu(c_ref[...]).astype(BF16)
    o_ref[0] = _dot(a, w_ref[0].astype(BF16)) + b_ref[0]


def _adaln(cond8, w_mod, b_mod):
    n = w_mod.shape[-1]
    tn = 1024
    return pl.pallas_call(
        _adaln_kernel,
        grid=(DEPTH, n // tn),
        in_specs=[
            pl.BlockSpec((8, D_MODEL), lambda l, j: (0, 0)),
            pl.BlockSpec((1, D_MODEL, tn), lambda l, j: (l, 0, j)),
            pl.BlockSpec((1, 1, tn), lambda l, j: (l, 0, j)),
        ],
        out_specs=pl.BlockSpec((1, 8, tn), lambda l, j: (l, 0, j)),
        out_shape=jax.ShapeDtypeStruct((DEPTH, 8, n), F32),
        compiler_params=_cparams(("arbitrary", "arbitrary")),
        name="adaln",
    )(cond8, w_mod, b_mod.reshape(DEPTH, 1, n))


def _norm_mod(x, g, shift, scale):
    ms = jnp.mean(x * x, axis=-1, keepdims=True)
    y = x * lax.rsqrt(ms + EPS) * g
    return y * (1.0 + scale) + shift


def _ffn_kernel(x_ref, g_ref, sh_ref, sc_ref, gt_ref, wa_ref, wu_ref, wo_ref, *rest, final):
    if final:
        fg_ref, o_ref, h_scr = rest
    else:
        o_ref, h_scr = rest
    j = pl.program_id(1)

    @pl.when(j == 0)
    def _():
        def pre(rows):
            h = _norm_mod(x_ref[rows, :], g_ref[...], sh_ref[0], sc_ref[0])
            h_scr[rows, :] = h.astype(BF16)
            o_ref[rows, :] = jnp.zeros((ROW_CHUNK, D_MODEL), F32)

        _row_chunks(TM, pre)

    h = h_scr[...]
    a = _dot(h, wa_ref[...].astype(BF16))
    u = _dot(h, wu_ref[...].astype(BF16))
    mid = (_silu(a) * u).astype(BF16)
    wn = D_MODEL // FFN_NSPLIT
    for n in range(FFN_NSPLIT):
        cs = slice(n * wn, (n + 1) * wn)
        o_ref[:, cs] += _dot(mid, wo_ref[:, cs].astype(BF16))

    @pl.when(j == pl.num_programs(1) - 1)
    def _():
        def post(rows):
            out = x_ref[rows, :] + (0.5 * gt_ref[0]) * o_ref[rows, :]
            if final:
                ms = jnp.mean(out * out, axis=-1, keepdims=True)
                out = out * lax.rsqrt(ms + EPS) * fg_ref[...]
            o_ref[rows, :] = out

        _row_chunks(TM, post)


def _ffn(x, gains, modt, w_in, w_out, l, f, final_g=None):
    final = final_g is not None
    sub = 2 * f
    nf = D_FF // TF
    tok = pl.BlockSpec((TM, D_MODEL), lambda i, j: (i, 0))
    in_specs = [
        tok, _gain_spec(l, sub), _mod_spec(l, 3 * sub), _mod_spec(l, 3 * sub + 1), _mod_spec(l, 3 * sub + 2),
        pl.BlockSpec((None, None, D_MODEL, TF), lambda i, j: (l, f, 0, j)),
        pl.BlockSpec((None, None, D_MODEL, TF), lambda i, j: (l, f, 0, j + nf)),
        pl.BlockSpec((None, None, TF, D_MODEL), lambda i, j: (l, f, j, 0)),
    ]
    args = [x, gains, modt, modt, modt, w_in, w_in, w_out]
    if final:
        in_specs.append(pl.BlockSpec((1, D_MODEL), lambda i, j: (0, 0)))
        args.append(final_g.reshape(1, D_MODEL))
    return pl.pallas_call(
        functools.partial(_ffn_kernel, final=final),
        grid=(N_MT, nf),
        in_specs=in_specs,
        out_specs=tok,
        out_shape=jax.ShapeDtypeStruct((N_TOK, D_MODEL), F32),
        scratch_shapes=[pltpu.VMEM((TM, D_MODEL), BF16)],
        compiler_params=_cparams(("arbitrary", "arbitrary"), VMEM_LIMIT_FFN),
        name="ffn_final" if final else "ffn",
    )(*args)


def _resident_weight_spec(rows, l, ncol):
    return pl.BlockSpec((None, rows, TN), lambda i, j: (l, 0, jnp.where(i == 0, j, ncol - 1)))


def _inproj_kernel(x_ref, g_ref, sh_ref, sc_ref, w_ref, o_ref, h_scr, w_scr):
    i = pl.program_id(0)
    j = pl.program_id(1)

    @pl.when(j == 0)
    def _():
        def pre(rows):
            h = _norm_mod(x_ref[rows, :], g_ref[...], sh_ref[0], sc_ref[0])
            h_scr[rows, :] = h.astype(BF16)

        _row_chunks(TM, pre)

    @pl.when(i == 0)
    def _():
        w_scr[j] = w_ref[...].astype(BF16)

    o_ref[...] = _dot(h_scr[...], w_scr[j])


def _inproj(x, gains, modt, w, l):
    ncol = IN_W // TN
    return pl.pallas_call(
        _inproj_kernel,
        grid=(N_MT, ncol),
        in_specs=[
            pl.BlockSpec((TM, D_MODEL), lambda i, j: (i, 0)),
            _gain_spec(l, 1), _mod_spec(l, 3), _mod_spec(l, 4),
            _resident_weight_spec(D_MODEL, l, ncol),
        ],
        out_specs=pl.BlockSpec((TM, TN), lambda i, j: (i, j)),
        out_shape=jax.ShapeDtypeStruct((N_TOK, IN_W), F32),
        scratch_shapes=[pltpu.VMEM((TM, D_MODEL), BF16), pltpu.VMEM((ncol, D_MODEL, TN), BF16)],
        compiler_params=_cparams(("arbitrary", "arbitrary")),
        name="inproj",
    )(x, gains, modt, modt, w)


def _outproj_kernel(x_ref, a_ref, s_ref, r_ref, gt_ref, w_ref, o_ref, w_scr):
    i = pl.program_id(0)
    j = pl.program_id(1)

    @pl.when(i == 0)
    def _():
        w_scr[j] = w_ref[...].astype(BF16)

    y = _dot(a_ref[...], w_scr[j, 0:ATTN_W, :])
    y += _dot(s_ref[...], w_scr[j, ATTN_W:ATTN_W + SSM_W, :])
    y += _dot(r_ref[...], w_scr[j, ATTN_W + SSM_W:, :])
    o_ref[...] = x_ref[...] + gt_ref[0] * y


def _outproj(x, attn, ssm, ret, modt, w, l):
    ncol = D_MODEL // TN
    return pl.pallas_call(
        _outproj_kernel,
        grid=(N_MT, ncol),
        in_specs=[
            pl.BlockSpec((TM, TN), lambda i, j: (i, j)),
            pl.BlockSpec((TM, ATTN_W), lambda i, j: (i, 0)),
            pl.BlockSpec((TM, SSM_W), lambda i, j: (i, 0)),
            pl.BlockSpec((TM, RET_W), lambda i, j: (i, 0)),
            _mod_spec(l, 5, TN, lambda j: j),
            _resident_weight_spec(D_MODEL, l, ncol),
        ],
        out_specs=pl.BlockSpec((TM, TN), lambda i, j: (i, j)),
        out_shape=jax.ShapeDtypeStruct((N_TOK, D_MODEL), F32),
        scratch_shapes=[pltpu.VMEM((ncol, D_MODEL, TN), BF16)],
        compiler_params=_cparams(("arbitrary", "arbitrary")),
        name="outproj",
    )(x, attn, ssm, ret, modt, w)


def _head_rms(x, g):
    ms = jnp.mean(x * x, axis=-1, keepdims=True)
    return x * lax.rsqrt(ms + EPS) * g


def _rope(x, cos, sin_signed):
    lane = lax.broadcasted_iota(jnp.int32, x.shape, 1)
    partner = jnp.where((lane % 64) < 32, pltpu.roll(x, 96, 1), pltpu.roll(x, 32, 1))
    return x * cos + partner * sin_signed


def _attend(qs, k, v):
    s = _dot_nt(qs.astype(BF16), k) * (HEAD_DIM ** -0.5)
    p = jnp.exp(s - jnp.max(s, axis=-1, keepdims=True))
    l = jnp.sum(p, axis=-1, keepdims=True)
    return _dot(p.astype(BF16), v) / l


def _attn_kernel(q_ref, k_ref, v_ref, ck_ref, cv_ref, qg_ref, kg_ref, cos_ref, sin_ref,
                 o_ref, kc_ref, vc_ref, k_scr, v_scr):
    i = pl.program_id(0)
    qg = qg_ref[...]
    kg = kg_ref[...]

    def heads(kv):
        return [slice((kv * Q_PER_KV + g) * HEAD_DIM, (kv * Q_PER_KV + g + 1) * HEAD_DIM) for g in range(Q_PER_KV)]

    @pl.when(i < N_PT)
    def _():
        def seq_body(sq, carry):
            rows = pl.ds(pl.multiple_of(sq * SEQ, SEQ), SEQ)
            v = v_ref[rows, :]
            vc_ref[sq] = v
            for kv in range(N_KV_HEADS):
                ksl = slice(kv * HEAD_DIM, (kv + 1) * HEAD_DIM)
                kn = _head_rms(k_ref[rows, ksl], kg)
                kc_ref[sq, :, ksl] = kn
                qs = jnp.concatenate([_head_rms(q_ref[rows, hs], qg) for hs in heads(kv)], axis=0)
                o = _attend(qs, kn.astype(BF16), v[:, ksl].astype(BF16))
                for g, hs in enumerate(heads(kv)):
                    o_ref[rows, hs] = o[g * SEQ:(g + 1) * SEQ].astype(BF16)
            return carry

        lax.fori_loop(0, SEQ_PER_TILE, seq_body, 0)

    @pl.when(i >= N_PT)
    def _():
        k_scr[0:PAST_LEN, :] = ck_ref[...].astype(BF16)
        v_scr[0:PAST_LEN, :] = cv_ref[...].astype(BF16)
        v_scr[PAST_LEN:, :] = v_ref[...].astype(BF16)
        for kv in range(N_KV_HEADS):
            ksl = slice(kv * HEAD_DIM, (kv + 1) * HEAD_DIM)
            kn = _rope(_head_rms(k_ref[:, ksl], kg), cos_ref[...], sin_ref[...])
            k_scr[PAST_LEN:, ksl] = kn.astype(BF16)

        def q_body(qb, carry):
            rows = pl.ds(pl.multiple_of(qb * ATT_TQ, ATT_TQ), ATT_TQ)
            cos = cos_ref[rows, :]
            sin = sin_ref[rows, :]
            for kv in range(N_KV_HEADS):
                ksl = slice(kv * HEAD_DIM, (kv + 1) * HEAD_DIM)
                qs = jnp.concatenate([_rope(_head_rms(q_ref[rows, hs], qg), cos, sin) for hs in heads(kv)], axis=0)
                o = _attend(qs, k_scr[:, ksl], v_scr[:, ksl])
                for g, hs in enumerate(heads(kv)):
                    o_ref[rows, hs] = o[g * ATT_TQ:(g + 1) * ATT_TQ].astype(BF16)
            return carry

        lax.fori_loop(0, DEC_SEQ // ATT_TQ, q_body, 0)


def _attention(p, cache_k, cache_v, qg, kg, cos_t, sin_t, l):
    cache_spec = pl.BlockSpec((None, None, PAST_LEN, KV_W), lambda i: (_sample_idx(i), l, 0, 0))
    vec = pl.BlockSpec((None, 1, HEAD_DIM), lambda i: (l, 0, 0))
    tab = pl.BlockSpec((DEC_SEQ, HEAD_DIM), lambda i: (0, 0))
    new_cache = pl.BlockSpec((SEQ_PER_TILE, SEQ, KV_W), lambda i: (_prompt_idx(i), 0, 0))
    return pl.pallas_call(
        _attn_kernel,
        grid=(N_MT,),
        in_specs=[
            pl.BlockSpec((TM, ATTN_W), lambda i: (i, 0)),
            pl.BlockSpec((TM, KV_W), lambda i: (i, COL_K)),
            pl.BlockSpec((TM, KV_W), lambda i: (i, COL_V)),
            cache_spec, cache_spec, vec, vec, tab, tab,
        ],
        out_specs=[pl.BlockSpec((TM, ATTN_W), lambda i: (i, 0)), new_cache, new_cache],
        out_shape=[
            jax.ShapeDtypeStruct((N_TOK, ATTN_W), BF16),
            jax.ShapeDtypeStruct((BATCH, SEQ, KV_W), F32),
            jax.ShapeDtypeStruct((BATCH, SEQ, KV_W), F32),
        ],
        scratch_shapes=[pltpu.VMEM((ATT_S, KV_W), BF16), pltpu.VMEM((ATT_S, KV_W), BF16)],
        compiler_params=_cparams(("arbitrary",)),
        name="attention",
    )(p, p, p, cache_k.reshape(DEC_BATCH, DEPTH, PAST_LEN, KV_W), cache_v.reshape(DEC_BATCH, DEPTH, PAST_LEN, KV_W),
      qg.reshape(DEPTH, 1, HEAD_DIM), kg.reshape(DEPTH, 1, HEAD_DIM), cos_t, sin_t)


def _log_sigmoid(x):
    return -(jnp.maximum(-x, 0.0) + jnp.log(1.0 + jnp.exp(-jnp.abs(x))))


def _retention_kernel(q_ref, k_ref, v_ref, g_ref, dl_ref, cos_ref, sin_ref, s0_ref, o_ref, sf_ref, acc):
    i = pl.program_id(0)
    C = RET_CHUNK
    ii = lax.broadcasted_iota(jnp.int32, (C, C), 0).astype(F32)
    jj = lax.broadcasted_iota(jnp.int32, (C, C), 1).astype(F32)

    decs = {}
    for d in range(2):
        for h in range(RET_HEADS):
            lg = _log_sigmoid(dl_ref[d, h][0:1, :])
            if d == 0:
                diff = ii - jj
                q_dec = jnp.exp((ii + 1.0) * lg)
                k_dec = jnp.exp((C - 1.0 - ii) * lg)
            else:
                diff = jj - ii
                q_dec = jnp.exp((C - ii) * lg)
                k_dec = jnp.exp(ii * lg)
            inner = jnp.where(diff >= 0, jnp.exp(jnp.maximum(diff, 0.0) * lg), 0.0)
            decs[d, h] = (inner, q_dec, k_dec, jnp.exp(C * lg))

    def scan_seq(row0, n, rope, init):
        states = {(d, h): init(d, h) for d in range(2) for h in range(RET_HEADS)}
        for t in range(n):
            for d in range(2):
                c = t if d == 0 else n - 1 - t
                rows = slice(row0 + c * C, row0 + (c + 1) * C)
                for h in range(RET_HEADS):
                    hs = slice(h * HEAD_DIM, (h + 1) * HEAD_DIM)
                    inner, q_dec, k_dec, c_dec = decs[d, h]
                    qc = q_ref[rows, hs]
                    kc = k_ref[rows, hs] * (HEAD_DIM ** -0.5)
                    vc = v_ref[rows, hs].astype(BF16)
                    if rope:
                        qc = _rope(qc, cos_ref[rows, :], sin_ref[rows, :])
                        kc = _rope(kc, cos_ref[rows, :], sin_ref[rows, :])
                    s = states[d, h]
                    att = (_dot_nt(qc.astype(BF16), kc.astype(BF16)) * inner).astype(BF16)
                    o = _dot(att, vc) + _dot(qc.astype(BF16), s.astype(BF16)) * q_dec
                    if 2 * t < n:
                        acc[rows, hs] = o
                    else:
                        acc[rows, hs] += o
                    kd = (kc * k_dec).T.astype(BF16)
                    states[d, h] = s * c_dec + _dot(kd, vc)
        return states

    @pl.when(i < N_PT)
    def _():
        for sq in range(SEQ_PER_TILE):
            fin = scan_seq(sq * SEQ, SEQ // C, False, lambda d, h: jnp.zeros((HEAD_DIM, HEAD_DIM), F32))
            for (d, h), s in fin.items():
                sf_ref[sq, d, h] = s

    @pl.when(i >= N_PT)
    def _():
        scan_seq(0, DEC_SEQ // C, True, lambda d, h: s0_ref[d, h])

    for h in range(RET_HEADS):
        hs = slice(h * HEAD_DIM, (h + 1) * HEAD_DIM)
        o = acc[:, hs]
        o = o - jnp.mean(o, axis=-1, keepdims=True)
        o = o * lax.rsqrt(jnp.mean(o * o, axis=-1, keepdims=True) + EPS)
        o_ref[:, hs] = (o * _silu(g_ref[:, hs])).astype(BF16)


def _retention(p, dl, cos_t, sin_t, state_ret, l):
    tab = pl.BlockSpec((DEC_SEQ, HEAD_DIM), lambda i: (0, 0))
    st = (2, RET_HEADS, HEAD_DIM, HEAD_DIM)
    return pl.pallas_call(
        _retention_kernel,
        grid=(N_MT,),
        in_specs=[pl.BlockSpec((TM, RET_W), lambda i, k=k: (i, COL_R + k)) for k in range(4)] + [
            pl.BlockSpec((None, 2, RET_HEADS, 8, HEAD_DIM), lambda i: (l, 0, 0, 0, 0)),
            tab, tab,
            pl.BlockSpec((None, None) + st, lambda i: (_sample_idx(i), l, 0, 0, 0, 0)),
        ],
        out_specs=[
            pl.BlockSpec((TM, RET_W), lambda i: (i, 0)),
            pl.BlockSpec((SEQ_PER_TILE,) + st, lambda i: (_prompt_idx(i), 0, 0, 0, 0)),
        ],
        out_shape=[
            jax.ShapeDtypeStruct((N_TOK, RET_W), BF16),
            jax.ShapeDtypeStruct((BATCH,) + st, F32),
        ],
        scratch_shapes=[pltpu.VMEM((TM, RET_W), F32)],
        compiler_params=_cparams(("arbitrary",)),
        name="retention",
    )(p, p, p, p, dl, cos_t, sin_t, state_ret)


def _s5_disc_kernel(are_ref, aim_ref, ldt_ref, bre_ref, bim_ref, cim_ref,
                    lre_ref, lim_ref, bbre_ref, bbim_ref, ncim_ref):
    ar = are_ref[...]
    ai = aim_ref[...]
    dt = jnp.exp(ldt_ref[...])
    mag = jnp.exp(ar * dt)
    lr = mag * jnp.cos(ai * dt)
    li = mag * jnp.sin(ai * dt)
    den = ar * ar + ai * ai
    nr = lr - 1.0
    cr = (nr * ar + li * ai) / den
    ci = (li * ar - nr * ai) / den
    br = bre_ref[...]
    bi = bim_ref[...]
    lre_ref[...] = lr
    lim_ref[...] = li
    bbre_ref[...] = cr * br - ci * bi
    bbim_ref[...] = cr * bi + ci * br
    ncim_ref[...] = -cim_ref[...]


def _s5_discretize(a_re, a_im, log_dt, b_re, b_im, c_im):
    rows = DEPTH * 2 * SSM_GROUPS
    cols = SSM_STATE * SSM_GROUP
    shp = (DEPTH, 2, SSM_GROUPS, SSM_STATE, SSM_GROUP)
    args = [
        jnp.broadcast_to(a_re[..., None], shp).reshape(rows, cols),
        jnp.broadcast_to(a_im[..., None], shp).reshape(rows, cols),
        jnp.broadcast_to(log_dt[..., None, None], shp).reshape(rows, cols),
        b_re.reshape(rows, cols), b_im.reshape(rows, cols), c_im.reshape(rows, cols),
    ]
    spec = pl.BlockSpec((rows, cols), lambda: (0, 0))
    outs = pl.pallas_call(
        _s5_disc_kernel,
        in_specs=[spec] * 6,
        out_specs=[spec] * 5,
        out_shape=[jax.ShapeDtypeStruct((rows, cols), F32)] * 5,
        name="s5_discretize",
    )(*args)
    lre, lim, bbre, bbim, ncim = outs
    lre = lre.reshape(shp)[..., 0]
    lim = lim.reshape(shp)[..., 0]
    return (lre, lim, bbre.reshape(shp), bbim.reshape(shp),
            ncim.reshape(DEPTH, 2, SSM_GROUPS, SSM_GROUP, SSM_STATE))


def _state_cols(x):
    lead = x.shape[:-3]
    x = x.reshape(lead + (N_STILE, 2, SSM_STATE, 2))
    x = jnp.moveaxis(x, -1, -3)
    return x.reshape(lead + (STATE_W,))


def _state_uncols(x):
    lead = x.shape[:-1]
    x = x.reshape(lead + (N_STILE, 2, 2, SSM_STATE))
    x = jnp.moveaxis(x, -3, -1)
    return x.reshape(lead + (SSM_GROUPS, SSM_STATE, 2))


def _s5_compact(lre, lim, bbre, bbim, c_re, ncim):
    n = DEPTH * 2
    bb = jnp.stack([bbre, bbim], axis=-1)
    bb = bb.reshape(n, N_STILE, 2, SSM_STATE, SSM_GROUP, 2)
    wb = jnp.transpose(bb, (0, 4, 1, 5, 2, 3)).reshape(DEPTH, 2, SSM_GROUP, STATE_W)
    cc = jnp.stack([c_re, ncim], axis=-1)
    cc = cc.reshape(n, N_STILE, 2, SSM_GROUP, SSM_STATE, 2)
    wc = jnp.transpose(cc, (0, 3, 1, 5, 2, 4)).reshape(DEPTH, 2, SSM_GROUP, STATE_W)
    lam = jnp.concatenate([lre.reshape(DEPTH, 2, N_STILE, 128), lim.reshape(DEPTH, 2, N_STILE, 128)], axis=2)
    return wb, wc, lam


def _group_mask(cols):
    row_g = lax.broadcasted_iota(jnp.int32, (256, cols), 0) // SSM_GROUP
    col = lax.broadcasted_iota(jnp.int32, (256, cols), 1)
    col_g = ((col // 256) % 8) * 2 + (col % 128) // SSM_STATE
    return row_g == col_g


def _s5_scan_kernel(u_ref, wb_ref, wc_ref, lam_ref, h0_ref, y_ref, hf_ref,
                    bm_scr, cm_scr, bu_scr, hs_scr, y_scr, state, ends, *, segments):
    npass = 1 if segments == 1 else 2
    d = pl.program_id(0)
    ps = pl.program_id(1)
    ck = pl.program_id(2)
    nck = pl.num_programs(2)
    rows = SSM_TC * SSM_ROWS
    half = STATE_W // 2
    lam = lam_ref[...]

    @pl.when(jnp.logical_and(ps == 0, ck == 0))
    def _():
        wb = jnp.tile(wb_ref[...], (256 // SSM_GROUP, 1))
        bm_scr[...] = jnp.where(_group_mask(STATE_W), wb, 0.0).astype(BF16)
        for n_ in range(2):
            wc = jnp.tile(wc_ref[:, n_ * half:(n_ + 1) * half], (256 // SSM_GROUP, 1))
            cm_scr[n_] = jnp.where(_group_mask(half), wc, 0.0).astype(BF16)

    @pl.when(ck == 0)
    def _():
        if npass == 1:
            state[...] = h0_ref[0]
        else:
            @pl.when(ps == 0)
            def _():
                state[...] = jnp.zeros_like(state)

            @pl.when(ps == 1)
            def _():
                ends[...] = state[...]
                state[...] = h0_ref[0]
                lr = lam[0:N_STILE]
                li = lam[N_STILE:]
                for _ in range(int(math.log2(SSM_SEG_LEN))):
                    lr, li = lr * lr - li * li, 2.0 * lr * li

                def carry(order, prev):
                    for sg in order:
                        for sq in range(SSM_ROWS // segments):
                            r = sq * segments + sg
                            q = r + prev
                            for j in range(N_STILE):
                                re = slice(j * 256, j * 256 + 128)
                                im = slice(j * 256 + 128, (j + 1) * 256)
                                pr = state[q:q + 1, re]
                                pi = state[q:q + 1, im]
                                ar = lr[j:j + 1]
                                ai = li[j:j + 1]
                                state[r:r + 1, re] = ends[q:q + 1, re] + ar * pr - ai * pi
                                state[r:r + 1, im] = ends[q:q + 1, im] + ar * pi + ai * pr

                @pl.when(d == 0)
                def _():
                    carry(range(1, segments), -1)

                @pl.when(d == 1)
                def _():
                    carry(range(segments - 2, -1, -1), 1)

    u = u_ref[...].reshape(rows, SSM_W).astype(BF16)
    for j in range(N_STILE):
        k0 = (j // 8) * 256
        bu = _dot(u[:, k0:k0 + 256], bm_scr[:, j * 256:(j + 1) * 256])
        bu_scr[2 * j] = bu[:, :128]
        bu_scr[2 * j + 1] = bu[:, 128:]

    JT = 4
    for jb in range(N_STILE // JT):
        tiles = list(range(jb * JT, (jb + 1) * JT))
        lrs = [jnp.broadcast_to(lam[j:j + 1], (SSM_ROWS, 128)) for j in tiles]
        lis = [jnp.broadcast_to(lam[N_STILE + j:N_STILE + j + 1], (SSM_ROWS, 128)) for j in tiles]
        init = tuple(state[:, j * 256:j * 256 + 128] for j in tiles) + \
            tuple(state[:, j * 256 + 128:(j + 1) * 256] for j in tiles)

        def body(t, carry_, tiles=tiles, lrs=lrs, lis=lis):
            tt = t + d * (SSM_TC - 1 - 2 * t)
            src = pl.ds(tt, SSM_ROWS, stride=SSM_TC)
            dst = pl.ds(pl.multiple_of(tt * SSM_ROWS, SSM_ROWS), SSM_ROWS)
            out_r, out_i = [], []
            for n_, j in enumerate(tiles):
                hr, hi = carry_[n_], carry_[JT + n_]
                re = slice(j * 256, j * 256 + 128)
                im = slice(j * 256 + 128, (j + 1) * 256)
                nr = lrs[n_] * hr - lis[n_] * hi + bu_scr[2 * j, src, :]
                ni = lrs[n_] * hi + lis[n_] * hr + bu_scr[2 * j + 1, src, :]
                hs_scr[dst, re] = nr
                hs_scr[dst, im] = ni
                out_r.append(nr)
                out_i.append(ni)
            return tuple(out_r) + tuple(out_i)

        fin = lax.fori_loop(0, SSM_TC, body, init, unroll=2)
        for n_, j in enumerate(tiles):
            state[:, j * 256:j * 256 + 128] = fin[n_]
            state[:, j * 256 + 128:(j + 1) * 256] = fin[JT + n_]

    @pl.when(ps == npass - 1)
    def _():
        hb = hs_scr[...].astype(BF16)
        for n_ in range(2):
            y = _dot_nt(hb[:, n_ * half:(n_ + 1) * half], cm_scr[n_])
            y_scr[2 * n_] = y[:, :128]
            y_scr[2 * n_ + 1] = y[:, 128:]
        for r in range(SSM_ROWS):
            for c_ in range(SSM_W // 128):
                y_ref[r, :, c_ * 128:(c_ + 1) * 128] = y_scr[c_, pl.ds(r, SSM_TC, stride=SSM_ROWS), :]

    @pl.when(jnp.logical_and(ps == npass - 1, ck == nck - 1))
    def _():
        hf_ref[0] = state[...]


def _s5_scan(p3, row_block, wb, wc, lam, h0, l, *, segments):
    steps = p3.shape[1]
    nck = steps // SSM_TC
    npass = 1 if segments == 1 else 2

    def chunk(d, c):
        return c + d * (nck - 1 - 2 * c)

    def y_chunk(d, p, c):
        return jnp.where(p == npass - 1, chunk(d, c), chunk(d, 0))

    par = pl.BlockSpec((None, None, SSM_GROUP, STATE_W), lambda d, p, c: (l, d, 0, 0))
    st = pl.BlockSpec((1, SSM_ROWS, STATE_W), lambda d, p, c: (d, 0, 0))
    return pl.pallas_call(
        functools.partial(_s5_scan_kernel, segments=segments),
        grid=(2, npass, nck),
        in_specs=[
            pl.BlockSpec((SSM_ROWS, SSM_TC, SSM_W), lambda d, p, c: (row_block, chunk(d, c), COL_U)),
            par, par,
            pl.BlockSpec((None, None, 2 * N_STILE, 128), lambda d, p, c: (l, d, 0, 0)),
            st,
        ],
        out_specs=[
            pl.BlockSpec((None, SSM_ROWS, SSM_TC, SSM_W), lambda d, p, c: (d, 0, y_chunk(d, p, c), 0)),
            st,
        ],
        out_shape=[
            jax.ShapeDtypeStruct((2, SSM_ROWS, steps, SSM_W), F32),
            jax.ShapeDtypeStruct((2, SSM_ROWS, STATE_W), F32),
        ],
        scratch_shapes=[
            pltpu.VMEM((256, STATE_W), BF16),
            pltpu.VMEM((2, 256, STATE_W // 2), BF16),
            pltpu.VMEM((STATE_W // 128, SSM_TC * SSM_ROWS, 128), F32),
            pltpu.VMEM((SSM_TC * SSM_ROWS, STATE_W), F32),
            pltpu.VMEM((SSM_W // 128, SSM_TC * SSM_ROWS, 128), F32),
            pltpu.VMEM((SSM_ROWS, STATE_W), F32),
            pltpu.VMEM((SSM_ROWS, STATE_W), F32),
        ],
        compiler_params=_cparams(("arbitrary", "arbitrary", "arbitrary")),
        name="s5_scan_seg%d" % segments,
    )(p3, wb, wc, lam, h0)


def _s5_glu_kernel(ypf_ref, ypb_ref, ysf_ref, ysb_ref, u_ref, d_ref, w_ref, o_ref):
    i = pl.program_id(0)
    w = w_ref[...].astype(BF16)

    def glu(y):
        y = y + d_ref[...] * u_ref[...]
        z = _dot(jax.nn.gelu(y).astype(BF16), w)
        o_ref[...] = (z[:, :SSM_W] * _sigmoid(z[:, SSM_W:])).astype(BF16)

    @pl.when(i < N_PT)
    def _():
        glu(ypf_ref[...] + ypb_ref[...])

    @pl.when(i >= N_PT)
    def _():
        glu(ysf_ref[...] + ysb_ref[...])


def _s5_glu(yp, ys, p, d, w, l):
    return pl.pallas_call(
        _s5_glu_kernel,
        grid=(N_MT,),
        in_specs=[
            pl.BlockSpec((None, TM, SSM_W), lambda i: (0, _prompt_idx(i), 0)),
            pl.BlockSpec((None, TM, SSM_W), lambda i: (1, _prompt_idx(i), 0)),
            pl.BlockSpec((None, TM, SSM_W), lambda i: (0, _sample_idx(i), 0)),
            pl.BlockSpec((None, TM, SSM_W), lambda i: (1, _sample_idx(i), 0)),
            pl.BlockSpec((TM, SSM_W), lambda i: (i, COL_U)),
            pl.BlockSpec((None, 1, SSM_W), lambda i: (l, 0, 0)),
            pl.BlockSpec((None, SSM_W, 2 * SSM_W), lambda i: (l, 0, 0)),
        ],
        out_specs=pl.BlockSpec((TM, SSM_W), lambda i: (i, 0)),
        out_shape=jax.ShapeDtypeStruct((N_TOK, SSM_W), BF16),
        compiler_params=_cparams(("arbitrary",)),
        name="s5_glu",
    )(yp, yp, ys, ys, p, d.reshape(DEPTH, 1, SSM_W), w)


def _s5_layer(p, s5p, st_l, ssm_d, w_ssm_glu, l):
    wb, wc, lam = s5p
    yp, hfin = _s5_scan(p.reshape(N_TOK // SEQ, SEQ, IN_W), 0, wb, wc, lam,
                        jnp.zeros((2, SSM_ROWS, STATE_W), F32), l, segments=1)
    h0 = _state_cols(jnp.moveaxis(st_l, 0, 1))
    h0 = jnp.stack([
        jnp.zeros((DEC_BATCH, SSM_SEGS, STATE_W), F32).at[:, 0].set(h0[0]),
        jnp.zeros((DEC_BATCH, SSM_SEGS, STATE_W), F32).at[:, SSM_SEGS - 1].set(h0[1]),
    ]).reshape(2, SSM_ROWS, STATE_W)
    ys, _ = _s5_scan(p.reshape(N_TOK // SSM_SEG_LEN, SSM_SEG_LEN, IN_W), N_PROMPT // SSM_SEG_LEN // SSM_ROWS,
                     wb, wc, lam, h0, l, segments=SSM_SEGS)
    out = _s5_glu(yp.reshape(2, N_PROMPT, SSM_W), ys.reshape(2, N_SAMPLE, SSM_W), p, ssm_d, w_ssm_glu, l)
    new_state = jnp.moveaxis(_state_uncols(hfin), 0, 1)
    return out, new_state


def _rope_tables():
    rows = DEC_SEQ // GRID_W
    r = jnp.repeat(jnp.arange(rows), GRID_W).astype(F32)
    col = jnp.tile(jnp.arange(GRID_W), rows).astype(F32)
    inv = ROPE_BASE ** (-jnp.arange(ROPE_FREQS, dtype=F32) / ROPE_FREQS)
    ang = jnp.stack([r, col], axis=-1)[:, :, None] * inv
    cos = jnp.cos(ang)
    sin = jnp.sin(ang)
    cos_t = jnp.concatenate([cos, cos], axis=-1).reshape(DEC_SEQ, HEAD_DIM)
    sin_t = jnp.concatenate([-sin, sin], axis=-1).reshape(DEC_SEQ, HEAD_DIM)
    return cos_t, sin_t


def kernel(x_prompt, x_sample, cache_k, cache_v, state_ssm, state_ret, c, c_ctx, w_mod, b_mod, norm_g,
           w_ffn_in, w_ffn_out, w_in, w_out, q_norm_g, k_norm_g, ssm_a_re, ssm_a_im, ssm_log_dt,
           ssm_b_re, ssm_b_im, ssm_c_re, ssm_c_im, ssm_d, w_ssm_glu, ret_decay_logit, final_norm_g):
    x = jnp.concatenate([x_prompt.reshape(N_PROMPT, D_MODEL), x_sample.reshape(N_SAMPLE, D_MODEL)], axis=0)
    cond8 = jnp.concatenate([c_ctx[None], c, jnp.zeros((5, D_MODEL), F32)], axis=0)
    mods = _adaln(cond8, w_mod, b_mod).reshape(DEPTH, 8, N_SUB * 3, D_MODEL)
    modt = jnp.transpose(mods[:, jnp.array(TILE_MOD, jnp.int32)], (0, 2, 1, 3))[:, :, :, None, :]
    gains = norm_g.reshape(DEPTH * N_SUB, 1, D_MODEL)
    cos_t, sin_t = _rope_tables()
    lre, lim, bbre, bbim, ncim = _s5_discretize(ssm_a_re, ssm_a_im, ssm_log_dt, ssm_b_re, ssm_b_im, ssm_c_im)
    s5p = _s5_compact(lre, lim, bbre, bbim, ssm_c_re, ncim)
    dl = jnp.broadcast_to(ret_decay_logit[:, :, :, None, None], (DEPTH, 2, RET_HEADS, 8, HEAD_DIM))
    ks_, vs_, hs_, ss_ = [], [], [], []
    for l in range(DEPTH):
        x = _ffn(x, gains, modt, w_ffn_in, w_ffn_out, l, 0)
        p = _inproj(x, gains, modt, w_in, l)
        attn, k_l, v_l = _attention(p, cache_k, cache_v, q_norm_g, k_norm_g, cos_t, sin_t, l)
        ssm, h_l = _s5_layer(p, s5p, state_ssm[:, l], ssm_d, w_ssm_glu, l)
        ret, s_l = _retention(p, dl, cos_t, sin_t, state_ret, l)
        x = _outproj(x, attn, ssm, ret, modt, w_out, l)
        x = _ffn(x, gains, modt, w_ffn_in, w_ffn_out, l, 1, final_g=final_norm_g if l == DEPTH - 1 else None)
        ks_.append(k_l.reshape(BATCH, SEQ, N_KV_HEADS, HEAD_DIM))
        vs_.append(v_l.reshape(BATCH, SEQ, N_KV_HEADS, HEAD_DIM))
        hs_.append(h_l)
        ss_.append(s_l)
    y_prompt = x[:N_PROMPT].reshape(BATCH, SEQ, D_MODEL)
    y_sample = x[N_PROMPT:].reshape(DEC_BATCH, DEC_SEQ, D_MODEL)
    return (y_prompt, y_sample, jnp.stack(ks_, axis=1), jnp.stack(vs_, axis=1),
            jnp.stack(hs_, axis=1), jnp.stack(ss_, axis=1))
```

```python
import functools
import math

import jax
import jax.numpy as jnp
from jax import lax
from jax.experimental import pallas as pl
from jax.experimental.pallas import tpu as pltpu

D_MODEL = 2048
BATCH = 16
SEQ = 256
DEPTH = 2
DEC_BATCH = 2
DEC_SEQ = 1024
PAST_LEN = 512
GRID_W = 64
HEAD_DIM = 128
N_Q_HEADS = 8
N_KV_HEADS = 2
Q_PER_KV = 4
ATTN_W = 1024
KV_W = 256
SSM_W = 512
SSM_GROUP = 16
SSM_GROUPS = 32
SSM_STATE = 64
RET_HEADS = 4
RET_W = 512
IN_W = 4096
D_FF = 5632
N_SUB = 3
RET_CHUNK = 128
ROPE_BASE = 10000.0
ROPE_FREQS = 32
EPS = 1e-6

N_PROMPT = BATCH * SEQ
N_SAMPLE = DEC_BATCH * DEC_SEQ
N_TOK = N_PROMPT + N_SAMPLE
TM = 1024
N_MT = N_TOK // TM
N_PT = N_PROMPT // TM
SEQ_PER_TILE = TM // SEQ
TILE_MOD = (0, 0, 0, 0, 1, 2)
ROW_CHUNK = 128
TF = 256
FFN_NSPLIT = 4
TN = 512
STATE_W = 2 * SSM_GROUPS * SSM_STATE
N_STILE = STATE_W // 256
SSM_ROWS = 16
SSM_SEGS = 8
SSM_SEG_LEN = DEC_SEQ // SSM_SEGS
SSM_TC = 32
ATT_TQ = 256
ATT_S = PAST_LEN + DEC_SEQ
VMEM_LIMIT = 56 * 1024 * 1024
VMEM_LIMIT_FFN = 60 * 1024 * 1024

COL_K = ATTN_W // KV_W
COL_V = COL_K + 1
COL_U = (ATTN_W + 2 * KV_W) // SSM_W
COL_R = (ATTN_W + 2 * KV_W + SSM_W) // RET_W

BF16 = jnp.bfloat16
F32 = jnp.float32


def _cparams(sem, limit=VMEM_LIMIT):
    return pltpu.CompilerParams(dimension_semantics=sem, vmem_limit_bytes=limit)


def _dot(a, b):
    return jnp.dot(a, b, preferred_element_type=F32)


def _dot_nt(a, b):
    return lax.dot_general(a, b, (((1,), (1,)), ((), ())), preferred_element_type=F32)


def _sigmoid(x):
    return 1.0 / (1.0 + jnp.exp(-x))


def _silu(x):
    return x * _sigmoid(x)


def _sample_idx(i):
    return jnp.clip(i - N_PT, 0, DEC_BATCH - 1)


def _prompt_idx(i):
    return jnp.minimum(i, N_PT - 1)


def _row_chunks(n_rows, body):
    def step(c, carry):
        body(pl.ds(pl.multiple_of(c * ROW_CHUNK, ROW_CHUNK), ROW_CHUNK))
        return carry

    lax.fori_loop(0, n_rows // ROW_CHUNK, step, 0)


def _mod_spec(l, k, width=D_MODEL, col=lambda j: 0):
    return pl.BlockSpec((None, None, 1, 1, width), lambda i, j: (l, k, i, 0, col(j)))


def _gain_spec(l, sub):
    return pl.BlockSpec((None, 1, D_MODEL), lambda i, j: (l * N_SUB + sub, 0, 0))


def _adaln_kernel(c_ref, w_ref, b_ref, o_ref):
    a = _silu(c_ref[...]).astype(BF16)
    o_ref[0] = _dot(a, w_ref[0].astype(BF16)) + b_ref[0]


def _adaln(cond8, w_mod, b_mod):
    n = w_mod.shape[-1]
    tn = 1024
    return pl.pallas_call(
        _adaln_kernel,
        grid=(DEPTH, n // tn),
        in_specs=[
            pl.BlockSpec((8, D_MODEL), lambda l, j: (0, 0)),
            pl.BlockSpec((1, D_MODEL, tn), lambda l, j: (l, 0, j)),
            pl.BlockSpec((1, 1, tn), lambda l, j: (l, 0, j)),
        ],
        out_specs=pl.BlockSpec((1, 8, tn), lambda l, j: (l, 0, j)),
        out_shape=jax.ShapeDtypeStruct((DEPTH, 8, n), F32),
        compiler_params=_cparams(("arbitrary", "arbitrary")),
        name="adaln",
    )(cond8, w_mod, b_mod.reshape(DEPTH, 1, n))


def _norm_mod(x, g, shift, scale):
    ms = jnp.mean(x * x, axis=-1, keepdims=True)
    y = x * lax.rsqrt(ms + EPS) * g
    return y * (1.0 + scale) + shift


def _ffn_kernel(x_ref, g_ref, sh_ref, sc_ref, gt_ref, wa_ref, wu_ref, wo_ref, *rest, final):
    if final:
        fg_ref, o_ref, h_scr = rest
    else:
        o_ref, h_scr = rest
    j = pl.program_id(1)

    @pl.when(j == 0)
    def _():
        def pre(rows):
            h = _norm_mod(x_ref[rows, :], g_ref[...], sh_ref[0], sc_ref[0])
            h_scr[rows, :] = h.astype(BF16)
            o_ref[rows, :] = jnp.zeros((ROW_CHUNK, D_MODEL), F32)

        _row_chunks(TM, pre)

    h = h_scr[...]
    a = _dot(h, wa_ref[...].astype(BF16))
    u = _dot(h, wu_ref[...].astype(BF16))
    mid = (_silu(a) * u).astype(BF16)
    wn = D_MODEL // FFN_NSPLIT
    for n in range(FFN_NSPLIT):
        cs = slice(n * wn, (n + 1) * wn)
        o_ref[:, cs] += _dot(mid, wo_ref[:, cs].astype(BF16))

    @pl.when(j == pl.num_programs(1) - 1)
    def _():
        def post(rows):
            out = x_ref[rows, :] + (0.5 * gt_ref[0]) * o_ref[rows, :]
            if final:
                ms = jnp.mean(out * out, axis=-1, keepdims=True)
                out = out * lax.rsqrt(ms + EPS) * fg_ref[...]
            o_ref[rows, :] = out

        _row_chunks(TM, post)


def _ffn(x, gains, modt, w_in, w_out, l, f, final_g=None):
    final = final_g is not None
    sub = 2 * f
    nf = D_FF // TF
    tok = pl.BlockSpec((TM, D_MODEL), lambda i, j: (i, 0))
    in_specs = [
        tok, _gain_spec(l, sub), _mod_spec(l, 3 * sub), _mod_spec(l, 3 * sub + 1), _mod_spec(l, 3 * sub + 2),
        pl.BlockSpec((None, None, D_MODEL, TF), lambda i, j: (l, f, 0, j)),
        pl.BlockSpec((None, None, D_MODEL, TF), lambda i, j: (l, f, 0, j + nf)),
        pl.BlockSpec((None, None, TF, D_MODEL), lambda i, j: (l, f, j, 0)),
    ]
    args = [x, gains, modt, modt, modt, w_in, w_in, w_out]
    if final:
        in_specs.append(pl.BlockSpec((1, D_MODEL), lambda i, j: (0, 0)))
        args.append(final_g.reshape(1, D_MODEL))
    return pl.pallas_call(
        functools.partial(_ffn_kernel, final=final),
        grid=(N_MT, nf),
        in_specs=in_specs,
        out_specs=tok,
        out_shape=jax.ShapeDtypeStruct((N_TOK, D_MODEL), F32),
        scratch_shapes=[pltpu.VMEM((TM, D_MODEL), BF16)],
        compiler_params=_cparams(("arbitrary", "arbitrary"), VMEM_LIMIT_FFN),
        name="ffn_final" if final else "ffn",
    )(*args)


def _resident_weight_spec(rows, l, ncol):
    return pl.BlockSpec((None, rows, TN), lambda i, j: (l, 0, jnp.where(i == 0, j, ncol - 1)))


def _inproj_kernel(x_ref, g_ref, sh_ref, sc_ref, w_ref, o_ref, h_scr, w_scr):
    i = pl.program_id(0)
    j = pl.program_id(1)

    @pl.when(j == 0)
    def _():
        def pre(rows):
            h = _norm_mod(x_ref[rows, :], g_ref[...], sh_ref[0], sc_ref[0])
            h_scr[rows, :] = h.astype(BF16)

        _row_chunks(TM, pre)

    @pl.when(i == 0)
    def _():
        w_scr[j] = w_ref[...].astype(BF16)

    o_ref[...] = _dot(h_scr[...], w_scr[j])


def _inproj(x, gains, modt, w, l):
    ncol = IN_W // TN
    return pl.pallas_call(
        _inproj_kernel,
        grid=(N_MT, ncol),
        in_specs=[
            pl.BlockSpec((TM, D_MODEL), lambda i, j: (i, 0)),
            _gain_spec(l, 1), _mod_spec(l, 3), _mod_spec(l, 4),
            _resident_weight_spec(D_MODEL, l, ncol),
        ],
        out_specs=pl.BlockSpec((TM, TN), lambda i, j: (i, j)),
        out_shape=jax.ShapeDtypeStruct((N_TOK, IN_W), F32),
        scratch_shapes=[pltpu.VMEM((TM, D_MODEL), BF16), pltpu.VMEM((ncol, D_MODEL, TN), BF16)],
        compiler_params=_cparams(("arbitrary", "arbitrary")),
        name="inproj",
    )(x, gains, modt, modt, w)


def _outproj_kernel(x_ref, a_ref, s_ref, r_ref, gt_ref, w_ref, o_ref, w_scr):
    i = pl.program_id(0)
    j = pl.program_id(1)

    @pl.when(i == 0)
    def _():
        w_scr[j] = w_ref[...].astype(BF16)

    y = _dot(a_ref[...], w_scr[j, 0:ATTN_W, :])
    y += _dot(s_ref[...], w_scr[j, ATTN_W:ATTN_W + SSM_W, :])
    y += _dot(r_ref[...], w_scr[j, ATTN_W + SSM_W:, :])
    o_ref[...] = x_ref[...] + gt_ref[0] * y


def _outproj(x, attn, ssm, ret, modt, w, l):
    ncol = D_MODEL // TN
    return pl.pallas_call(
        _outproj_kernel,
        grid=(N_MT, ncol),
        in_specs=[
            pl.BlockSpec((TM, TN), lambda i, j: (i, j)),
            pl.BlockSpec((TM, ATTN_W), lambda i, j: (i, 0)),
            pl.BlockSpec((TM, SSM_W), lambda i, j: (i, 0)),
            pl.BlockSpec((TM, RET_W), lambda i, j: (i, 0)),
            _mod_spec(l, 5, TN, lambda j: j),
            _resident_weight_spec(D_MODEL, l, ncol),
        ],
        out_specs=pl.BlockSpec((TM, TN), lambda i, j: (i, j)),
        out_shape=jax.ShapeDtypeStruct((N_TOK, D_MODEL), F32),
        scratch_shapes=[pltpu.VMEM((ncol, D_MODEL, TN), BF16)],
        compiler_params=_cparams(("arbitrary", "arbitrary")),
        name="outproj",
    )(x, attn, ssm, ret, modt, w)


def _head_rms(x, g):
    ms = jnp.mean(x * x, axis=-1, keepdims=True)
    return x * lax.rsqrt(ms + EPS) * g


def _rope(x, cos, sin_signed):
    lane = lax.broadcasted_iota(jnp.int32, x.shape, 1)
    partner = jnp.where((lane % 64) < 32, pltpu.roll(x, 96, 1), pltpu.roll(x, 32, 1))
    return x * cos + partner * sin_signed


def _attend_short(qs, k, v):
    s = _dot_nt(qs.astype(BF16), k) * (HEAD_DIM ** -0.5)
    p = jnp.exp(s - jnp.max(s, axis=-1, keepdims=True))
    l = jnp.sum(p, axis=-1, keepdims=True)
    return _dot(p.astype(BF16), v) / l


def _attend(qs, k, vt):
    st = _dot_nt(k, qs.astype(BF16)) * (HEAD_DIM ** -0.5)
    p = jnp.exp(st - jnp.max(st, axis=0, keepdims=True))
    l = jnp.sum(p, axis=0, keepdims=True)
    return (_dot(vt, p.astype(BF16)) / l).T


def _attn_kernel(q_ref, k_ref, v_ref, ck_ref, cv_ref, qg_ref, kg_ref, cos_ref, sin_ref,
                 o_ref, kc_ref, vc_ref, k_scr, vt_scr):
    i = pl.program_id(0)
    qg = qg_ref[...]
    kg = kg_ref[...]

    def heads(kv):
        return [slice((kv * Q_PER_KV + g) * HEAD_DIM, (kv * Q_PER_KV + g + 1) * HEAD_DIM) for g in range(Q_PER_KV)]

    @pl.when(i < N_PT)
    def _():
        def seq_body(sq, carry):
            rows = pl.ds(pl.multiple_of(sq * SEQ, SEQ), SEQ)
            v = v_ref[rows, :]
            vc_ref[sq] = v
            for kv in range(N_KV_HEADS):
                ksl = slice(kv * HEAD_DIM, (kv + 1) * HEAD_DIM)
                kn = _head_rms(k_ref[rows, ksl], kg)
                kc_ref[sq, :, ksl] = kn
                qs = jnp.concatenate([_head_rms(q_ref[rows, hs], qg) for hs in heads(kv)], axis=0)
                o = _attend_short(qs, kn.astype(BF16), v[:, ksl].astype(BF16))
                for g, hs in enumerate(heads(kv)):
                    o_ref[rows, hs] = o[g * SEQ:(g + 1) * SEQ].astype(BF16)
            return carry

        lax.fori_loop(0, SEQ_PER_TILE, seq_body, 0)

    @pl.when(i >= N_PT)
    def _():
        k_scr[0:PAST_LEN, :] = ck_ref[...].astype(BF16)
        for kv in range(N_KV_HEADS):
            ksl = slice(kv * HEAD_DIM, (kv + 1) * HEAD_DIM)
            vt_scr[kv, :, 0:PAST_LEN] = cv_ref[:, ksl].T.astype(BF16)
            vt_scr[kv, :, PAST_LEN:] = v_ref[:, ksl].T.astype(BF16)
            kn = _rope(_head_rms(k_ref[:, ksl], kg), cos_ref[...], sin_ref[...])
            k_scr[PAST_LEN:, ksl] = kn.astype(BF16)

        def q_body(qb, carry):
            rows = pl.ds(pl.multiple_of(qb * ATT_TQ, ATT_TQ), ATT_TQ)
            cos = cos_ref[rows, :]
            sin = sin_ref[rows, :]
            for kv in range(N_KV_HEADS):
                ksl = slice(kv * HEAD_DIM, (kv + 1) * HEAD_DIM)
                qs = jnp.concatenate([_rope(_head_rms(q_ref[rows, hs], qg), cos, sin) for hs in heads(kv)], axis=0)
                o = _attend(qs, k_scr[:, ksl], vt_scr[kv])
                for g, hs in enumerate(heads(kv)):
                    o_ref[rows, hs] = o[g * ATT_TQ:(g + 1) * ATT_TQ].astype(BF16)
            return carry

        lax.fori_loop(0, DEC_SEQ // ATT_TQ, q_body, 0)


def _attention(p, cache_k, cache_v, qg, kg, cos_t, sin_t, l):
    cache_spec = pl.BlockSpec((None, None, PAST_LEN, KV_W), lambda i: (_sample_idx(i), l, 0, 0))
    vec = pl.BlockSpec((None, 1, HEAD_DIM), lambda i: (l, 0, 0))
    tab = pl.BlockSpec((DEC_SEQ, HEAD_DIM), lambda i: (0, 0))
    new_cache = pl.BlockSpec((SEQ_PER_TILE, SEQ, KV_W), lambda i: (_prompt_idx(i), 0, 0))
    return pl.pallas_call(
        _attn_kernel,
        grid=(N_MT,),
        in_specs=[
            pl.BlockSpec((TM, ATTN_W), lambda i: (i, 0)),
            pl.BlockSpec((TM, KV_W), lambda i: (i, COL_K)),
            pl.BlockSpec((TM, KV_W), lambda i: (i, COL_V)),
            cache_spec, cache_spec, vec, vec, tab, tab,
        ],
        out_specs=[pl.BlockSpec((TM, ATTN_W), lambda i: (i, 0)), new_cache, new_cache],
        out_shape=[
            jax.ShapeDtypeStruct((N_TOK, ATTN_W), BF16),
            jax.ShapeDtypeStruct((BATCH, SEQ, KV_W), F32),
            jax.ShapeDtypeStruct((BATCH, SEQ, KV_W), F32),
        ],
        scratch_shapes=[pltpu.VMEM((ATT_S, KV_W), BF16), pltpu.VMEM((N_KV_HEADS, HEAD_DIM, ATT_S), BF16)],
        compiler_params=_cparams(("arbitrary",)),
        name="attention",
    )(p, p, p, cache_k.reshape(DEC_BATCH, DEPTH, PAST_LEN, KV_W), cache_v.reshape(DEC_BATCH, DEPTH, PAST_LEN, KV_W),
      qg.reshape(DEPTH, 1, HEAD_DIM), kg.reshape(DEPTH, 1, HEAD_DIM), cos_t, sin_t)


def _log_sigmoid(x):
    return -(jnp.maximum(-x, 0.0) + jnp.log(1.0 + jnp.exp(-jnp.abs(x))))


def _retention_kernel(q_ref, k_ref, v_ref, g_ref, dl_ref, cos_ref, sin_ref, s0_ref, o_ref, sf_ref, acc):
    i = pl.program_id(0)
    C = RET_CHUNK
    ii = lax.broadcasted_iota(jnp.int32, (C, C), 0).astype(F32)
    jj = lax.broadcasted_iota(jnp.int32, (C, C), 1).astype(F32)

    decs = {}
    for d in range(2):
        for h in range(RET_HEADS):
            lg = _log_sigmoid(dl_ref[d, h][0:1, :])
            if d == 0:
                diff = ii - jj
                q_dec = jnp.exp((ii + 1.0) * lg)
                k_dec = jnp.exp((C - 1.0 - ii) * lg)
            else:
                diff = jj - ii
                q_dec = jnp.exp((C - ii) * lg)
                k_dec = jnp.exp(ii * lg)
            inner = jnp.where(diff >= 0, jnp.exp(jnp.maximum(diff, 0.0) * lg), 0.0)
            decs[d, h] = (inner, q_dec, k_dec, jnp.exp(C * lg))

    def scan_seq(row0, n, rope, init):
        states = {(d, h): init(d, h) for d in range(2) for h in range(RET_HEADS)}
        for t in range(n):
            for d in range(2):
                c = t if d == 0 else n - 1 - t
                rows = slice(row0 + c * C, row0 + (c + 1) * C)
                for h in range(RET_HEADS):
                    hs = slice(h * HEAD_DIM, (h + 1) * HEAD_DIM)
                    inner, q_dec, k_dec, c_dec = decs[d, h]
                    qc = q_ref[rows, hs]
                    kc = k_ref[rows, hs] * (HEAD_DIM ** -0.5)
                    vc = v_ref[rows, hs].astype(BF16)
                    if rope:
                        qc = _rope(qc, cos_ref[rows, :], sin_ref[rows, :])
                        kc = _rope(kc, cos_ref[rows, :], sin_ref[rows, :])
                    s = states[d, h]
                    att = (_dot_nt(qc.astype(BF16), kc.astype(BF16)) * inner).astype(BF16)
                    o = _dot(att, vc) + _dot(qc.astype(BF16), s.astype(BF16)) * q_dec
                    if 2 * t < n:
                        acc[rows, hs] = o
                    else:
                        acc[rows, hs] += o
                    kd = (kc * k_dec).T.astype(BF16)
                    states[d, h] = s * c_dec + _dot(kd, vc)
        return states

    @pl.when(i < N_PT)
    def _():
        for sq in range(SEQ_PER_TILE):
            fin = scan_seq(sq * SEQ, SEQ // C, False, lambda d, h: jnp.zeros((HEAD_DIM, HEAD_DIM), F32))
            for (d, h), s in fin.items():
                sf_ref[sq, d, h] = s

    @pl.when(i >= N_PT)
    def _():
        scan_seq(0, DEC_SEQ // C, True, lambda d, h: s0_ref[d, h])

    for h in range(RET_HEADS):
        hs = slice(h * HEAD_DIM, (h + 1) * HEAD_DIM)
        o = acc[:, hs]
        o = o - jnp.mean(o, axis=-1, keepdims=True)
        o = o * lax.rsqrt(jnp.mean(o * o, axis=-1, keepdims=True) + EPS)
        o_ref[:, hs] = (o * _silu(g_ref[:, hs])).astype(BF16)


def _retention(p, dl, cos_t, sin_t, state_ret, l):
    tab = pl.BlockSpec((DEC_SEQ, HEAD_DIM), lambda i: (0, 0))
    st = (2, RET_HEADS, HEAD_DIM, HEAD_DIM)
    return pl.pallas_call(
        _retention_kernel,
        grid=(N_MT,),
        in_specs=[pl.BlockSpec((TM, RET_W), lambda i, k=k: (i, COL_R + k)) for k in range(4)] + [
            pl.BlockSpec((None, 2, RET_HEADS, 8, HEAD_DIM), lambda i: (l, 0, 0, 0, 0)),
            tab, tab,
            pl.BlockSpec((None, None) + st, lambda i: (_sample_idx(i), l, 0, 0, 0, 0)),
        ],
        out_specs=[
            pl.BlockSpec((TM, RET_W), lambda i: (i, 0)),
            pl.BlockSpec((SEQ_PER_TILE,) + st, lambda i: (_prompt_idx(i), 0, 0, 0, 0)),
        ],
        out_shape=[
            jax.ShapeDtypeStruct((N_TOK, RET_W), BF16),
            jax.ShapeDtypeStruct((BATCH,) + st, F32),
        ],
        scratch_shapes=[pltpu.VMEM((TM, RET_W), F32)],
        compiler_params=_cparams(("arbitrary",)),
        name="retention",
    )(p, p, p, p, dl, cos_t, sin_t, state_ret)


def _s5_disc_kernel(are_ref, aim_ref, ldt_ref, bre_ref, bim_ref, cim_ref,
                    lre_ref, lim_ref, bbre_ref, bbim_ref, ncim_ref):
    ar = are_ref[...]
    ai = aim_ref[...]
    dt = jnp.exp(ldt_ref[...])
    mag = jnp.exp(ar * dt)
    lr = mag * jnp.cos(ai * dt)
    li = mag * jnp.sin(ai * dt)
    den = ar * ar + ai * ai
    nr = lr - 1.0
    cr = (nr * ar + li * ai) / den
    ci = (li * ar - nr * ai) / den
    br = bre_ref[...]
    bi = bim_ref[...]
    lre_ref[...] = lr
    lim_ref[...] = li
    bbre_ref[...] = cr * br - ci * bi
    bbim_ref[...] = cr * bi + ci * br
    ncim_ref[...] = -cim_ref[...]


def _s5_discretize(a_re, a_im, log_dt, b_re, b_im, c_im):
    rows = DEPTH * 2 * SSM_GROUPS
    cols = SSM_STATE * SSM_GROUP
    shp = (DEPTH, 2, SSM_GROUPS, SSM_STATE, SSM_GROUP)
    args = [
        jnp.broadcast_to(a_re[..., None], shp).reshape(rows, cols),
        jnp.broadcast_to(a_im[..., None], shp).reshape(rows, cols),
        jnp.broadcast_to(log_dt[..., None, None], shp).reshape(rows, cols),
        b_re.reshape(rows, cols), b_im.reshape(rows, cols), c_im.reshape(rows, cols),
    ]
    spec = pl.BlockSpec((rows, cols), lambda: (0, 0))
    outs = pl.pallas_call(
        _s5_disc_kernel,
        in_specs=[spec] * 6,
        out_specs=[spec] * 5,
        out_shape=[jax.ShapeDtypeStruct((rows, cols), F32)] * 5,
        name="s5_discretize",
    )(*args)
    lre, lim, bbre, bbim, ncim = outs
    lre = lre.reshape(shp)[..., 0]
    lim = lim.reshape(shp)[..., 0]
    return (lre, lim, bbre.reshape(shp), bbim.reshape(shp),
            ncim.reshape(DEPTH, 2, SSM_GROUPS, SSM_GROUP, SSM_STATE))


def _state_cols(x):
    lead = x.shape[:-3]
    x = x.reshape(lead + (N_STILE, 2, SSM_STATE, 2))
    x = jnp.moveaxis(x, -1, -3)
    return x.reshape(lead + (STATE_W,))


def _state_uncols(x):
    lead = x.shape[:-1]
    x = x.reshape(lead + (N_STILE, 2, 2, SSM_STATE))
    x = jnp.moveaxis(x, -3, -1)
    return x.reshape(lead + (SSM_GROUPS, SSM_STATE, 2))


def _s5_compact(lre, lim, bbre, bbim, c_re, ncim):
    n = DEPTH * 2
    bb = jnp.stack([bbre, bbim], axis=-1)
    bb = bb.reshape(n, N_STILE, 2, SSM_STATE, SSM_GROUP, 2)
    wb = jnp.transpose(bb, (0, 4, 1, 5, 2, 3)).reshape(DEPTH, 2, SSM_GROUP, STATE_W)
    cc = jnp.stack([c_re, ncim], axis=-1)
    cc = cc.reshape(n, N_STILE, 2, SSM_GROUP, SSM_STATE, 2)
    wc = jnp.transpose(cc, (0, 3, 1, 5, 2, 4)).reshape(DEPTH, 2, SSM_GROUP, STATE_W)
    lam = jnp.concatenate([lre.reshape(DEPTH, 2, N_STILE, 128), lim.reshape(DEPTH, 2, N_STILE, 128)], axis=2)
    return wb, wc, lam


def _group_mask(cols):
    row_g = lax.broadcasted_iota(jnp.int32, (256, cols), 0) // SSM_GROUP
    col = lax.broadcasted_iota(jnp.int32, (256, cols), 1)
    col_g = ((col // 256) % 8) * 2 + (col % 128) // SSM_STATE
    return row_g == col_g


def _s5_scan_kernel(u_ref, wb_ref, wc_ref, lam_ref, h0_ref, y_ref, hf_ref,
                    pm_scr, bm_scr, cm_scr, hbuf, y_scr, state, ends, *, segments):
    npass = 1 if segments == 1 else 2
    d = pl.program_id(0)
    ps = pl.program_id(1)
    ck = pl.program_id(2)
    nck = pl.num_programs(2)
    rows = SSM_TC * SSM_ROWS
    half = STATE_W // 2
    lam = lam_ref[...]

    @pl.when(jnp.logical_and(ps == 0, ck == 0))
    def _():
        a = lax.broadcasted_iota(jnp.int32, (rows, rows), 0)
        b = lax.broadcasted_iota(jnp.int32, (rows, rows), 1)
        same = jnp.logical_and(a // SSM_ROWS == b % SSM_TC, a % SSM_ROWS == b // SSM_TC)
        pm_scr[...] = jnp.where(same, 1.0, 0.0).astype(BF16)
        wb = jnp.tile(wb_ref[...], (256 // SSM_GROUP, 1))
        bm_scr[...] = jnp.where(_group_mask(STATE_W), wb, 0.0).astype(BF16)
        for n_ in range(2):
            wc = jnp.tile(wc_ref[:, n_ * half:(n_ + 1) * half], (256 // SSM_GROUP, 1))
            cm_scr[n_] = jnp.where(_group_mask(half), wc, 0.0).astype(BF16)

    @pl.when(ck == 0)
    def _():
        if npass == 1:
            state[...] = h0_ref[0]
        else:
            @pl.when(ps == 0)
            def _():
                state[...] = jnp.zeros_like(state)

            @pl.when(ps == 1)
            def _():
                ends[...] = state[...]
                state[...] = h0_ref[0]
                lr = lam[0:N_STILE]
                li = lam[N_STILE:]
                for _ in range(int(math.log2(SSM_SEG_LEN))):
                    lr, li = lr * lr - li * li, 2.0 * lr * li

                def carry(order, prev):
                    for sg in order:
                        for sq in range(SSM_ROWS // segments):
                            r = sq * segments + sg
                            q = r + prev
                            for j in range(N_STILE):
                                re = slice(j * 256, j * 256 + 128)
                                im = slice(j * 256 + 128, (j + 1) * 256)
                                pr = state[q:q + 1, re]
                                pi = state[q:q + 1, im]
                                ar = lr[j:j + 1]
                                ai = li[j:j + 1]
                                state[r:r + 1, re] = ends[q:q + 1, re] + ar * pr - ai * pi
                                state[r:r + 1, im] = ends[q:q + 1, im] + ar * pi + ai * pr

                @pl.when(d == 0)
                def _():
                    carry(range(1, segments), -1)

                @pl.when(d == 1)
                def _():
                    carry(range(segments - 2, -1, -1), 1)

    u = _dot(pm_scr[...], u_ref[...].reshape(rows, SSM_W).astype(BF16)).astype(BF16)
    for j in range(N_STILE):
        k0 = (j // 8) * 256
        hbuf[:, j * 256:(j + 1) * 256] = _dot(u[:, k0:k0 + 256], bm_scr[:, j * 256:(j + 1) * 256])

    JT = 4
    for jb in range(N_STILE // JT):
        tiles = list(range(jb * JT, (jb + 1) * JT))
        lrs = [jnp.broadcast_to(lam[j:j + 1], (SSM_ROWS, 128)) for j in tiles]
        lis = [jnp.broadcast_to(lam[N_STILE + j:N_STILE + j + 1], (SSM_ROWS, 128)) for j in tiles]
        init = tuple(state[:, j * 256:j * 256 + 128] for j in tiles) + \
            tuple(state[:, j * 256 + 128:(j + 1) * 256] for j in tiles)

        def body(t, carry_, tiles=tiles, lrs=lrs, lis=lis):
            tt = t + d * (SSM_TC - 1 - 2 * t)
            r = pl.ds(pl.multiple_of(tt * SSM_ROWS, SSM_ROWS), SSM_ROWS)
            out_r, out_i = [], []
            for n_, j in enumerate(tiles):
                hr, hi = carry_[n_], carry_[JT + n_]
                re = slice(j * 256, j * 256 + 128)
                im = slice(j * 256 + 128, (j + 1) * 256)
                nr = lrs[n_] * hr - lis[n_] * hi + hbuf[r, re]
                ni = lrs[n_] * hi + lis[n_] * hr + hbuf[r, im]
                hbuf[r, re] = nr
                hbuf[r, im] = ni
                out_r.append(nr)
                out_i.append(ni)
            return tuple(out_r) + tuple(out_i)

        fin = lax.fori_loop(0, SSM_TC, body, init, unroll=2)
        for n_, j in enumerate(tiles):
            state[:, j * 256:j * 256 + 128] = fin[n_]
            state[:, j * 256 + 128:(j + 1) * 256] = fin[JT + n_]

    @pl.when(ps == npass - 1)
    def _():
        hb = hbuf[...].astype(BF16)
        for n_ in range(2):
            y = _dot_nt(hb[:, n_ * half:(n_ + 1) * half], cm_scr[n_])
            y_scr[2 * n_] = y[:, :128]
            y_scr[2 * n_ + 1] = y[:, 128:]
        for r in range(SSM_ROWS):
            for c_ in range(SSM_W // 128):
                y_ref[r, :, c_ * 128:(c_ + 1) * 128] = y_scr[c_, pl.ds(r, SSM_TC, stride=SSM_ROWS), :]

    @pl.when(jnp.logical_and(ps == npass - 1, ck == nck - 1))
    def _():
        hf_ref[0] = state[...]


def _s5_scan(p3, row_block, wb, wc, lam, h0, l, *, segments):
    steps = p3.shape[1]
    nck = steps // SSM_TC
    npass = 1 if segments == 1 else 2

    def chunk(d, c):
        return c + d * (nck - 1 - 2 * c)

    def y_chunk(d, p, c):
        return jnp.where(p == npass - 1, chunk(d, c), chunk(d, 0))

    par = pl.BlockSpec((None, None, SSM_GROUP, STATE_W), lambda d, p, c: (l, d, 0, 0))
    st = pl.BlockSpec((1, SSM_ROWS, STATE_W), lambda d, p, c: (d, 0, 0))
    return pl.pallas_call(
        functools.partial(_s5_scan_kernel, segments=segments),
        grid=(2, npass, nck),
        in_specs=[
            pl.BlockSpec((SSM_ROWS, SSM_TC, SSM_W), lambda d, p, c: (row_block, chunk(d, c), COL_U)),
            par, par,
            pl.BlockSpec((None, None, 2 * N_STILE, 128), lambda d, p, c: (l, d, 0, 0)),
            st,
        ],
        out_specs=[
            pl.BlockSpec((None, SSM_ROWS, SSM_TC, SSM_W), lambda d, p, c: (d, 0, y_chunk(d, p, c), 0)),
            st,
        ],
        out_shape=[
            jax.ShapeDtypeStruct((2, SSM_ROWS, steps, SSM_W), F32),
            jax.ShapeDtypeStruct((2, SSM_ROWS, STATE_W), F32),
        ],
        scratch_shapes=[
            pltpu.VMEM((SSM_TC * SSM_ROWS, SSM_TC * SSM_ROWS), BF16),
            pltpu.VMEM((256, STATE_W), BF16),
            pltpu.VMEM((2, 256, STATE_W // 2), BF16),
            pltpu.VMEM((SSM_TC * SSM_ROWS, STATE_W), F32),
            pltpu.VMEM((SSM_W // 128, SSM_TC * SSM_ROWS, 128), F32),
            pltpu.VMEM((SSM_ROWS, STATE_W), F32),
            pltpu.VMEM((SSM_ROWS, STATE_W), F32),
        ],
        compiler_params=_cparams(("arbitrary", "arbitrary", "arbitrary")),
        name="s5_scan_seg%d" % segments,
    )(p3, wb, wc, lam, h0)


def _s5_glu_kernel(ypf_ref, ypb_ref, ysf_ref, ysb_ref, u_ref, d_ref, w_ref, o_ref):
    i = pl.program_id(0)
    w = w_ref[...].astype(BF16)

    def glu(y):
        y = y + d_ref[...] * u_ref[...]
        z = _dot(jax.nn.gelu(y).astype(BF16), w)
        o_ref[...] = (z[:, :SSM_W] * _sigmoid(z[:, SSM_W:])).astype(BF16)

    @pl.when(i < N_PT)
    def _():
        glu(ypf_ref[...] + ypb_ref[...])

    @pl.when(i >= N_PT)
    def _():
        glu(ysf_ref[...] + ysb_ref[...])


def _s5_glu(yp, ys, p, d, w, l):
    return pl.pallas_call(
        _s5_glu_kernel,
        grid=(N_MT,),
        in_specs=[
            pl.BlockSpec((None, TM, SSM_W), lambda i: (0, _prompt_idx(i), 0)),
            pl.BlockSpec((None, TM, SSM_W), lambda i: (1, _prompt_idx(i), 0)),
            pl.BlockSpec((None, TM, SSM_W), lambda i: (0, _sample_idx(i), 0)),
            pl.BlockSpec((None, TM, SSM_W), lambda i: (1, _sample_idx(i), 0)),
            pl.BlockSpec((TM, SSM_W), lambda i: (i, COL_U)),
            pl.BlockSpec((None, 1, SSM_W), lambda i: (l, 0, 0)),
            pl.BlockSpec((None, SSM_W, 2 * SSM_W), lambda i: (l, 0, 0)),
        ],
        out_specs=pl.BlockSpec((TM, SSM_W), lambda i: (i, 0)),
        out_shape=jax.ShapeDtypeStruct((N_TOK, SSM_W), BF16),
        compiler_params=_cparams(("arbitrary",)),
        name="s5_glu",
    )(yp, yp, ys, ys, p, d.reshape(DEPTH, 1, SSM_W), w)


def _s5_layer(p, s5p, st_l, ssm_d, w_ssm_glu, l):
    wb, wc, lam = s5p
    yp, hfin = _s5_scan(p.reshape(N_TOK // SEQ, SEQ, IN_W), 0, wb, wc, lam,
                        jnp.zeros((2, SSM_ROWS, STATE_W), F32), l, segments=1)
    h0 = _state_cols(jnp.moveaxis(st_l, 0, 1))
    h0 = jnp.stack([
        jnp.zeros((DEC_BATCH, SSM_SEGS, STATE_W), F32).at[:, 0].set(h0[0]),
        jnp.zeros((DEC_BATCH, SSM_SEGS, STATE_W), F32).at[:, SSM_SEGS - 1].set(h0[1]),
    ]).reshape(2, SSM_ROWS, STATE_W)
    ys, _ = _s5_scan(p.reshape(N_TOK // SSM_SEG_LEN, SSM_SEG_LEN, IN_W), N_PROMPT // SSM_SEG_LEN // SSM_ROWS,
                     wb, wc, lam, h0, l, segments=SSM_SEGS)
    out = _s5_glu(yp.reshape(2, N_PROMPT, SSM_W), ys.reshape(2, N_SAMPLE, SSM_W), p, ssm_d, w_ssm_glu, l)
    new_state = jnp.moveaxis(_state_uncols(hfin), 0, 1)
    return out, new_state


def _rope_tables():
    rows = DEC_SEQ // GRID_W
    r = jnp.repeat(jnp.arange(rows), GRID_W).astype(F32)
    col = jnp.tile(jnp.arange(GRID_W), rows).astype(F32)
    inv = ROPE_BASE ** (-jnp.arange(ROPE_FREQS, dtype=F32) / ROPE_FREQS)
    ang = jnp.stack([r, col], axis=-1)[:, :, None] * inv
    cos = jnp.cos(ang)
    sin = jnp.sin(ang)
    cos_t = jnp.concatenate([cos, cos], axis=-1).reshape(DEC_SEQ, HEAD_DIM)
    sin_t = jnp.concatenate([-sin, sin], axis=-1).reshape(DEC_SEQ, HEAD_DIM)
    return cos_t, sin_t


def kernel(x_prompt, x_sample, cache_k, cache_v, state_ssm, state_ret, c, c_ctx, w_mod, b_mod, norm_g,
           w_ffn_in, w_ffn_out, w_in, w_out, q_norm_g, k_norm_g, ssm_a_re, ssm_a_im, ssm_log_dt,
           ssm_b_re, ssm_b_im, ssm_c_re, ssm_c_im, ssm_d, w_ssm_glu, ret_decay_logit, final_norm_g):
    x = jnp.concatenate([x_prompt.reshape(N_PROMPT, D_MODEL), x_sample.reshape(N_SAMPLE, D_MODEL)], axis=0)
    cond8 = jnp.concatenate([c_ctx[None], c, jnp.zeros((5, D_MODEL), F32)], axis=0)
    mods = _adaln(cond8, w_mod, b_mod).reshape(DEPTH, 8, N_SUB * 3, D_MODEL)
    modt = jnp.transpose(mods[:, jnp.array(TILE_MOD, jnp.int32)], (0, 2, 1, 3))[:, :, :, None, :]
    gains = norm_g.reshape(DEPTH * N_SUB, 1, D_MODEL)
    cos_t, sin_t = _rope_tables()
    lre, lim, bbre, bbim, ncim = _s5_discretize(ssm_a_re, ssm_a_im, ssm_log_dt, ssm_b_re, ssm_b_im, ssm_c_im)
    s5p = _s5_compact(lre, lim, bbre, bbim, ssm_c_re, ncim)
    dl = jnp.broadcast_to(ret_decay_logit[:, :, :, None, None], (DEPTH, 2, RET_HEADS, 8, HEAD_DIM))
    ks_, vs_, hs_, ss_ = [], [], [], []
    for l in range(DEPTH):
        x = _ffn(x, gains, modt, w_ffn_in, w_ffn_out, l, 0)
        p = _inproj(x, gains, modt, w_in, l)
        attn, k_l, v_l = _attention(p, cache_k, cache_v, q_norm_g, k_norm_g, cos_t, sin_t, l)
        ssm, h_l = _s5_layer(p, s5p, state_ssm[:, l], ssm_d, w_ssm_glu, l)
        ret, s_l = _retention(p, dl, cos_t, sin_t, state_ret, l)
        x = _outproj(x, attn, ssm, ret, modt, w_out, l)
        x = _ffn(x, gains, modt, w_ffn_in, w_ffn_out, l, 1, final_g=final_norm_g if l == DEPTH - 1 else None)
        ks_.append(k_l.reshape(BATCH, SEQ, N_KV_HEADS, HEAD_DIM))
        vs_.append(v_l.reshape(BATCH, SEQ, N_KV_HEADS, HEAD_DIM))
        hs_.append(h_l)
        ss_.append(s_l)
    y_prompt = x[:N_PROMPT].reshape(BATCH, SEQ, D_MODEL)
    y_sample = x[N_PROMPT:].reshape(DEC_BATCH, DEC_SEQ, D_MODEL)
    return (y_prompt, y_sample, jnp.stack(ks_, axis=1), jnp.stack(vs_, axis=1),
            jnp.stack(hs_, axis=1), jnp.stack(ss_, axis=1))
```

```python
import functools
import math

import jax
import jax.numpy as jnp
from jax import lax
from jax.experimental import pallas as pl
from jax.experimental.pallas import tpu as pltpu

D_MODEL = 2048
BATCH = 16
SEQ = 256
DEPTH = 2
DEC_BATCH = 2
DEC_SEQ = 1024
PAST_LEN = 512
GRID_W = 64
HEAD_DIM = 128
N_Q_HEADS = 8
N_KV_HEADS = 2
Q_PER_KV = 4
ATTN_W = 1024
KV_W = 256
SSM_W = 512
SSM_GROUP = 16
SSM_GROUPS = 32
SSM_STATE = 64
RET_HEADS = 4
RET_W = 512
IN_W = 4096
D_FF = 5632
N_SUB = 3
RET_CHUNK = 128
ROPE_BASE = 10000.0
ROPE_FREQS = 32
EPS = 1e-6

N_PROMPT = BATCH * SEQ
N_SAMPLE = DEC_BATCH * DEC_SEQ
N_TOK = N_PROMPT + N_SAMPLE
TM = 1024
N_MT = N_TOK // TM
N_PT = N_PROMPT // TM
SEQ_PER_TILE = TM // SEQ
TILE_MOD = (0, 0, 0, 0, 1, 2)
ROW_CHUNK = 128
TF = 256
FFN_NSPLIT = 4
TN = 512
STATE_W = 2 * SSM_GROUPS * SSM_STATE
N_STILE = STATE_W // 256
SSM_ROWS = 16
SSM_SEGS = 8
SSM_SEG_LEN = DEC_SEQ // SSM_SEGS
SSM_TC = 32
ATT_TQ = 256
ATT_S = PAST_LEN + DEC_SEQ
VMEM_LIMIT = 56 * 1024 * 1024
VMEM_LIMIT_FFN = 60 * 1024 * 1024

COL_K = ATTN_W // KV_W
COL_V = COL_K + 1
COL_U = (ATTN_W + 2 * KV_W) // SSM_W
COL_R = (ATTN_W + 2 * KV_W + SSM_W) // RET_W

BF16 = jnp.bfloat16
F32 = jnp.float32


def _cparams(sem, limit=VMEM_LIMIT):
    return pltpu.CompilerParams(dimension_semantics=sem, vmem_limit_bytes=limit)


def _dot(a, b):
    return jnp.dot(a, b, preferred_element_type=F32)


def _dot_nt(a, b):
    return lax.dot_general(a, b, (((1,), (1,)), ((), ())), preferred_element_type=F32)


def _sigmoid(x):
    return 1.0 / (1.0 + jnp.exp(-x))


def _silu(x):
    return x * _sigmoid(x)


def _sample_idx(i):
    return jnp.clip(i - N_PT, 0, DEC_BATCH - 1)


def _prompt_idx(i):
    return jnp.minimum(i, N_PT - 1)


def _row_chunks(n_rows, body):
    def step(c, carry):
        body(pl.ds(pl.multiple_of(c * ROW_CHUNK, ROW_CHUNK), ROW_CHUNK))
        return carry

    lax.fori_loop(0, n_rows // ROW_CHUNK, step, 0)


def _mod_spec(l, k, width=D_MODEL, col=lambda j: 0):
    return pl.BlockSpec((None, None, 1, 1, width), lambda i, j: (l, k, i, 0, col(j)))


def _gain_spec(l, sub):
    return pl.BlockSpec((None, 1, D_MODEL), lambda i, j: (l * N_SUB + sub, 0, 0))


def _adaln_kernel(c_ref, w_ref, b_ref, o_ref):
    a = _silu(c_ref[...]).astype(BF16)
    o_ref[0] = _dot(a, w_ref[0].astype(BF16)) + b_ref[0]


def _adaln(cond8, w_mod, b_mod):
    n = w_mod.shape[-1]
    tn = 1024
    return pl.pallas_call(
        _adaln_kernel,
        grid=(DEPTH, n // tn),
        in_specs=[
            pl.BlockSpec((8, D_MODEL), lambda l, j: (0, 0)),
            pl.BlockSpec((1, D_MODEL, tn), lambda l, j: (l, 0, j)),
            pl.BlockSpec((1, 1, tn), lambda l, j: (l, 0, j)),
        ],
        out_specs=pl.BlockSpec((1, 8, tn), lambda l, j: (l, 0, j)),
        out_shape=jax.ShapeDtypeStruct((DEPTH, 8, n), F32),
        compiler_params=_cparams(("arbitrary", "arbitrary")),
        name="adaln",
    )(cond8, w_mod, b_mod.reshape(DEPTH, 1, n))


def _norm_mod(x, g, shift, scale):
    ms = jnp.mean(x * x, axis=-1, keepdims=True)
    y = x * lax.rsqrt(ms + EPS) * g
    return y * (1.0 + scale) + shift


def _ffn_kernel(x_ref, g_ref, sh_ref, sc_ref, gt_ref, wa_ref, wu_ref, wo_ref, *rest, final):
    if final:
        fg_ref, o_ref, h_scr = rest
    else:
        o_ref, h_scr = rest
    j = pl.program_id(1)

    @pl.when(j == 0)
    def _():
        def pre(rows):
            h = _norm_mod(x_ref[rows, :], g_ref[...], sh_ref[0], sc_ref[0])
            h_scr[rows, :] = h.astype(BF16)
            o_ref[rows, :] = jnp.zeros((ROW_CHUNK, D_MODEL), F32)

        _row_chunks(TM, pre)

    h = h_scr[...]
    a = _dot(h, wa_ref[...].astype(BF16))
    u = _dot(h, wu_ref[...].astype(BF16))
    mid = (_silu(a) * u).astype(BF16)
    wn = D_MODEL // FFN_NSPLIT
    for n in range(FFN_NSPLIT):
        cs = slice(n * wn, (n + 1) * wn)
        o_ref[:, cs] += _dot(mid, wo_ref[:, cs].astype(BF16))

    @pl.when(j == pl.num_programs(1) - 1)
    def _():
        def post(rows):
            out = x_ref[rows, :] + (0.5 * gt_ref[0]) * o_ref[rows, :]
            if final:
                ms = jnp.mean(out * out, axis=-1, keepdims=True)
                out = out * lax.rsqrt(ms + EPS) * fg_ref[...]
            o_ref[rows, :] = out

        _row_chunks(TM, post)


def _ffn(x, gains, modt, w_in, w_out, l, f, final_g=None):
    final = final_g is not None
    sub = 2 * f
    nf = D_FF // TF
    tok = pl.BlockSpec((TM, D_MODEL), lambda i, j: (i, 0))
    in_specs = [
        tok, _gain_spec(l, sub), _mod_spec(l, 3 * sub), _mod_spec(l, 3 * sub + 1), _mod_spec(l, 3 * sub + 2),
        pl.BlockSpec((None, None, D_MODEL, TF), lambda i, j: (l, f, 0, j)),
        pl.BlockSpec((None, None, D_MODEL, TF), lambda i, j: (l, f, 0, j + nf)),
        pl.BlockSpec((None, None, TF, D_MODEL), lambda i, j: (l, f, j, 0)),
    ]
    args = [x, gains, modt, modt, modt, w_in, w_in, w_out]
    if final:
        in_specs.append(pl.BlockSpec((1, D_MODEL), lambda i, j: (0, 0)))
        args.append(final_g.reshape(1, D_MODEL))
    return pl.pallas_call(
        functools.partial(_ffn_kernel, final=final),
        grid=(N_MT, nf),
        in_specs=in_specs,
        out_specs=tok,
        out_shape=jax.ShapeDtypeStruct((N_TOK, D_MODEL), F32),
        scratch_shapes=[pltpu.VMEM((TM, D_MODEL), BF16)],
        compiler_params=_cparams(("arbitrary", "arbitrary"), VMEM_LIMIT_FFN),
        name="ffn_final" if final else "ffn",
    )(*args)


def _resident_weight_spec(rows, l, ncol):
    return pl.BlockSpec((None, rows, TN), lambda i, j: (l, 0, jnp.where(i == 0, j, ncol - 1)))


def _inproj_kernel(x_ref, g_ref, sh_ref, sc_ref, w_ref, o_ref, h_scr, w_scr):
    i = pl.program_id(0)
    j = pl.program_id(1)

    @pl.when(j == 0)
    def _():
        def pre(rows):
            h = _norm_mod(x_ref[rows, :], g_ref[...], sh_ref[0], sc_ref[0])
            h_scr[rows, :] = h.astype(BF16)

        _row_chunks(TM, pre)

    @pl.when(i == 0)
    def _():
        w_scr[j] = w_ref[...].astype(BF16)

    o_ref[...] = _dot(h_scr[...], w_scr[j])


def _inproj(x, gains, modt, w, l):
    ncol = IN_W // TN
    return pl.pallas_call(
        _inproj_kernel,
        grid=(N_MT, ncol),
        in_specs=[
            pl.BlockSpec((TM, D_MODEL), lambda i, j: (i, 0)),
            _gain_spec(l, 1), _mod_spec(l, 3), _mod_spec(l, 4),
            _resident_weight_spec(D_MODEL, l, ncol),
        ],
        out_specs=pl.BlockSpec((TM, TN), lambda i, j: (i, j)),
        out_shape=jax.ShapeDtypeStruct((N_TOK, IN_W), F32),
        scratch_shapes=[pltpu.VMEM((TM, D_MODEL), BF16), pltpu.VMEM((ncol, D_MODEL, TN), BF16)],
        compiler_params=_cparams(("arbitrary", "arbitrary")),
        name="inproj",
    )(x, gains, modt, modt, w)


def _outproj_kernel(x_ref, a_ref, s_ref, r_ref, gt_ref, w_ref, o_ref, w_scr):
    i = pl.program_id(0)
    j = pl.program_id(1)

    @pl.when(i == 0)
    def _():
        w_scr[j] = w_ref[...].astype(BF16)

    y = _dot(a_ref[...], w_scr[j, 0:ATTN_W, :])
    y += _dot(s_ref[...], w_scr[j, ATTN_W:ATTN_W + SSM_W, :])
    y += _dot(r_ref[...], w_scr[j, ATTN_W + SSM_W:, :])
    o_ref[...] = x_ref[...] + gt_ref[0] * y


def _outproj(x, attn, ssm, ret, modt, w, l):
    ncol = D_MODEL // TN
    return pl.pallas_call(
        _outproj_kernel,
        grid=(N_MT, ncol),
        in_specs=[
            pl.BlockSpec((TM, TN), lambda i, j: (i, j)),
            pl.BlockSpec((TM, ATTN_W), lambda i, j: (i, 0)),
            pl.BlockSpec((TM, SSM_W), lambda i, j: (i, 0)),
            pl.BlockSpec((TM, RET_W), lambda i, j: (i, 0)),
            _mod_spec(l, 5, TN, lambda j: j),
            _resident_weight_spec(D_MODEL, l, ncol),
        ],
        out_specs=pl.BlockSpec((TM, TN), lambda i, j: (i, j)),
        out_shape=jax.ShapeDtypeStruct((N_TOK, D_MODEL), F32),
        scratch_shapes=[pltpu.VMEM((ncol, D_MODEL, TN), BF16)],
        compiler_params=_cparams(("arbitrary", "arbitrary")),
        name="outproj",
    )(x, attn, ssm, ret, modt, w)


def _head_rms(x, g):
    ms = jnp.mean(x * x, axis=-1, keepdims=True)
    return x * lax.rsqrt(ms + EPS) * g


def _rope(x, cos, sin_signed):
    lane = lax.broadcasted_iota(jnp.int32, x.shape, 1)
    partner = jnp.where((lane % 64) < 32, pltpu.roll(x, 96, 1), pltpu.roll(x, 32, 1))
    return x * cos + partner * sin_signed


SOFTMAX_EXP2_SCALE = HEAD_DIM ** -0.5 * math.log2(math.e)


def _attend_short(qs, k, v):
    s = _dot_nt(qs.astype(BF16), k)
    p = jnp.exp2((s - jnp.max(s, axis=-1, keepdims=True)) * SOFTMAX_EXP2_SCALE)
    l = jnp.sum(p, axis=-1, keepdims=True)
    return _dot(p.astype(BF16), v) / l


def _attend(qs, k, vt):
    st = _dot_nt(k, qs.astype(BF16))
    p = jnp.exp2((st - jnp.max(st, axis=0, keepdims=True)) * SOFTMAX_EXP2_SCALE)
    l = jnp.sum(p, axis=0, keepdims=True)
    return (_dot(vt, p.astype(BF16)) / l).T


def _attn_kernel(q_ref, k_ref, v_ref, ck_ref, cv_ref, qg_ref, kg_ref, cos_ref, sin_ref,
                 o_ref, kc_ref, vc_ref, k_scr, vt_scr):
    i = pl.program_id(0)
    qg = qg_ref[...]
    kg = kg_ref[...]

    def heads(kv):
        return [slice((kv * Q_PER_KV + g) * HEAD_DIM, (kv * Q_PER_KV + g + 1) * HEAD_DIM) for g in range(Q_PER_KV)]

    @pl.when(i < N_PT)
    def _():
        def seq_body(sq, carry):
            rows = pl.ds(pl.multiple_of(sq * SEQ, SEQ), SEQ)
            v = v_ref[rows, :]
            vc_ref[sq] = v
            for kv in range(N_KV_HEADS):
                ksl = slice(kv * HEAD_DIM, (kv + 1) * HEAD_DIM)
                kn = _head_rms(k_ref[rows, ksl], kg)
                kc_ref[sq, :, ksl] = kn
                qs = jnp.concatenate([_head_rms(q_ref[rows, hs], qg) for hs in heads(kv)], axis=0)
                o = _attend_short(qs, kn.astype(BF16), v[:, ksl].astype(BF16))
                for g, hs in enumerate(heads(kv)):
                    o_ref[rows, hs] = o[g * SEQ:(g + 1) * SEQ].astype(BF16)
            return carry

        lax.fori_loop(0, SEQ_PER_TILE, seq_body, 0)

    @pl.when(i >= N_PT)
    def _():
        k_scr[0:PAST_LEN, :] = ck_ref[...].astype(BF16)
        for kv in range(N_KV_HEADS):
            ksl = slice(kv * HEAD_DIM, (kv + 1) * HEAD_DIM)
            vt_scr[kv, :, 0:PAST_LEN] = cv_ref[:, ksl].T.astype(BF16)
            vt_scr[kv, :, PAST_LEN:] = v_ref[:, ksl].T.astype(BF16)
            kn = _rope(_head_rms(k_ref[:, ksl], kg), cos_ref[...], sin_ref[...])
            k_scr[PAST_LEN:, ksl] = kn.astype(BF16)

        def q_body(qb, carry):
            rows = pl.ds(pl.multiple_of(qb * ATT_TQ, ATT_TQ), ATT_TQ)
            cos = cos_ref[rows, :]
            sin = sin_ref[rows, :]
            for kv in range(N_KV_HEADS):
                ksl = slice(kv * HEAD_DIM, (kv + 1) * HEAD_DIM)
                qs = jnp.concatenate([_rope(_head_rms(q_ref[rows, hs], qg), cos, sin) for hs in heads(kv)], axis=0)
                o = _attend(qs, k_scr[:, ksl], vt_scr[kv])
                for g, hs in enumerate(heads(kv)):
                    o_ref[rows, hs] = o[g * ATT_TQ:(g + 1) * ATT_TQ].astype(BF16)
            return carry

        lax.fori_loop(0, DEC_SEQ // ATT_TQ, q_body, 0)


def _attention(p, cache_k, cache_v, qg, kg, cos_t, sin_t, l):
    cache_spec = pl.BlockSpec((None, None, PAST_LEN, KV_W), lambda i: (_sample_idx(i), l, 0, 0))
    vec = pl.BlockSpec((None, 1, HEAD_DIM), lambda i: (l, 0, 0))
    tab = pl.BlockSpec((DEC_SEQ, HEAD_DIM), lambda i: (0, 0))
    new_cache = pl.BlockSpec((SEQ_PER_TILE, SEQ, KV_W), lambda i: (_prompt_idx(i), 0, 0))
    return pl.pallas_call(
        _attn_kernel,
        grid=(N_MT,),
        in_specs=[
            pl.BlockSpec((TM, ATTN_W), lambda i: (i, 0)),
            pl.BlockSpec((TM, KV_W), lambda i: (i, COL_K)),
            pl.BlockSpec((TM, KV_W), lambda i: (i, COL_V)),
            cache_spec, cache_spec, vec, vec, tab, tab,
        ],
        out_specs=[pl.BlockSpec((TM, ATTN_W), lambda i: (i, 0)), new_cache, new_cache],
        out_shape=[
            jax.ShapeDtypeStruct((N_TOK, ATTN_W), BF16),
            jax.ShapeDtypeStruct((BATCH, SEQ, KV_W), F32),
            jax.ShapeDtypeStruct((BATCH, SEQ, KV_W), F32),
        ],
        scratch_shapes=[pltpu.VMEM((ATT_S, KV_W), BF16), pltpu.VMEM((N_KV_HEADS, HEAD_DIM, ATT_S), BF16)],
        compiler_params=_cparams(("arbitrary",)),
        name="attention",
    )(p, p, p, cache_k.reshape(DEC_BATCH, DEPTH, PAST_LEN, KV_W), cache_v.reshape(DEC_BATCH, DEPTH, PAST_LEN, KV_W),
      qg.reshape(DEPTH, 1, HEAD_DIM), kg.reshape(DEPTH, 1, HEAD_DIM), cos_t, sin_t)


def _log_sigmoid(x):
    return -(jnp.maximum(-x, 0.0) + jnp.log(1.0 + jnp.exp(-jnp.abs(x))))


def _decay_mask(qi, kj, lg_f, lg_b):
    dd = qi - kj
    log2e = math.log2(math.e)
    w = jnp.exp2(dd * jnp.where(dd > 0, lg_f * log2e, -lg_b * log2e))
    return jnp.where(dd == 0, 2.0, w)


def _retention_kernel(q_ref, k_ref, v_ref, g_ref, dl_ref, cos_ref, sin_ref, s0_ref, o_ref, sf_ref,
                      acc, qr_scr, kr_scr, vb_scr):
    i = pl.program_id(0)
    scale = HEAD_DIM ** -0.5
    lg = [[_log_sigmoid(dl_ref[d, h][0:1, :]) for h in range(RET_HEADS)] for d in range(2)]
    lg1 = [[lg[d][h][:, 0:1] for h in range(RET_HEADS)] for d in range(2)]
    hsl = [slice(h * HEAD_DIM, (h + 1) * HEAD_DIM) for h in range(RET_HEADS)]

    @pl.when(i < N_PT)
    def _():
        qi = lax.broadcasted_iota(jnp.int32, (SEQ, SEQ), 0).astype(F32)
        kj = lax.broadcasted_iota(jnp.int32, (SEQ, SEQ), 1).astype(F32)
        pos = lax.broadcasted_iota(jnp.int32, (SEQ, HEAD_DIM), 0).astype(F32)
        masks = [_decay_mask(qi, kj, lg1[0][h], lg1[1][h]) for h in range(RET_HEADS)]
        kdec_f = [jnp.exp((SEQ - 1.0 - pos) * lg[0][h]) for h in range(RET_HEADS)]
        kdec_b = [jnp.exp(pos * lg[1][h]) for h in range(RET_HEADS)]

        def seq_body(sq, carry):
            rows = pl.ds(pl.multiple_of(sq * SEQ, SEQ), SEQ)
            for h in range(RET_HEADS):
                q = q_ref[rows, hsl[h]].astype(BF16)
                k = k_ref[rows, hsl[h]] * scale
                v = v_ref[rows, hsl[h]].astype(BF16)
                a = (_dot_nt(q, k.astype(BF16)) * masks[h]).astype(BF16)
                acc[rows, hsl[h]] = _dot(a, v)
                sf_ref[sq, 0, h] = _dot((k * kdec_f[h]).T.astype(BF16), v)
                sf_ref[sq, 1, h] = _dot((k * kdec_b[h]).T.astype(BF16), v)
            return carry

        lax.fori_loop(0, SEQ_PER_TILE, seq_body, 0)

    @pl.when(i >= N_PT)
    def _():
        for h in range(RET_HEADS):
            qr_scr[:, hsl[h]] = _rope(q_ref[:, hsl[h]], cos_ref[...], sin_ref[...]).astype(BF16)
            kr_scr[:, hsl[h]] = _rope(k_ref[:, hsl[h]] * scale, cos_ref[...], sin_ref[...]).astype(BF16)
        vb_scr[...] = v_ref[...].astype(BF16)

        def q_body(qb, carry):
            row0 = qb * ATT_TQ
            rows = pl.ds(pl.multiple_of(row0, ATT_TQ), ATT_TQ)
            qi = (row0 + lax.broadcasted_iota(jnp.int32, (ATT_TQ, DEC_SEQ), 0)).astype(F32)
            kj = lax.broadcasted_iota(jnp.int32, (ATT_TQ, DEC_SEQ), 1).astype(F32)
            pos = (row0 + lax.broadcasted_iota(jnp.int32, (ATT_TQ, HEAD_DIM), 0)).astype(F32)
            for h in range(RET_HEADS):
                q = qr_scr[rows, hsl[h]]
                a = (_dot_nt(q, kr_scr[:, hsl[h]]) * _decay_mask(qi, kj, lg1[0][h], lg1[1][h])).astype(BF16)
                o = _dot(a, vb_scr[:, hsl[h]])
                o += _dot(q, s0_ref[0, h].astype(BF16)) * jnp.exp((pos + 1.0) * lg[0][h])
                o += _dot(q, s0_ref[1, h].astype(BF16)) * jnp.exp((DEC_SEQ - pos) * lg[1][h])
                acc[rows, hsl[h]] = o
            return carry

        lax.fori_loop(0, DEC_SEQ // ATT_TQ, q_body, 0)

    for h in range(RET_HEADS):
        o = acc[:, hsl[h]]
        o = o - jnp.mean(o, axis=-1, keepdims=True)
        o = o * lax.rsqrt(jnp.mean(o * o, axis=-1, keepdims=True) + EPS)
        o_ref[:, hsl[h]] = (o * _silu(g_ref[:, hsl[h]])).astype(BF16)


def _retention(p, dl, cos_t, sin_t, state_ret, l):
    tab = pl.BlockSpec((DEC_SEQ, HEAD_DIM), lambda i: (0, 0))
    st = (2, RET_HEADS, HEAD_DIM, HEAD_DIM)
    return pl.pallas_call(
        _retention_kernel,
        grid=(N_MT,),
        in_specs=[pl.BlockSpec((TM, RET_W), lambda i, k=k: (i, COL_R + k)) for k in range(4)] + [
            pl.BlockSpec((None, 2, RET_HEADS, 8, HEAD_DIM), lambda i: (l, 0, 0, 0, 0)),
            tab, tab,
            pl.BlockSpec((None, None) + st, lambda i: (_sample_idx(i), l, 0, 0, 0, 0)),
        ],
        out_specs=[
            pl.BlockSpec((TM, RET_W), lambda i: (i, 0)),
            pl.BlockSpec((SEQ_PER_TILE,) + st, lambda i: (_prompt_idx(i), 0, 0, 0, 0)),
        ],
        out_shape=[
            jax.ShapeDtypeStruct((N_TOK, RET_W), BF16),
            jax.ShapeDtypeStruct((BATCH,) + st, F32),
        ],
        scratch_shapes=[pltpu.VMEM((TM, RET_W), F32)] + [pltpu.VMEM((TM, RET_W), BF16)] * 3,
        compiler_params=_cparams(("arbitrary",)),
        name="retention",
    )(p, p, p, p, dl, cos_t, sin_t, state_ret)


def _s5_disc_kernel(are_ref, aim_ref, ldt_ref, bre_ref, bim_ref, cim_ref,
                    lre_ref, lim_ref, bbre_ref, bbim_ref, ncim_ref):
    ar = are_ref[...]
    ai = aim_ref[...]
    dt = jnp.exp(ldt_ref[...])
    mag = jnp.exp(ar * dt)
    lr = mag * jnp.cos(ai * dt)
    li = mag * jnp.sin(ai * dt)
    den = ar * ar + ai * ai
    nr = lr - 1.0
    cr = (nr * ar + li * ai) / den
    ci = (li * ar - nr * ai) / den
    br = bre_ref[...]
    bi = bim_ref[...]
    lre_ref[...] = lr
    lim_ref[...] = li
    bbre_ref[...] = cr * br - ci * bi
    bbim_ref[...] = cr * bi + ci * br
    ncim_ref[...] = -cim_ref[...]


def _s5_discretize(a_re, a_im, log_dt, b_re, b_im, c_im):
    rows = DEPTH * 2 * SSM_GROUPS
    cols = SSM_STATE * SSM_GROUP
    shp = (DEPTH, 2, SSM_GROUPS, SSM_STATE, SSM_GROUP)
    args = [
        jnp.broadcast_to(a_re[..., None], shp).reshape(rows, cols),
        jnp.broadcast_to(a_im[..., None], shp).reshape(rows, cols),
        jnp.broadcast_to(log_dt[..., None, None], shp).reshape(rows, cols),
        b_re.reshape(rows, cols), b_im.reshape(rows, cols), c_im.reshape(rows, cols),
    ]
    spec = pl.BlockSpec((rows, cols), lambda: (0, 0))
    outs = pl.pallas_call(
        _s5_disc_kernel,
        in_specs=[spec] * 6,
        out_specs=[spec] * 5,
        out_shape=[jax.ShapeDtypeStruct((rows, cols), F32)] * 5,
        name="s5_discretize",
    )(*args)
    lre, lim, bbre, bbim, ncim = outs
    lre = lre.reshape(shp)[..., 0]
    lim = lim.reshape(shp)[..., 0]
    return (lre, lim, bbre.reshape(shp), bbim.reshape(shp),
            ncim.reshape(DEPTH, 2, SSM_GROUPS, SSM_GROUP, SSM_STATE))


def _state_cols(x):
    lead = x.shape[:-3]
    x = x.reshape(lead + (N_STILE, 2, SSM_STATE, 2))
    x = jnp.moveaxis(x, -1, -3)
    return x.reshape(lead + (STATE_W,))


def _state_uncols(x):
    lead = x.shape[:-1]
    x = x.reshape(lead + (N_STILE, 2, 2, SSM_STATE))
    x = jnp.moveaxis(x, -3, -1)
    return x.reshape(lead + (SSM_GROUPS, SSM_STATE, 2))


def _s5_compact(lre, lim, bbre, bbim, c_re, ncim):
    n = DEPTH * 2
    bb = jnp.stack([bbre, bbim], axis=-1)
    bb = bb.reshape(n, N_STILE, 2, SSM_STATE, SSM_GROUP, 2)
    wb = jnp.transpose(bb, (0, 4, 1, 5, 2, 3)).reshape(DEPTH, 2, SSM_GROUP, STATE_W)
    cc = jnp.stack([c_re, ncim], axis=-1)
    cc = cc.reshape(n, N_STILE, 2, SSM_GROUP, SSM_STATE, 2)
    wc = jnp.transpose(cc, (0, 3, 1, 5, 2, 4)).reshape(DEPTH, 2, SSM_GROUP, STATE_W)
    lam = jnp.concatenate([lre.reshape(DEPTH, 2, N_STILE, 128), lim.reshape(DEPTH, 2, N_STILE, 128)], axis=2)
    return wb, wc, lam


def _group_mask(cols):
    row_g = lax.broadcasted_iota(jnp.int32, (256, cols), 0) // SSM_GROUP
    col = lax.broadcasted_iota(jnp.int32, (256, cols), 1)
    col_g = ((col // 256) % 8) * 2 + (col % 128) // SSM_STATE
    return row_g == col_g


def _s5_scan_kernel(u_ref, wb_ref, wc_ref, lam_ref, h0_ref, y_ref, hf_ref,
                    pm_scr, bm_scr, cm_scr, hbuf, y_scr, state, ends, *, segments):
    npass = 1 if segments == 1 else 2
    d = pl.program_id(0)
    ps = pl.program_id(1)
    ck = pl.program_id(2)
    nck = pl.num_programs(2)
    rows = SSM_TC * SSM_ROWS
    half = STATE_W // 2
    lam = lam_ref[...]

    @pl.when(jnp.logical_and(ps == 0, ck == 0))
    def _():
        a = lax.broadcasted_iota(jnp.int32, (rows, rows), 0)
        b = lax.broadcasted_iota(jnp.int32, (rows, rows), 1)
        same = jnp.logical_and(a // SSM_ROWS == b % SSM_TC, a % SSM_ROWS == b // SSM_TC)
        pm_scr[...] = jnp.where(same, 1.0, 0.0).astype(BF16)
        wb = jnp.tile(wb_ref[...], (256 // SSM_GROUP, 1))
        bm_scr[...] = jnp.where(_group_mask(STATE_W), wb, 0.0).astype(BF16)
        for n_ in range(2):
            wc = jnp.tile(wc_ref[:, n_ * half:(n_ + 1) * half], (256 // SSM_GROUP, 1))
            cm_scr[n_] = jnp.where(_group_mask(half), wc, 0.0).astype(BF16)

    @pl.when(ck == 0)
    def _():
        if npass == 1:
            state[...] = h0_ref[0]
        else:
            @pl.when(ps == 0)
            def _():
                state[...] = jnp.zeros_like(state)

            @pl.when(ps == 1)
            def _():
                ends[...] = state[...]
                state[...] = h0_ref[0]
                lr = lam[0:N_STILE]
                li = lam[N_STILE:]
                for _ in range(int(math.log2(SSM_SEG_LEN))):
                    lr, li = lr * lr - li * li, 2.0 * lr * li

                def carry(order, prev):
                    for sg in order:
                        for sq in range(SSM_ROWS // segments):
                            r = sq * segments + sg
                            q = r + prev
                            for j in range(N_STILE):
                                re = slice(j * 256, j * 256 + 128)
                                im = slice(j * 256 + 128, (j + 1) * 256)
                                pr = state[q:q + 1, re]
                                pi = state[q:q + 1, im]
                                ar = lr[j:j + 1]
                                ai = li[j:j + 1]
                                state[r:r + 1, re] = ends[q:q + 1, re] + ar * pr - ai * pi
                                state[r:r + 1, im] = ends[q:q + 1, im] + ar * pi + ai * pr

                @pl.when(d == 0)
                def _():
                    carry(range(1, segments), -1)

                @pl.when(d == 1)
                def _():
                    carry(range(segments - 2, -1, -1), 1)

    u = _dot(pm_scr[...], u_ref[...].reshape(rows, SSM_W).astype(BF16)).astype(BF16)
    for j in range(N_STILE):
        k0 = (j // 8) * 256
        hbuf[:, j * 256:(j + 1) * 256] = _dot(u[:, k0:k0 + 256], bm_scr[:, j * 256:(j + 1) * 256])

    JT = 4
    for jb in range(N_STILE // JT):
        tiles = list(range(jb * JT, (jb + 1) * JT))
        lrs = [jnp.broadcast_to(lam[j:j + 1], (SSM_ROWS, 128)) for j in tiles]
        lis = [jnp.broadcast_to(lam[N_STILE + j:N_STILE + j + 1], (SSM_ROWS, 128)) for j in tiles]
        init = tuple(state[:, j * 256:j * 256 + 128] for j in tiles) + \
            tuple(state[:, j * 256 + 128:(j + 1) * 256] for j in tiles)

        def body(t, carry_, tiles=tiles, lrs=lrs, lis=lis):
            tt = t + d * (SSM_TC - 1 - 2 * t)
            r = pl.ds(pl.multiple_of(tt * SSM_ROWS, SSM_ROWS), SSM_ROWS)
            out_r, out_i = [], []
            for n_, j in enumerate(tiles):
                hr, hi = carry_[n_], carry_[JT + n_]
                re = slice(j * 256, j * 256 + 128)
                im = slice(j * 256 + 128, (j + 1) * 256)
                nr = lrs[n_] * hr - lis[n_] * hi + hbuf[r, re]
                ni = lrs[n_] * hi + lis[n_] * hr + hbuf[r, im]
                hbuf[r, re] = nr
                hbuf[r, im] = ni
                out_r.append(nr)
                out_i.append(ni)
            return tuple(out_r) + tuple(out_i)

        fin = lax.fori_loop(0, SSM_TC, body, init, unroll=2)
        for n_, j in enumerate(tiles):
            state[:, j * 256:j * 256 + 128] = fin[n_]
            state[:, j * 256 + 128:(j + 1) * 256] = fin[JT + n_]

    @pl.when(ps == npass - 1)
    def _():
        hb = hbuf[...].astype(BF16)
        for n_ in range(2):
            y = _dot_nt(hb[:, n_ * half:(n_ + 1) * half], cm_scr[n_])
            y_scr[2 * n_] = y[:, :128]
            y_scr[2 * n_ + 1] = y[:, 128:]
        for r in range(SSM_ROWS):
            for c_ in range(SSM_W // 128):
                y_ref[r, :, c_ * 128:(c_ + 1) * 128] = y_scr[c_, pl.ds(r, SSM_TC, stride=SSM_ROWS), :]

    @pl.when(jnp.logical_and(ps == npass - 1, ck == nck - 1))
    def _():
        hf_ref[0] = state[...]


def _s5_scan(p3, row_block, wb, wc, lam, h0, l, *, segments):
    steps = p3.shape[1]
    nck = steps // SSM_TC
    npass = 1 if segments == 1 else 2

    def chunk(d, c):
        return c + d * (nck - 1 - 2 * c)

    def y_chunk(d, p, c):
        return jnp.where(p == npass - 1, chunk(d, c), chunk(d, 0))

    par = pl.BlockSpec((None, None, SSM_GROUP, STATE_W), lambda d, p, c: (l, d, 0, 0))
    st = pl.BlockSpec((1, SSM_ROWS, STATE_W), lambda d, p, c: (d, 0, 0))
    return pl.pallas_call(
        functools.partial(_s5_scan_kernel, segments=segments),
        grid=(2, npass, nck),
        in_specs=[
            pl.BlockSpec((SSM_ROWS, SSM_TC, SSM_W), lambda d, p, c: (row_block, chunk(d, c), COL_U)),
            par, par,
            pl.BlockSpec((None, None, 2 * N_STILE, 128), lambda d, p, c: (l, d, 0, 0)),
            st,
        ],
        out_specs=[
            pl.BlockSpec((None, SSM_ROWS, SSM_TC, SSM_W), lambda d, p, c: (d, 0, y_chunk(d, p, c), 0)),
            st,
        ],
        out_shape=[
            jax.ShapeDtypeStruct((2, SSM_ROWS, steps, SSM_W), F32),
            jax.ShapeDtypeStruct((2, SSM_ROWS, STATE_W), F32),
        ],
        scratch_shapes=[
            pltpu.VMEM((SSM_TC * SSM_ROWS, SSM_TC * SSM_ROWS), BF16),
            pltpu.VMEM((256, STATE_W), BF16),
            pltpu.VMEM((2, 256, STATE_W // 2), BF16),
            pltpu.VMEM((SSM_TC * SSM_ROWS, STATE_W), F32),
            pltpu.VMEM((SSM_W // 128, SSM_TC * SSM_ROWS, 128), F32),
            pltpu.VMEM((SSM_ROWS, STATE_W), F32),
            pltpu.VMEM((SSM_ROWS, STATE_W), F32),
        ],
        compiler_params=_cparams(("arbitrary", "arbitrary", "arbitrary")),
        name="s5_scan_seg%d" % segments,
    )(p3, wb, wc, lam, h0)


def _s5_glu_kernel(ypf_ref, ypb_ref, ysf_ref, ysb_ref, u_ref, d_ref, w_ref, o_ref):
    i = pl.program_id(0)
    w = w_ref[...].astype(BF16)

    def glu(y):
        y = y + d_ref[...] * u_ref[...]
        z = _dot(jax.nn.gelu(y).astype(BF16), w)
        o_ref[...] = (z[:, :SSM_W] * _sigmoid(z[:, SSM_W:])).astype(BF16)

    @pl.when(i < N_PT)
    def _():
        glu(ypf_ref[...] + ypb_ref[...])

    @pl.when(i >= N_PT)
    def _():
        glu(ysf_ref[...] + ysb_ref[...])


def _s5_glu(yp, ys, p, d, w, l):
    return pl.pallas_call(
        _s5_glu_kernel,
        grid=(N_MT,),
        in_specs=[
            pl.BlockSpec((None, TM, SSM_W), lambda i: (0, _prompt_idx(i), 0)),
            pl.BlockSpec((None, TM, SSM_W), lambda i: (1, _prompt_idx(i), 0)),
            pl.BlockSpec((None, TM, SSM_W), lambda i: (0, _sample_idx(i), 0)),
            pl.BlockSpec((None, TM, SSM_W), lambda i: (1, _sample_idx(i), 0)),
            pl.BlockSpec((TM, SSM_W), lambda i: (i, COL_U)),
            pl.BlockSpec((None, 1, SSM_W), lambda i: (l, 0, 0)),
            pl.BlockSpec((None, SSM_W, 2 * SSM_W), lambda i: (l, 0, 0)),
        ],
        out_specs=pl.BlockSpec((TM, SSM_W), lambda i: (i, 0)),
        out_shape=jax.ShapeDtypeStruct((N_TOK, SSM_W), BF16),
        compiler_params=_cparams(("arbitrary",)),
        name="s5_glu",
    )(yp, yp, ys, ys, p, d.reshape(DEPTH, 1, SSM_W), w)


def _s5_layer(p, s5p, st_l, ssm_d, w_ssm_glu, l):
    wb, wc, lam = s5p
    yp, hfin = _s5_scan(p.reshape(N_TOK // SEQ, SEQ, IN_W), 0, wb, wc, lam,
                        jnp.zeros((2, SSM_ROWS, STATE_W), F32), l, segments=1)
    h0 = _state_cols(jnp.moveaxis(st_l, 0, 1))
    h0 = jnp.stack([
        jnp.zeros((DEC_BATCH, SSM_SEGS, STATE_W), F32).at[:, 0].set(h0[0]),
        jnp.zeros((DEC_BATCH, SSM_SEGS, STATE_W), F32).at[:, SSM_SEGS - 1].set(h0[1]),
    ]).reshape(2, SSM_ROWS, STATE_W)
    ys, _ = _s5_scan(p.reshape(N_TOK // SSM_SEG_LEN, SSM_SEG_LEN, IN_W), N_PROMPT // SSM_SEG_LEN // SSM_ROWS,
                     wb, wc, lam, h0, l, segments=SSM_SEGS)
    out = _s5_glu(yp.reshape(2, N_PROMPT, SSM_W), ys.reshape(2, N_SAMPLE, SSM_W), p, ssm_d, w_ssm_glu, l)
    new_state = jnp.moveaxis(_state_uncols(hfin), 0, 1)
    return out, new_state


def _rope_tables():
    rows = DEC_SEQ // GRID_W
    r = jnp.repeat(jnp.arange(rows), GRID_W).astype(F32)
    col = jnp.tile(jnp.arange(GRID_W), rows).astype(F32)
    inv = ROPE_BASE ** (-jnp.arange(ROPE_FREQS, dtype=F32) / ROPE_FREQS)
    ang = jnp.stack([r, col], axis=-1)[:, :, None] * inv
    cos = jnp.cos(ang)
    sin = jnp.sin(ang)
    cos_t = jnp.concatenate([cos, cos], axis=-1).reshape(DEC_SEQ, HEAD_DIM)
    sin_t = jnp.concatenate([-sin, sin], axis=-1).reshape(DEC_SEQ, HEAD_DIM)
    return cos_t, sin_t


def kernel(x_prompt, x_sample, cache_k, cache_v, state_ssm, state_ret, c, c_ctx, w_mod, b_mod, norm_g,
           w_ffn_in, w_ffn_out, w_in, w_out, q_norm_g, k_norm_g, ssm_a_re, ssm_a_im, ssm_log_dt,
           ssm_b_re, ssm_b_im, ssm_c_re, ssm_c_im, ssm_d, w_ssm_glu, ret_decay_logit, final_norm_g):
    x = jnp.concatenate([x_prompt.reshape(N_PROMPT, D_MODEL), x_sample.reshape(N_SAMPLE, D_MODEL)], axis=0)
    cond8 = jnp.concatenate([c_ctx[None], c, jnp.zeros((5, D_MODEL), F32)], axis=0)
    mods = _adaln(cond8, w_mod, b_mod).reshape(DEPTH, 8, N_SUB * 3, D_MODEL)
    modt = jnp.transpose(mods[:, jnp.array(TILE_MOD, jnp.int32)], (0, 2, 1, 3))[:, :, :, None, :]
    gains = norm_g.reshape(DEPTH * N_SUB, 1, D_MODEL)
    cos_t, sin_t = _rope_tables()
    lre, lim, bbre, bbim, ncim = _s5_discretize(ssm_a_re, ssm_a_im, ssm_log_dt, ssm_b_re, ssm_b_im, ssm_c_im)
    s5p = _s5_compact(lre, lim, bbre, bbim, ssm_c_re, ncim)
    dl = jnp.broadcast_to(ret_decay_logit[:, :, :, None, None], (DEPTH, 2, RET_HEADS, 8, HEAD_DIM))
    ks_, vs_, hs_, ss_ = [], [], [], []
    for l in range(DEPTH):
        x = _ffn(x, gains, modt, w_ffn_in, w_ffn_out, l, 0)
        p = _inproj(x, gains, modt, w_in, l)
        attn, k_l, v_l = _attention(p, cache_k, cache_v, q_norm_g, k_norm_g, cos_t, sin_t, l)
        ssm, h_l = _s5_layer(p, s5p, state_ssm[:, l], ssm_d, w_ssm_glu, l)
        ret, s_l = _retention(p, dl, cos_t, sin_t, state_ret, l)
        x = _outproj(x, attn, ssm, ret, modt, w_out, l)
        x = _ffn(x, gains, modt, w_ffn_in, w_ffn_out, l, 1, final_g=final_norm_g if l == DEPTH - 1 else None)
        ks_.append(k_l.reshape(BATCH, SEQ, N_KV_HEADS, HEAD_DIM))
        vs_.append(v_l.reshape(BATCH, SEQ, N_KV_HEADS, HEAD_DIM))
        hs_.append(h_l)
        ss_.append(s_l)
    y_prompt = x[:N_PROMPT].reshape(BATCH, SEQ, D_MODEL)
    y_sample = x[N_PROMPT:].reshape(DEC_BATCH, DEC_SEQ, D_MODEL)
    return (y_prompt, y_sample, jnp.stack(ks_, axis=1), jnp.stack(vs_, axis=1),
            jnp.stack(hs_, axis=1), jnp.stack(ss_, axis=1))
```

```python
import functools
import math

import jax
import jax.numpy as jnp
from jax import lax
from jax.experimental import pallas as pl
from jax.experimental.pallas import tpu as pltpu

D_MODEL = 2048
BATCH = 16
SEQ = 256
DEPTH = 2
DEC_BATCH = 2
DEC_SEQ = 1024
PAST_LEN = 512
GRID_W = 64
HEAD_DIM = 128
N_Q_HEADS = 8
N_KV_HEADS = 2
Q_PER_KV = 4
ATTN_W = 1024
KV_W = 256
SSM_W = 512
SSM_GROUP = 16
SSM_GROUPS = 32
SSM_STATE = 64
RET_HEADS = 4
RET_W = 512
IN_W = 4096
D_FF = 5632
N_SUB = 3
RET_CHUNK = 128
ROPE_BASE = 10000.0
ROPE_FREQS = 32
EPS = 1e-6

N_PROMPT = BATCH * SEQ
N_SAMPLE = DEC_BATCH * DEC_SEQ
N_TOK = N_PROMPT + N_SAMPLE
TM = 1024
N_MT = N_TOK // TM
N_PT = N_PROMPT // TM
SEQ_PER_TILE = TM // SEQ
TILE_MOD = (0, 0, 0, 0, 1, 2)
ROW_CHUNK = 128
TF = 256
FFN_NSPLIT = 4
TN = 512
STATE_W = 2 * SSM_GROUPS * SSM_STATE
N_STILE = STATE_W // 256
SSM_ROWS = 16
SSM_SEGS = 8
SSM_SEG_LEN = DEC_SEQ // SSM_SEGS
SSM_TC = 32
ATT_TQ = 256
ATT_S = PAST_LEN + DEC_SEQ
VMEM_LIMIT = 56 * 1024 * 1024
VMEM_LIMIT_FFN = 60 * 1024 * 1024

COL_K = ATTN_W // KV_W
COL_V = COL_K + 1
COL_U = (ATTN_W + 2 * KV_W) // SSM_W
COL_R = (ATTN_W + 2 * KV_W + SSM_W) // RET_W

BF16 = jnp.bfloat16
F32 = jnp.float32


def _cparams(sem, limit=VMEM_LIMIT):
    return pltpu.CompilerParams(dimension_semantics=sem, vmem_limit_bytes=limit)


def _dot(a, b):
    return jnp.dot(a, b, preferred_element_type=F32)


def _dot_nt(a, b):
    return lax.dot_general(a, b, (((1,), (1,)), ((), ())), preferred_element_type=F32)


def _sigmoid(x):
    return 1.0 / (1.0 + jnp.exp(-x))


def _silu(x):
    return x * _sigmoid(x)


def _sample_idx(i):
    return jnp.clip(i - N_PT, 0, DEC_BATCH - 1)


def _prompt_idx(i):
    return jnp.minimum(i, N_PT - 1)


def _row_chunks(n_rows, body):
    def step(c, carry):
        body(pl.ds(pl.multiple_of(c * ROW_CHUNK, ROW_CHUNK), ROW_CHUNK))
        return carry

    lax.fori_loop(0, n_rows // ROW_CHUNK, step, 0)


def _mod_spec(l, k, width=D_MODEL, col=lambda j: 0):
    return pl.BlockSpec((None, None, 1, 1, width), lambda i, j: (l, k, i, 0, col(j)))


def _gain_spec(l, sub):
    return pl.BlockSpec((None, 1, D_MODEL), lambda i, j: (l * N_SUB + sub, 0, 0))


def _adaln_kernel(c_ref, w_ref, b_ref, o_ref):
    a = _silu(c_ref[...]).astype(BF16)
    o_ref[0] = _dot(a, w_ref[0].astype(BF16)) + b_ref[0]


def _adaln(cond8, w_mod, b_mod):
    n = w_mod.shape[-1]
    tn = 1024
    return pl.pallas_call(
        _adaln_kernel,
        grid=(DEPTH, n // tn),
        in_specs=[
            pl.BlockSpec((8, D_MODEL), lambda l, j: (0, 0)),
            pl.BlockSpec((1, D_MODEL, tn), lambda l, j: (l, 0, j)),
            pl.BlockSpec((1, 1, tn), lambda l, j: (l, 0, j)),
        ],
        out_specs=pl.BlockSpec((1, 8, tn), lambda l, j: (l, 0, j)),
        out_shape=jax.ShapeDtypeStruct((DEPTH, 8, n), F32),
        compiler_params=_cparams(("arbitrary", "arbitrary")),
        name="adaln",
    )(cond8, w_mod, b_mod.reshape(DEPTH, 1, n))


def _norm_mod(x, g, shift, scale):
    ms = jnp.mean(x * x, axis=-1, keepdims=True)
    y = x * lax.rsqrt(ms + EPS) * g
    return y * (1.0 + scale) + shift


def _ffn_kernel(*refs, n_x, n_out, final):
    refs = list(refs)
    x_refs = refs[:n_x]
    g_ref, sh_ref, sc_ref, gt_ref, wa_ref, wu_ref, wo_ref = refs[n_x:n_x + 7]
    rest = refs[n_x + 7:]
    fg_ref = rest.pop(0) if final else None
    o_refs, h_scr = rest[:n_out], rest[n_out]
    i = pl.program_id(0)
    j = pl.program_id(1)

    def run(x_ref, o_ref):
        @pl.when(j == 0)
        def _():
            def pre(rows):
                h = _norm_mod(x_ref[rows, :], g_ref[...], sh_ref[0], sc_ref[0])
                h_scr[rows, :] = h.astype(BF16)
                o_ref[rows, :] = jnp.zeros((ROW_CHUNK, D_MODEL), F32)

            _row_chunks(TM, pre)

        h = h_scr[...]
        a = _dot(h, wa_ref[...].astype(BF16))
        u = _dot(h, wu_ref[...].astype(BF16))
        mid = (_silu(a) * u).astype(BF16)
        wn = D_MODEL // FFN_NSPLIT
        for n in range(FFN_NSPLIT):
            cs = slice(n * wn, (n + 1) * wn)
            o_ref[:, cs] += _dot(mid, wo_ref[:, cs].astype(BF16))

        @pl.when(j == pl.num_programs(1) - 1)
        def _():
            def post(rows):
                out = x_ref[rows, :] + (0.5 * gt_ref[0]) * o_ref[rows, :]
                if final:
                    ms = jnp.mean(out * out, axis=-1, keepdims=True)
                    out = out * lax.rsqrt(ms + EPS) * fg_ref[...]
                o_ref[rows, :] = out

            _row_chunks(TM, post)

    if n_x == 1 and n_out == 1:
        run(x_refs[0], o_refs[0])
    else:
        @pl.when(i < N_PT)
        def _():
            run(x_refs[0], o_refs[0])

        @pl.when(i >= N_PT)
        def _():
            run(x_refs[-1], o_refs[-1])


def _split_tok_specs():
    one = pl.Buffered(1)
    return [pl.BlockSpec((TM, D_MODEL), lambda i, j: (_prompt_idx(i), 0), pipeline_mode=one),
            pl.BlockSpec((TM, D_MODEL), lambda i, j: (_sample_idx(i), 0), pipeline_mode=one)]


def _ffn(x, gains, modt, w_in, w_out, l, f, final_g=None):
    final = final_g is not None
    xs = list(x) if isinstance(x, (tuple, list)) else [x]
    sub = 2 * f
    nf = D_FF // TF
    tok = pl.BlockSpec((TM, D_MODEL), lambda i, j: (i, 0))
    in_specs = (_split_tok_specs() if len(xs) == 2 else [tok]) + [
        _gain_spec(l, sub), _mod_spec(l, 3 * sub), _mod_spec(l, 3 * sub + 1), _mod_spec(l, 3 * sub + 2),
        pl.BlockSpec((None, None, D_MODEL, TF), lambda i, j: (l, f, 0, j)),
        pl.BlockSpec((None, None, D_MODEL, TF), lambda i, j: (l, f, 0, j + nf)),
        pl.BlockSpec((None, None, TF, D_MODEL), lambda i, j: (l, f, j, 0)),
    ]
    args = xs + [gains, modt, modt, modt, w_in, w_in, w_out]
    if final:
        in_specs.append(pl.BlockSpec((1, D_MODEL), lambda i, j: (0, 0)))
        args.append(final_g.reshape(1, D_MODEL))
        out_specs = _split_tok_specs()
        out_shape = [jax.ShapeDtypeStruct((N_PROMPT, D_MODEL), F32), jax.ShapeDtypeStruct((N_SAMPLE, D_MODEL), F32)]
    else:
        out_specs = tok
        out_shape = jax.ShapeDtypeStruct((N_TOK, D_MODEL), F32)
    return pl.pallas_call(
        functools.partial(_ffn_kernel, n_x=len(xs), n_out=2 if final else 1, final=final),
        grid=(N_MT, nf),
        in_specs=in_specs,
        out_specs=out_specs,
        out_shape=out_shape,
        scratch_shapes=[pltpu.VMEM((TM, D_MODEL), BF16)],
        compiler_params=_cparams(("arbitrary", "arbitrary"), VMEM_LIMIT_FFN),
        name="ffn_final" if final else ("ffn_first" if len(xs) == 2 else "ffn"),
    )(*args)


def _resident_weight_spec(rows, l, ncol):
    return pl.BlockSpec((None, rows, TN), lambda i, j: (l, 0, jnp.where(i == 0, j, ncol - 1)))


def _inproj_kernel(x_ref, g_ref, sh_ref, sc_ref, w_ref, o_ref, h_scr, w_scr):
    i = pl.program_id(0)
    j = pl.program_id(1)

    @pl.when(j == 0)
    def _():
        def pre(rows):
            h = _norm_mod(x_ref[rows, :], g_ref[...], sh_ref[0], sc_ref[0])
            h_scr[rows, :] = h.astype(BF16)

        _row_chunks(TM, pre)

    @pl.when(i == 0)
    def _():
        w_scr[j] = w_ref[...].astype(BF16)

    o_ref[...] = _dot(h_scr[...], w_scr[j])


def _inproj(x, gains, modt, w, l):
    ncol = IN_W // TN
    return pl.pallas_call(
        _inproj_kernel,
        grid=(N_MT, ncol),
        in_specs=[
            pl.BlockSpec((TM, D_MODEL), lambda i, j: (i, 0)),
            _gain_spec(l, 1), _mod_spec(l, 3), _mod_spec(l, 4),
            _resident_weight_spec(D_MODEL, l, ncol),
        ],
        out_specs=pl.BlockSpec((TM, TN), lambda i, j: (i, j)),
        out_shape=jax.ShapeDtypeStruct((N_TOK, IN_W), F32),
        scratch_shapes=[pltpu.VMEM((TM, D_MODEL), BF16), pltpu.VMEM((ncol, D_MODEL, TN), BF16)],
        compiler_params=_cparams(("arbitrary", "arbitrary")),
        name="inproj",
    )(x, gains, modt, modt, w)


def _outproj_kernel(x_ref, a_ref, s_ref, r_ref, gt_ref, w_ref, o_ref, w_scr):
    i = pl.program_id(0)
    j = pl.program_id(1)

    @pl.when(i == 0)
    def _():
        w_scr[j] = w_ref[...].astype(BF16)

    y = _dot(a_ref[...], w_scr[j, 0:ATTN_W, :])
    y += _dot(s_ref[...], w_scr[j, ATTN_W:ATTN_W + SSM_W, :])
    y += _dot(r_ref[...], w_scr[j, ATTN_W + SSM_W:, :])
    o_ref[...] = x_ref[...] + gt_ref[0] * y


def _outproj(x, attn, ssm, ret, modt, w, l):
    ncol = D_MODEL // TN
    return pl.pallas_call(
        _outproj_kernel,
        grid=(N_MT, ncol),
        in_specs=[
            pl.BlockSpec((TM, TN), lambda i, j: (i, j)),
            pl.BlockSpec((TM, ATTN_W), lambda i, j: (i, 0)),
            pl.BlockSpec((TM, SSM_W), lambda i, j: (i, 0)),
            pl.BlockSpec((TM, RET_W), lambda i, j: (i, 0)),
            _mod_spec(l, 5, TN, lambda j: j),
            _resident_weight_spec(D_MODEL, l, ncol),
        ],
        out_specs=pl.BlockSpec((TM, TN), lambda i, j: (i, j)),
        out_shape=jax.ShapeDtypeStruct((N_TOK, D_MODEL), F32),
        scratch_shapes=[pltpu.VMEM((ncol, D_MODEL, TN), BF16)],
        compiler_params=_cparams(("arbitrary", "arbitrary")),
        name="outproj",
    )(x, attn, ssm, ret, modt, w)


def _head_rms(x, g):
    ms = jnp.mean(x * x, axis=-1, keepdims=True)
    return x * lax.rsqrt(ms + EPS) * g


def _rope(x, cos, sin_signed):
    lane = lax.broadcasted_iota(jnp.int32, x.shape, 1)
    partner = jnp.where((lane % 64) < 32, pltpu.roll(x, 96, 1), pltpu.roll(x, 32, 1))
    return x * cos + partner * sin_signed


SOFTMAX_EXP2_SCALE = HEAD_DIM ** -0.5 * math.log2(math.e)


def _attend_short(qs, k, v):
    s = _dot_nt(qs.astype(BF16), k)
    p = jnp.exp2((s - jnp.max(s, axis=-1, keepdims=True)) * SOFTMAX_EXP2_SCALE)
    l = jnp.sum(p, axis=-1, keepdims=True)
    return _dot(p.astype(BF16), v) / l


def _attend(qs, k, vt):
    st = _dot_nt(k, qs.astype(BF16))
    p = jnp.exp2((st - jnp.max(st, axis=0, keepdims=True)) * SOFTMAX_EXP2_SCALE)
    l = jnp.sum(p, axis=0, keepdims=True)
    return (_dot(vt, p.astype(BF16)) / l).T


def _attn_kernel(q_ref, k_ref, v_ref, ck_ref, cv_ref, qg_ref, kg_ref, cos_ref, sin_ref,
                 o_ref, kc_ref, vc_ref, k_scr, vt_scr):
    i = pl.program_id(0)
    qg = qg_ref[...]
    kg = kg_ref[...]

    def heads(kv):
        return [slice((kv * Q_PER_KV + g) * HEAD_DIM, (kv * Q_PER_KV + g + 1) * HEAD_DIM) for g in range(Q_PER_KV)]

    @pl.when(i < N_PT)
    def _():
        def seq_body(sq, carry):
            rows = pl.ds(pl.multiple_of(sq * SEQ, SEQ), SEQ)
            v = v_ref[rows, :]
            vc_ref[sq] = v
            for kv in range(N_KV_HEADS):
                ksl = slice(kv * HEAD_DIM, (kv + 1) * HEAD_DIM)
                kn = _head_rms(k_ref[rows, ksl], kg)
                kc_ref[sq, :, ksl] = kn
                qs = jnp.concatenate([_head_rms(q_ref[rows, hs], qg) for hs in heads(kv)], axis=0)
                o = _attend_short(qs, kn.astype(BF16), v[:, ksl].astype(BF16))
                for g, hs in enumerate(heads(kv)):
                    o_ref[rows, hs] = o[g * SEQ:(g + 1) * SEQ].astype(BF16)
            return carry

        lax.fori_loop(0, SEQ_PER_TILE, seq_body, 0)

    @pl.when(i >= N_PT)
    def _():
        k_scr[0:PAST_LEN, :] = ck_ref[...].astype(BF16)
        for kv in range(N_KV_HEADS):
            ksl = slice(kv * HEAD_DIM, (kv + 1) * HEAD_DIM)
            vt_scr[kv, :, 0:PAST_LEN] = cv_ref[:, ksl].T.astype(BF16)
            vt_scr[kv, :, PAST_LEN:] = v_ref[:, ksl].T.astype(BF16)
            kn = _rope(_head_rms(k_ref[:, ksl], kg), cos_ref[...], sin_ref[...])
            k_scr[PAST_LEN:, ksl] = kn.astype(BF16)

        def q_body(qb, carry):
            rows = pl.ds(pl.multiple_of(qb * ATT_TQ, ATT_TQ), ATT_TQ)
            cos = cos_ref[rows, :]
            sin = sin_ref[rows, :]
            for kv in range(N_KV_HEADS):
                ksl = slice(kv * HEAD_DIM, (kv + 1) * HEAD_DIM)
                qs = jnp.concatenate([_rope(_head_rms(q_ref[rows, hs], qg), cos, sin) for hs in heads(kv)], axis=0)
                o = _attend(qs, k_scr[:, ksl], vt_scr[kv])
                for g, hs in enumerate(heads(kv)):
                    o_ref[rows, hs] = o[g * ATT_TQ:(g + 1) * ATT_TQ].astype(BF16)
            return carry

        lax.fori_loop(0, DEC_SEQ // ATT_TQ, q_body, 0)


def _attention(p, cache_k, cache_v, qg, kg, cos_t, sin_t, l):
    cache_spec = pl.BlockSpec((None, None, PAST_LEN, KV_W), lambda i: (_sample_idx(i), l, 0, 0))
    vec = pl.BlockSpec((None, 1, HEAD_DIM), lambda i: (l, 0, 0))
    tab = pl.BlockSpec((DEC_SEQ, HEAD_DIM), lambda i: (0, 0))
    new_cache = pl.BlockSpec((SEQ_PER_TILE, SEQ, KV_W), lambda i: (_prompt_idx(i), 0, 0))
    return pl.pallas_call(
        _attn_kernel,
        grid=(N_MT,),
        in_specs=[
            pl.BlockSpec((TM, ATTN_W), lambda i: (i, 0)),
            pl.BlockSpec((TM, KV_W), lambda i: (i, COL_K)),
            pl.BlockSpec((TM, KV_W), lambda i: (i, COL_V)),
            cache_spec, cache_spec, vec, vec, tab, tab,
        ],
        out_specs=[pl.BlockSpec((TM, ATTN_W), lambda i: (i, 0)), new_cache, new_cache],
        out_shape=[
            jax.ShapeDtypeStruct((N_TOK, ATTN_W), BF16),
            jax.ShapeDtypeStruct((BATCH, SEQ, KV_W), F32),
            jax.ShapeDtypeStruct((BATCH, SEQ, KV_W), F32),
        ],
        scratch_shapes=[pltpu.VMEM((ATT_S, KV_W), BF16), pltpu.VMEM((N_KV_HEADS, HEAD_DIM, ATT_S), BF16)],
        compiler_params=_cparams(("arbitrary",)),
        name="attention",
    )(p, p, p, cache_k.reshape(DEC_BATCH, DEPTH, PAST_LEN, KV_W), cache_v.reshape(DEC_BATCH, DEPTH, PAST_LEN, KV_W),
      qg.reshape(DEPTH, 1, HEAD_DIM), kg.reshape(DEPTH, 1, HEAD_DIM), cos_t, sin_t)


def _log_sigmoid(x):
    return -(jnp.maximum(-x, 0.0) + jnp.log(1.0 + jnp.exp(-jnp.abs(x))))


def _decay_mask(qi, kj, lg_f, lg_b):
    dd = qi - kj
    log2e = math.log2(math.e)
    w = jnp.exp2(dd * jnp.where(dd > 0, lg_f * log2e, -lg_b * log2e))
    return jnp.where(dd == 0, 2.0, w)


def _retention_kernel(q_ref, k_ref, v_ref, g_ref, dl_ref, cos_ref, sin_ref, s0_ref, o_ref, sf_ref,
                      acc, qr_scr, kr_scr, vb_scr):
    i = pl.program_id(0)
    scale = HEAD_DIM ** -0.5
    lg = [[_log_sigmoid(dl_ref[d, h][0:1, :]) for h in range(RET_HEADS)] for d in range(2)]
    lg1 = [[lg[d][h][:, 0:1] for h in range(RET_HEADS)] for d in range(2)]
    hsl = [slice(h * HEAD_DIM, (h + 1) * HEAD_DIM) for h in range(RET_HEADS)]

    @pl.when(i < N_PT)
    def _():
        qi = lax.broadcasted_iota(jnp.int32, (SEQ, SEQ), 0).astype(F32)
        kj = lax.broadcasted_iota(jnp.int32, (SEQ, SEQ), 1).astype(F32)
        pos = lax.broadcasted_iota(jnp.int32, (SEQ, HEAD_DIM), 0).astype(F32)
        masks = [_decay_mask(qi, kj, lg1[0][h], lg1[1][h]) for h in range(RET_HEADS)]
        kdec_f = [jnp.exp((SEQ - 1.0 - pos) * lg[0][h]) for h in range(RET_HEADS)]
        kdec_b = [jnp.exp(pos * lg[1][h]) for h in range(RET_HEADS)]

        def seq_body(sq, carry):
            rows = pl.ds(pl.multiple_of(sq * SEQ, SEQ), SEQ)
            for h in range(RET_HEADS):
                q = q_ref[rows, hsl[h]].astype(BF16)
                k = k_ref[rows, hsl[h]] * scale
                v = v_ref[rows, hsl[h]].astype(BF16)
                a = (_dot_nt(q, k.astype(BF16)) * masks[h]).astype(BF16)
                acc[rows, hsl[h]] = _dot(a, v)
                sf_ref[sq, 0, h] = _dot((k * kdec_f[h]).T.astype(BF16), v)
                sf_ref[sq, 1, h] = _dot((k * kdec_b[h]).T.astype(BF16), v)
            return carry

        lax.fori_loop(0, SEQ_PER_TILE, seq_body, 0)

    @pl.when(i >= N_PT)
    def _():
        for h in range(RET_HEADS):
            qr_scr[:, hsl[h]] = _rope(q_ref[:, hsl[h]], cos_ref[...], sin_ref[...]).astype(BF16)
            kr_scr[:, hsl[h]] = _rope(k_ref[:, hsl[h]] * scale, cos_ref[...], sin_ref[...]).astype(BF16)
        vb_scr[...] = v_ref[...].astype(BF16)

        def q_body(qb, carry):
            row0 = qb * ATT_TQ
            rows = pl.ds(pl.multiple_of(row0, ATT_TQ), ATT_TQ)
            qi = (row0 + lax.broadcasted_iota(jnp.int32, (ATT_TQ, DEC_SEQ), 0)).astype(F32)
            kj = lax.broadcasted_iota(jnp.int32, (ATT_TQ, DEC_SEQ), 1).astype(F32)
            pos = (row0 + lax.broadcasted_iota(jnp.int32, (ATT_TQ, HEAD_DIM), 0)).astype(F32)
            for h in range(RET_HEADS):
                q = qr_scr[rows, hsl[h]]
                a = (_dot_nt(q, kr_scr[:, hsl[h]]) * _decay_mask(qi, kj, lg1[0][h], lg1[1][h])).astype(BF16)
                o = _dot(a, vb_scr[:, hsl[h]])
                o += _dot(q, s0_ref[0, h].astype(BF16)) * jnp.exp((pos + 1.0) * lg[0][h])
                o += _dot(q, s0_ref[1, h].astype(BF16)) * jnp.exp((DEC_SEQ - pos) * lg[1][h])
                acc[rows, hsl[h]] = o
            return carry

        lax.fori_loop(0, DEC_SEQ // ATT_TQ, q_body, 0)

    for h in range(RET_HEADS):
        o = acc[:, hsl[h]]
        o = o - jnp.mean(o, axis=-1, keepdims=True)
        o = o * lax.rsqrt(jnp.mean(o * o, axis=-1, keepdims=True) + EPS)
        o_ref[:, hsl[h]] = (o * _silu(g_ref[:, hsl[h]])).astype(BF16)


def _retention(p, dl, cos_t, sin_t, state_ret, l):
    tab = pl.BlockSpec((DEC_SEQ, HEAD_DIM), lambda i: (0, 0))
    st = (2, RET_HEADS, HEAD_DIM, HEAD_DIM)
    return pl.pallas_call(
        _retention_kernel,
        grid=(N_MT,),
        in_specs=[pl.BlockSpec((TM, RET_W), lambda i, k=k: (i, COL_R + k)) for k in range(4)] + [
            pl.BlockSpec((None, 2, RET_HEADS, 8, HEAD_DIM), lambda i: (l, 0, 0, 0, 0)),
            tab, tab,
            pl.BlockSpec((None, None) + st, lambda i: (_sample_idx(i), l, 0, 0, 0, 0)),
        ],
        out_specs=[
            pl.BlockSpec((TM, RET_W), lambda i: (i, 0)),
            pl.BlockSpec((SEQ_PER_TILE,) + st, lambda i: (_prompt_idx(i), 0, 0, 0, 0)),
        ],
        out_shape=[
            jax.ShapeDtypeStruct((N_TOK, RET_W), BF16),
            jax.ShapeDtypeStruct((BATCH,) + st, F32),
        ],
        scratch_shapes=[pltpu.VMEM((TM, RET_W), F32)] + [pltpu.VMEM((TM, RET_W), BF16)] * 3,
        compiler_params=_cparams(("arbitrary",)),
        name="retention",
    )(p, p, p, p, dl, cos_t, sin_t, state_ret)


def _s5_disc_kernel(are_ref, aim_ref, ldt_ref, bre_ref, bim_ref, cim_ref,
                    lre_ref, lim_ref, bbre_ref, bbim_ref, ncim_ref):
    ar = are_ref[...]
    ai = aim_ref[...]
    dt = jnp.exp(ldt_ref[...])
    mag = jnp.exp(ar * dt)
    lr = mag * jnp.cos(ai * dt)
    li = mag * jnp.sin(ai * dt)
    den = ar * ar + ai * ai
    nr = lr - 1.0
    cr = (nr * ar + li * ai) / den
    ci = (li * ar - nr * ai) / den
    br = bre_ref[...]
    bi = bim_ref[...]
    lre_ref[...] = lr
    lim_ref[...] = li
    bbre_ref[...] = cr * br - ci * bi
    bbim_ref[...] = cr * bi + ci * br
    ncim_ref[...] = -cim_ref[...]


def _s5_discretize(a_re, a_im, log_dt, b_re, b_im, c_im):
    rows = DEPTH * 2 * SSM_GROUPS
    cols = SSM_STATE * SSM_GROUP
    shp = (DEPTH, 2, SSM_GROUPS, SSM_STATE, SSM_GROUP)
    args = [
        jnp.broadcast_to(a_re[..., None], shp).reshape(rows, cols),
        jnp.broadcast_to(a_im[..., None], shp).reshape(rows, cols),
        jnp.broadcast_to(log_dt[..., None, None], shp).reshape(rows, cols),
        b_re.reshape(rows, cols), b_im.reshape(rows, cols), c_im.reshape(rows, cols),
    ]
    spec = pl.BlockSpec((rows, cols), lambda: (0, 0))
    outs = pl.pallas_call(
        _s5_disc_kernel,
        in_specs=[spec] * 6,
        out_specs=[spec] * 5,
        out_shape=[jax.ShapeDtypeStruct((rows, cols), F32)] * 5,
        name="s5_discretize",
    )(*args)
    lre, lim, bbre, bbim, ncim = outs
    lre = lre.reshape(shp)[..., 0]
    lim = lim.reshape(shp)[..., 0]
    return (lre, lim, bbre.reshape(shp), bbim.reshape(shp),
            ncim.reshape(DEPTH, 2, SSM_GROUPS, SSM_GROUP, SSM_STATE))


def _state_cols(x):
    lead = x.shape[:-3]
    x = x.reshape(lead + (N_STILE, 2, SSM_STATE, 2))
    x = jnp.moveaxis(x, -1, -3)
    return x.reshape(lead + (STATE_W,))


def _state_uncols(x):
    lead = x.shape[:-1]
    x = x.reshape(lead + (N_STILE, 2, 2, SSM_STATE))
    x = jnp.moveaxis(x, -3, -1)
    return x.reshape(lead + (SSM_GROUPS, SSM_STATE, 2))


def _s5_compact(lre, lim, bbre, bbim, c_re, ncim):
    n = DEPTH * 2
    bb = jnp.stack([bbre, bbim], axis=-1)
    bb = bb.reshape(n, N_STILE, 2, SSM_STATE, SSM_GROUP, 2)
    wb = jnp.transpose(bb, (0, 4, 1, 5, 2, 3)).reshape(DEPTH, 2, SSM_GROUP, STATE_W)
    cc = jnp.stack([c_re, ncim], axis=-1)
    cc = cc.reshape(n, N_STILE, 2, SSM_GROUP, SSM_STATE, 2)
    wc = jnp.transpose(cc, (0, 3, 1, 5, 2, 4)).reshape(DEPTH, 2, SSM_GROUP, STATE_W)
    lam = jnp.concatenate([lre.reshape(DEPTH, 2, N_STILE, 128), lim.reshape(DEPTH, 2, N_STILE, 128)], axis=2)
    return wb, wc, lam


def _group_mask(cols):
    row_g = lax.broadcasted_iota(jnp.int32, (256, cols), 0) // SSM_GROUP
    col = lax.broadcasted_iota(jnp.int32, (256, cols), 1)
    col_g = ((col // 256) % 8) * 2 + (col % 128) // SSM_STATE
    return row_g == col_g


def _s5_scan_kernel(u_ref, wb_ref, wc_ref, lam_ref, h0_ref, y_ref, hf_ref,
                    pm_scr, bm_scr, cm_scr, hbuf, y_scr, state, ends, *, segments):
    npass = 1 if segments == 1 else 2
    d = pl.program_id(0)
    ps = pl.program_id(1)
    ck = pl.program_id(2)
    nck = pl.num_programs(2)
    rows = SSM_TC * SSM_ROWS
    half = STATE_W // 2
    lam = lam_ref[...]

    @pl.when(jnp.logical_and(ps == 0, ck == 0))
    def _():
        a = lax.broadcasted_iota(jnp.int32, (rows, rows), 0)
        b = lax.broadcasted_iota(jnp.int32, (rows, rows), 1)
        same = jnp.logical_and(a // SSM_ROWS == b % SSM_TC, a % SSM_ROWS == b // SSM_TC)
        pm_scr[...] = jnp.where(same, 1.0, 0.0).astype(BF16)
        wb = jnp.tile(wb_ref[...], (256 // SSM_GROUP, 1))
        bm_scr[...] = jnp.where(_group_mask(STATE_W), wb, 0.0).astype(BF16)
        for n_ in range(2):
            wc = jnp.tile(wc_ref[:, n_ * half:(n_ + 1) * half], (256 // SSM_GROUP, 1))
            cm_scr[n_] = jnp.where(_group_mask(half), wc, 0.0).astype(BF16)

    @pl.when(ck == 0)
    def _():
        if npass == 1:
            state[...] = h0_ref[0]
        else:
            @pl.when(ps == 0)
            def _():
                state[...] = jnp.zeros_like(state)

            @pl.when(ps == 1)
            def _():
                ends[...] = state[...]
                state[...] = h0_ref[0]
                lr = lam[0:N_STILE]
                li = lam[N_STILE:]
                for _ in range(int(math.log2(SSM_SEG_LEN))):
                    lr, li = lr * lr - li * li, 2.0 * lr * li

                def carry(order, prev):
                    for sg in order:
                        for sq in range(SSM_ROWS // segments):
                            r = sq * segments + sg
                            q = r + prev
                            for j in range(N_STILE):
                                re = slice(j * 256, j * 256 + 128)
                                im = slice(j * 256 + 128, (j + 1) * 256)
                                pr = state[q:q + 1, re]
                                pi = state[q:q + 1, im]
                                ar = lr[j:j + 1]
                                ai = li[j:j + 1]
                                state[r:r + 1, re] = ends[q:q + 1, re] + ar * pr - ai * pi
                                state[r:r + 1, im] = ends[q:q + 1, im] + ar * pi + ai * pr

                @pl.when(d == 0)
                def _():
                    carry(range(1, segments), -1)

                @pl.when(d == 1)
                def _():
                    carry(range(segments - 2, -1, -1), 1)

    u = _dot(pm_scr[...], u_ref[...].reshape(rows, SSM_W).astype(BF16)).astype(BF16)
    for j in range(N_STILE):
        k0 = (j // 8) * 256
        hbuf[:, j * 256:(j + 1) * 256] = _dot(u[:, k0:k0 + 256], bm_scr[:, j * 256:(j + 1) * 256])

    JT = 4
    for jb in range(N_STILE // JT):
        tiles = list(range(jb * JT, (jb + 1) * JT))
        lrs = [jnp.broadcast_to(lam[j:j + 1], (SSM_ROWS, 128)) for j in tiles]
        lis = [jnp.broadcast_to(lam[N_STILE + j:N_STILE + j + 1], (SSM_ROWS, 128)) for j in tiles]
        init = tuple(state[:, j * 256:j * 256 + 128] for j in tiles) + \
            tuple(state[:, j * 256 + 128:(j + 1) * 256] for j in tiles)

        def body(t, carry_, tiles=tiles, lrs=lrs, lis=lis):
            tt = t + d * (SSM_TC - 1 - 2 * t)
            r = pl.ds(pl.multiple_of(tt * SSM_ROWS, SSM_ROWS), SSM_ROWS)
            out_r, out_i = [], []
            for n_, j in enumerate(tiles):
                hr, hi = carry_[n_], carry_[JT + n_]
                re = slice(j * 256, j * 256 + 128)
                im = slice(j * 256 + 128, (j + 1) * 256)
                nr = lrs[n_] * hr - lis[n_] * hi + hbuf[r, re]
                ni = lrs[n_] * hi + lis[n_] * hr + hbuf[r, im]
                hbuf[r, re] = nr
                hbuf[r, im] = ni
                out_r.append(nr)
                out_i.append(ni)
            return tuple(out_r) + tuple(out_i)

        fin = lax.fori_loop(0, SSM_TC, body, init, unroll=2)
        for n_, j in enumerate(tiles):
            state[:, j * 256:j * 256 + 128] = fin[n_]
            state[:, j * 256 + 128:(j + 1) * 256] = fin[JT + n_]

    @pl.when(ps == npass - 1)
    def _():
        hb = hbuf[...].astype(BF16)
        for n_ in range(2):
            y = _dot_nt(hb[:, n_ * half:(n_ + 1) * half], cm_scr[n_])
            y_scr[2 * n_] = y[:, :128]
            y_scr[2 * n_ + 1] = y[:, 128:]
        for r in range(SSM_ROWS):
            for c_ in range(SSM_W // 128):
                y_ref[r, :, c_ * 128:(c_ + 1) * 128] = y_scr[c_, pl.ds(r, SSM_TC, stride=SSM_ROWS), :]

    @pl.when(jnp.logical_and(ps == npass - 1, ck == nck - 1))
    def _():
        hf_ref[0] = state[...]


def _s5_scan(p3, row_block, wb, wc, lam, h0, l, *, segments):
    steps = p3.shape[1]
    nck = steps // SSM_TC
    npass = 1 if segments == 1 else 2

    def chunk(d, c):
        return c + d * (nck - 1 - 2 * c)

    def y_chunk(d, p, c):
        return jnp.where(p == npass - 1, chunk(d, c), chunk(d, 0))

    par = pl.BlockSpec((None, None, SSM_GROUP, STATE_W), lambda d, p, c: (l, d, 0, 0))
    st = pl.BlockSpec((1, SSM_ROWS, STATE_W), lambda d, p, c: (d, 0, 0))
    return pl.pallas_call(
        functools.partial(_s5_scan_kernel, segments=segments),
        grid=(2, npass, nck),
        in_specs=[
            pl.BlockSpec((SSM_ROWS, SSM_TC, SSM_W), lambda d, p, c: (row_block, chunk(d, c), COL_U)),
            par, par,
            pl.BlockSpec((None, None, 2 * N_STILE, 128), lambda d, p, c: (l, d, 0, 0)),
            st,
        ],
        out_specs=[
            pl.BlockSpec((None, SSM_ROWS, SSM_TC, SSM_W), lambda d, p, c: (d, 0, y_chunk(d, p, c), 0)),
            st,
        ],
        out_shape=[
            jax.ShapeDtypeStruct((2, SSM_ROWS, steps, SSM_W), F32),
            jax.ShapeDtypeStruct((2, SSM_ROWS, STATE_W), F32),
        ],
        scratch_shapes=[
            pltpu.VMEM((SSM_TC * SSM_ROWS, SSM_TC * SSM_ROWS), BF16),
            pltpu.VMEM((256, STATE_W), BF16),
            pltpu.VMEM((2, 256, STATE_W // 2), BF16),
            pltpu.VMEM((SSM_TC * SSM_ROWS, STATE_W), F32),
            pltpu.VMEM((SSM_W // 128, SSM_TC * SSM_ROWS, 128), F32),
            pltpu.VMEM((SSM_ROWS, STATE_W), F32),
            pltpu.VMEM((SSM_ROWS, STATE_W), F32),
        ],
        compiler_params=_cparams(("arbitrary", "arbitrary", "arbitrary")),
        name="s5_scan_seg%d" % segments,
    )(p3, wb, wc, lam, h0)


def _s5_glu_kernel(ypf_ref, ypb_ref, ysf_ref, ysb_ref, u_ref, d_ref, w_ref, o_ref):
    i = pl.program_id(0)
    w = w_ref[...].astype(BF16)

    def glu(y):
        y = y + d_ref[...] * u_ref[...]
        z = _dot(jax.nn.gelu(y).astype(BF16), w)
        o_ref[...] = (z[:, :SSM_W] * _sigmoid(z[:, SSM_W:])).astype(BF16)

    @pl.when(i < N_PT)
    def _():
        glu(ypf_ref[...] + ypb_ref[...])

    @pl.when(i >= N_PT)
    def _():
        glu(ysf_ref[...] + ysb_ref[...])


def _s5_glu(yp, ys, p, d, w, l):
    return pl.pallas_call(
        _s5_glu_kernel,
        grid=(N_MT,),
        in_specs=[
            pl.BlockSpec((None, TM, SSM_W), lambda i: (0, _prompt_idx(i), 0)),
            pl.BlockSpec((None, TM, SSM_W), lambda i: (1, _prompt_idx(i), 0)),
            pl.BlockSpec((None, TM, SSM_W), lambda i: (0, _sample_idx(i), 0)),
            pl.BlockSpec((None, TM, SSM_W), lambda i: (1, _sample_idx(i), 0)),
            pl.BlockSpec((TM, SSM_W), lambda i: (i, COL_U)),
            pl.BlockSpec((None, 1, SSM_W), lambda i: (l, 0, 0)),
            pl.BlockSpec((None, SSM_W, 2 * SSM_W), lambda i: (l, 0, 0)),
        ],
        out_specs=pl.BlockSpec((TM, SSM_W), lambda i: (i, 0)),
        out_shape=jax.ShapeDtypeStruct((N_TOK, SSM_W), BF16),
        compiler_params=_cparams(("arbitrary",)),
        name="s5_glu",
    )(yp, yp, ys, ys, p, d.reshape(DEPTH, 1, SSM_W), w)


def _s5_layer(p, s5p, st_l, ssm_d, w_ssm_glu, l):
    wb, wc, lam = s5p
    yp, hfin = _s5_scan(p.reshape(N_TOK // SEQ, SEQ, IN_W), 0, wb, wc, lam,
                        jnp.zeros((2, SSM_ROWS, STATE_W), F32), l, segments=1)
    h0 = _state_cols(jnp.moveaxis(st_l, 0, 1))
    h0 = jnp.stack([
        jnp.zeros((DEC_BATCH, SSM_SEGS, STATE_W), F32).at[:, 0].set(h0[0]),
        jnp.zeros((DEC_BATCH, SSM_SEGS, STATE_W), F32).at[:, SSM_SEGS - 1].set(h0[1]),
    ]).reshape(2, SSM_ROWS, STATE_W)
    ys, _ = _s5_scan(p.reshape(N_TOK // SSM_SEG_LEN, SSM_SEG_LEN, IN_W), N_PROMPT // SSM_SEG_LEN // SSM_ROWS,
                     wb, wc, lam, h0, l, segments=SSM_SEGS)
    out = _s5_glu(yp.reshape(2, N_PROMPT, SSM_W), ys.reshape(2, N_SAMPLE, SSM_W), p, ssm_d, w_ssm_glu, l)
    new_state = jnp.moveaxis(_state_uncols(hfin), 0, 1)
    return out, new_state


def _rope_tables():
    rows = DEC_SEQ // GRID_W
    r = jnp.repeat(jnp.arange(rows), GRID_W).astype(F32)
    col = jnp.tile(jnp.arange(GRID_W), rows).astype(F32)
    inv = ROPE_BASE ** (-jnp.arange(ROPE_FREQS, dtype=F32) / ROPE_FREQS)
    ang = jnp.stack([r, col], axis=-1)[:, :, None] * inv
    cos = jnp.cos(ang)
    sin = jnp.sin(ang)
    cos_t = jnp.concatenate([cos, cos], axis=-1).reshape(DEC_SEQ, HEAD_DIM)
    sin_t = jnp.concatenate([-sin, sin], axis=-1).reshape(DEC_SEQ, HEAD_DIM)
    return cos_t, sin_t


def kernel(x_prompt, x_sample, cache_k, cache_v, state_ssm, state_ret, c, c_ctx, w_mod, b_mod, norm_g,
           w_ffn_in, w_ffn_out, w_in, w_out, q_norm_g, k_norm_g, ssm_a_re, ssm_a_im, ssm_log_dt,
           ssm_b_re, ssm_b_im, ssm_c_re, ssm_c_im, ssm_d, w_ssm_glu, ret_decay_logit, final_norm_g):
    x = (x_prompt.reshape(N_PROMPT, D_MODEL), x_sample.reshape(N_SAMPLE, D_MODEL))
    cond8 = jnp.concatenate([c_ctx[None], c, jnp.zeros((5, D_MODEL), F32)], axis=0)
    mods = _adaln(cond8, w_mod, b_mod).reshape(DEPTH, 8, N_SUB * 3, D_MODEL)
    modt = jnp.transpose(mods[:, jnp.array(TILE_MOD, jnp.int32)], (0, 2, 1, 3))[:, :, :, None, :]
    gains = norm_g.reshape(DEPTH * N_SUB, 1, D_MODEL)
    cos_t, sin_t = _rope_tables()
    lre, lim, bbre, bbim, ncim = _s5_discretize(ssm_a_re, ssm_a_im, ssm_log_dt, ssm_b_re, ssm_b_im, ssm_c_im)
    s5p = _s5_compact(lre, lim, bbre, bbim, ssm_c_re, ncim)
    dl = jnp.broadcast_to(ret_decay_logit[:, :, :, None, None], (DEPTH, 2, RET_HEADS, 8, HEAD_DIM))
    ks_, vs_, hs_, ss_ = [], [], [], []
    for l in range(DEPTH):
        x = _ffn(x, gains, modt, w_ffn_in, w_ffn_out, l, 0)
        p = _inproj(x, gains, modt, w_in, l)
        attn, k_l, v_l = _attention(p, cache_k, cache_v, q_norm_g, k_norm_g, cos_t, sin_t, l)
        ssm, h_l = _s5_layer(p, s5p, state_ssm[:, l], ssm_d, w_ssm_glu, l)
        ret, s_l = _retention(p, dl, cos_t, sin_t, state_ret, l)
        x = _outproj(x, attn, ssm, ret, modt, w_out, l)
        x = _ffn(x, gains, modt, w_ffn_in, w_ffn_out, l, 1, final_g=final_norm_g if l == DEPTH - 1 else None)
        ks_.append(k_l.reshape(BATCH, SEQ, N_KV_HEADS, HEAD_DIM))
        vs_.append(v_l.reshape(BATCH, SEQ, N_KV_HEADS, HEAD_DIM))
        hs_.append(h_l)
        ss_.append(s_l)
    y_prompt = x[0].reshape(BATCH, SEQ, D_MODEL)
    y_sample = x[1].reshape(DEC_BATCH, DEC_SEQ, D_MODEL)
    return (y_prompt, y_sample, jnp.stack(ks_, axis=1), jnp.stack(vs_, axis=1),
            jnp.stack(hs_, axis=1), jnp.stack(ss_, axis=1))
```

```python
import functools
import math

import jax
import jax.numpy as jnp
from jax import lax
from jax.experimental import pallas as pl
from jax.experimental.pallas import tpu as pltpu

D_MODEL = 2048
BATCH = 16
SEQ = 256
DEPTH = 2
DEC_BATCH = 2
DEC_SEQ = 1024
PAST_LEN = 512
GRID_W = 64
HEAD_DIM = 128
N_Q_HEADS = 8
N_KV_HEADS = 2
Q_PER_KV = 4
ATTN_W = 1024
KV_W = 256
SSM_W = 512
SSM_GROUP = 16
SSM_GROUPS = 32
SSM_STATE = 64
RET_HEADS = 4
RET_W = 512
IN_W = 4096
D_FF = 5632
N_SUB = 3
RET_CHUNK = 128
ROPE_BASE = 10000.0
ROPE_FREQS = 32
EPS = 1e-6

N_PROMPT = BATCH * SEQ
N_SAMPLE = DEC_BATCH * DEC_SEQ
N_TOK = N_PROMPT + N_SAMPLE
TM = 1024
N_MT = N_TOK // TM
N_PT = N_PROMPT // TM
SEQ_PER_TILE = TM // SEQ
TILE_MOD = (0, 0, 0, 0, 1, 2)
ROW_CHUNK = 256
TF = 256
FFN_NSPLIT = 4
TN = 512
STATE_W = 2 * SSM_GROUPS * SSM_STATE
N_STILE = STATE_W // 256
SSM_ROWS = 16
SSM_SEGS = 8
SSM_SEG_LEN = DEC_SEQ // SSM_SEGS
SSM_TC = 32
ATT_TQ = 256
ATT_S = PAST_LEN + DEC_SEQ
VMEM_LIMIT = 56 * 1024 * 1024
VMEM_LIMIT_FFN = 60 * 1024 * 1024

COL_K = ATTN_W // KV_W
COL_V = COL_K + 1
COL_U = (ATTN_W + 2 * KV_W) // SSM_W
COL_R = (ATTN_W + 2 * KV_W + SSM_W) // RET_W

BF16 = jnp.bfloat16
F32 = jnp.float32


def _cparams(sem, limit=VMEM_LIMIT):
    return pltpu.CompilerParams(dimension_semantics=sem, vmem_limit_bytes=limit)


def _dot(a, b):
    return jnp.dot(a, b, preferred_element_type=F32)


def _dot_nt(a, b):
    return lax.dot_general(a, b, (((1,), (1,)), ((), ())), preferred_element_type=F32)


def _sigmoid(x):
    return 1.0 / (1.0 + jnp.exp(-x))


def _silu(x):
    return x * _sigmoid(x)


def _sample_idx(i):
    return jnp.clip(i - N_PT, 0, DEC_BATCH - 1)


def _prompt_idx(i):
    return jnp.minimum(i, N_PT - 1)


def _row_chunks(n_rows, body):
    def step(c, carry):
        body(pl.ds(pl.multiple_of(c * ROW_CHUNK, ROW_CHUNK), ROW_CHUNK))
        return carry

    lax.fori_loop(0, n_rows // ROW_CHUNK, step, 0)


def _mod_spec(l, k, width=D_MODEL, col=lambda j: 0, tile=lambda i: i):
    return pl.BlockSpec((None, None, 1, 1, width), lambda i, j: (l, k, tile(i), 0, col(j)))


def _gain_spec(l, sub):
    return pl.BlockSpec((None, 1, D_MODEL), lambda i, j: (l * N_SUB + sub, 0, 0))


def _adaln_kernel(c_ref, w_ref, b_ref, o_ref):
    a = _silu(c_ref[...]).astype(BF16)
    o_ref[0] = _dot(a, w_ref[0].astype(BF16)) + b_ref[0]


def _adaln(cond8, w_mod, b_mod):
    n = w_mod.shape[-1]
    tn = 1024
    return pl.pallas_call(
        _adaln_kernel,
        grid=(DEPTH, n // tn),
        in_specs=[
            pl.BlockSpec((8, D_MODEL), lambda l, j: (0, 0)),
            pl.BlockSpec((1, D_MODEL, tn), lambda l, j: (l, 0, j)),
            pl.BlockSpec((1, 1, tn), lambda l, j: (l, 0, j)),
        ],
        out_specs=pl.BlockSpec((1, 8, tn), lambda l, j: (l, 0, j)),
        out_shape=jax.ShapeDtypeStruct((DEPTH, 8, n), F32),
        compiler_params=_cparams(("arbitrary", "arbitrary")),
        name="adaln",
    )(cond8, w_mod, b_mod.reshape(DEPTH, 1, n))


def _norm_mod(x, g, shift, scale):
    ms = jnp.mean(x * x, axis=-1, keepdims=True)
    return x * lax.rsqrt(ms + EPS) * (g * (1.0 + scale)) + shift


def _ffn_kernel(*refs, n_x, n_out, final):
    refs = list(refs)
    x_refs = refs[:n_x]
    g_ref, sh_ref, sc_ref, gt_ref, wa_ref, wu_ref, wo_ref = refs[n_x:n_x + 7]
    rest = refs[n_x + 7:]
    fg_ref = rest.pop(0) if final else None
    o_refs, h_scr = rest[:n_out], rest[n_out]
    i = pl.program_id(0)
    j = pl.program_id(1)

    def run(x_ref, o_ref):
        @pl.when(j == 0)
        def _():
            def pre(rows):
                h = _norm_mod(x_ref[rows, :], g_ref[...], sh_ref[0], sc_ref[0])
                h_scr[rows, :] = h.astype(BF16)

            _row_chunks(TM, pre)
            o_ref[...] = jnp.zeros_like(o_ref)

        h = h_scr[...]
        a = _dot(h, wa_ref[...].astype(BF16))
        u = _dot(h, wu_ref[...].astype(BF16))
        mid = (_silu(a) * u).astype(BF16)
        wn = D_MODEL // FFN_NSPLIT
        for n in range(FFN_NSPLIT):
            cs = slice(n * wn, (n + 1) * wn)
            o_ref[:, cs] += _dot(mid, wo_ref[:, cs].astype(BF16))

        @pl.when(j == pl.num_programs(1) - 1)
        def _():
            def post(rows):
                out = x_ref[rows, :] + (0.5 * gt_ref[0]) * o_ref[rows, :]
                if final:
                    ms = jnp.mean(out * out, axis=-1, keepdims=True)
                    out = out * lax.rsqrt(ms + EPS) * fg_ref[...]
                o_ref[rows, :] = out

            _row_chunks(TM, post)

    if n_x == 1 and n_out == 1:
        run(x_refs[0], o_refs[0])
    else:
        @pl.when(i < N_PT)
        def _():
            run(x_refs[0], o_refs[0])

        @pl.when(i >= N_PT)
        def _():
            run(x_refs[-1], o_refs[-1])


def _split_tok_specs():
    return [pl.BlockSpec((TM, D_MODEL), lambda i, j: (_prompt_idx(i), 0)),
            pl.BlockSpec((TM, D_MODEL), lambda i, j: (_sample_idx(i), 0), pipeline_mode=pl.Buffered(1))]


def _ffn(x, gains, modt, w_in, w_out, l, f, final_g=None):
    final = final_g is not None
    xs = list(x) if isinstance(x, (tuple, list)) else [x]
    sub = 2 * f
    nf = D_FF // TF
    tok = pl.BlockSpec((TM, D_MODEL), lambda i, j: (i, 0))
    in_specs = (_split_tok_specs() if len(xs) == 2 else [tok]) + [
        _gain_spec(l, sub), _mod_spec(l, 3 * sub), _mod_spec(l, 3 * sub + 1), _mod_spec(l, 3 * sub + 2),
        pl.BlockSpec((None, None, D_MODEL, TF), lambda i, j: (l, f, 0, j)),
        pl.BlockSpec((None, None, D_MODEL, TF), lambda i, j: (l, f, 0, j + nf)),
        pl.BlockSpec((None, None, TF, D_MODEL), lambda i, j: (l, f, j, 0)),
    ]
    args = xs + [gains, modt, modt, modt, w_in, w_in, w_out]
    if final:
        in_specs.append(pl.BlockSpec((1, D_MODEL), lambda i, j: (0, 0)))
        args.append(final_g.reshape(1, D_MODEL))
        out_specs = _split_tok_specs()
        out_shape = [jax.ShapeDtypeStruct((N_PROMPT, D_MODEL), F32), jax.ShapeDtypeStruct((N_SAMPLE, D_MODEL), F32)]
    else:
        out_specs = tok
        out_shape = jax.ShapeDtypeStruct((N_TOK, D_MODEL), F32)
    return pl.pallas_call(
        functools.partial(_ffn_kernel, n_x=len(xs), n_out=2 if final else 1, final=final),
        grid=(N_MT, nf),
        in_specs=in_specs,
        out_specs=out_specs,
        out_shape=out_shape,
        scratch_shapes=[pltpu.VMEM((TM, D_MODEL), BF16)],
        compiler_params=_cparams(("arbitrary", "arbitrary"), VMEM_LIMIT_FFN),
        name="ffn_final" if final else ("ffn_first" if len(xs) == 2 else "ffn"),
    )(*args)


def _resident_weight_spec(rows, l, ncol):
    return pl.BlockSpec((None, rows, TN), lambda i, j: (l, 0, jnp.where(i == 0, j, ncol - 1)))


def _inproj_kernel(x_ref, g_ref, sh_ref, sc_ref, w_ref, o_ref, h_scr, w_scr):
    i = pl.program_id(0)
    j = pl.program_id(1)

    @pl.when(j == 0)
    def _():
        def pre(rows):
            h = _norm_mod(x_ref[rows, :], g_ref[...], sh_ref[0], sc_ref[0])
            h_scr[rows, :] = h.astype(BF16)

        _row_chunks(TM, pre)

    @pl.when(i == 0)
    def _():
        w_scr[j] = w_ref[...].astype(BF16)

    o_ref[...] = _dot(h_scr[...], w_scr[j])


def _inproj(x, gains, modt, w, l):
    ncol = IN_W // TN
    return pl.pallas_call(
        _inproj_kernel,
        grid=(N_MT, ncol),
        in_specs=[
            pl.BlockSpec((TM, D_MODEL), lambda i, j: (i, 0)),
            _gain_spec(l, 1), _mod_spec(l, 3), _mod_spec(l, 4),
            _resident_weight_spec(D_MODEL, l, ncol),
        ],
        out_specs=pl.BlockSpec((TM, TN), lambda i, j: (i, j)),
        out_shape=jax.ShapeDtypeStruct((N_TOK, IN_W), F32),
        scratch_shapes=[pltpu.VMEM((TM, D_MODEL), BF16), pltpu.VMEM((ncol, D_MODEL, TN), BF16)],
        compiler_params=_cparams(("arbitrary", "arbitrary")),
        name="inproj",
    )(x, gains, modt, modt, w)


def _outproj_kernel(x_ref, a_ref, s_ref, r_ref, gt_ref, w_ref, o_ref, w_scr):
    i = pl.program_id(0)
    j = pl.program_id(1)

    @pl.when(i == 0)
    def _():
        w_scr[j] = w_ref[...].astype(BF16)

    y = _dot(a_ref[...], w_scr[j, 0:ATTN_W, :])
    y += _dot(s_ref[...], w_scr[j, ATTN_W:ATTN_W + SSM_W, :])
    y += _dot(r_ref[...], w_scr[j, ATTN_W + SSM_W:, :])
    o_ref[...] = x_ref[...] + gt_ref[0] * y


def _outproj(x, attn, ssm, ret, modt, w, l):
    ncol = D_MODEL // TN
    return pl.pallas_call(
        _outproj_kernel,
        grid=(N_MT, ncol),
        in_specs=[
            pl.BlockSpec((TM, TN), lambda i, j: (i, j)),
            pl.BlockSpec((TM, ATTN_W), lambda i, j: (i, 0)),
            pl.BlockSpec((TM, SSM_W), lambda i, j: (i, 0)),
            pl.BlockSpec((TM, RET_W), lambda i, j: (i, 0)),
            _mod_spec(l, 5, TN, lambda j: j),
            _resident_weight_spec(D_MODEL, l, ncol),
        ],
        out_specs=pl.BlockSpec((TM, TN), lambda i, j: (i, j)),
        out_shape=jax.ShapeDtypeStruct((N_TOK, D_MODEL), F32),
        scratch_shapes=[pltpu.VMEM((ncol, D_MODEL, TN), BF16)],
        compiler_params=_cparams(("arbitrary", "arbitrary")),
        name="outproj",
    )(x, attn, ssm, ret, modt, w)


def _head_rms(x, g):
    ms = jnp.mean(x * x, axis=-1, keepdims=True)
    return x * lax.rsqrt(ms + EPS) * g


def _rope(x, cos, sin_signed):
    lane = lax.broadcasted_iota(jnp.int32, x.shape, 1)
    partner = jnp.where((lane % 64) < 32, pltpu.roll(x, 96, 1), pltpu.roll(x, 32, 1))
    return x * cos + partner * sin_signed


SOFTMAX_EXP2_SCALE = HEAD_DIM ** -0.5 * math.log2(math.e)


def _attend_short(qs, k, v):
    s = _dot_nt(qs.astype(BF16), k)
    p = jnp.exp2((s - jnp.max(s, axis=-1, keepdims=True)) * SOFTMAX_EXP2_SCALE)
    l = jnp.sum(p, axis=-1, keepdims=True)
    return _dot(p.astype(BF16), v) / l


def _attend(qs, k, vt):
    st = _dot_nt(k, qs.astype(BF16))
    p = jnp.exp2((st - jnp.max(st, axis=0, keepdims=True)) * SOFTMAX_EXP2_SCALE)
    l = jnp.sum(p, axis=0, keepdims=True)
    return (_dot(vt, p.astype(BF16)) / l).T


def _attn_kernel(q_ref, k_ref, v_ref, ck_ref, cv_ref, qg_ref, kg_ref, cos_ref, sin_ref,
                 o_ref, kc_ref, vc_ref, k_scr, vt_scr):
    i = pl.program_id(0)
    qg = qg_ref[...]
    kg = kg_ref[...]

    def heads(kv):
        return [slice((kv * Q_PER_KV + g) * HEAD_DIM, (kv * Q_PER_KV + g + 1) * HEAD_DIM) for g in range(Q_PER_KV)]

    @pl.when(i < N_PT)
    def _():
        def seq_body(sq, carry):
            rows = pl.ds(pl.multiple_of(sq * SEQ, SEQ), SEQ)
            v = v_ref[rows, :]
            vc_ref[sq] = v
            for kv in range(N_KV_HEADS):
                ksl = slice(kv * HEAD_DIM, (kv + 1) * HEAD_DIM)
                kn = _head_rms(k_ref[rows, ksl], kg)
                kc_ref[sq, :, ksl] = kn
                qs = jnp.concatenate([_head_rms(q_ref[rows, hs], qg) for hs in heads(kv)], axis=0)
                o = _attend_short(qs, kn.astype(BF16), v[:, ksl].astype(BF16))
                for g, hs in enumerate(heads(kv)):
                    o_ref[rows, hs] = o[g * SEQ:(g + 1) * SEQ].astype(BF16)
            return carry

        lax.fori_loop(0, SEQ_PER_TILE, seq_body, 0)

    @pl.when(i >= N_PT)
    def _():
        k_scr[0:PAST_LEN, :] = ck_ref[...].astype(BF16)
        for kv in range(N_KV_HEADS):
            ksl = slice(kv * HEAD_DIM, (kv + 1) * HEAD_DIM)
            vt_scr[kv, :, 0:PAST_LEN] = cv_ref[:, ksl].T.astype(BF16)
            vt_scr[kv, :, PAST_LEN:] = v_ref[:, ksl].T.astype(BF16)
            kn = _rope(_head_rms(k_ref[:, ksl], kg), cos_ref[...], sin_ref[...])
            k_scr[PAST_LEN:, ksl] = kn.astype(BF16)

        def q_body(qb, carry):
            rows = pl.ds(pl.multiple_of(qb * ATT_TQ, ATT_TQ), ATT_TQ)
            cos = cos_ref[rows, :]
            sin = sin_ref[rows, :]
            for kv in range(N_KV_HEADS):
                ksl = slice(kv * HEAD_DIM, (kv + 1) * HEAD_DIM)
                qs = jnp.concatenate([_rope(_head_rms(q_ref[rows, hs], qg), cos, sin) for hs in heads(kv)], axis=0)
                o = _attend(qs, k_scr[:, ksl], vt_scr[kv])
                for g, hs in enumerate(heads(kv)):
                    o_ref[rows, hs] = o[g * ATT_TQ:(g + 1) * ATT_TQ].astype(BF16)
            return carry

        lax.fori_loop(0, DEC_SEQ // ATT_TQ, q_body, 0)


def _attention(p, cache_k, cache_v, qg, kg, cos_t, sin_t, l):
    cache_spec = pl.BlockSpec((None, None, PAST_LEN, KV_W), lambda i: (_sample_idx(i), l, 0, 0))
    vec = pl.BlockSpec((None, 1, HEAD_DIM), lambda i: (l, 0, 0))
    tab = pl.BlockSpec((DEC_SEQ, HEAD_DIM), lambda i: (0, 0))
    new_cache = pl.BlockSpec((SEQ_PER_TILE, SEQ, KV_W), lambda i: (_prompt_idx(i), 0, 0))
    return pl.pallas_call(
        _attn_kernel,
        grid=(N_MT,),
        in_specs=[
            pl.BlockSpec((TM, ATTN_W), lambda i: (i, 0)),
            pl.BlockSpec((TM, KV_W), lambda i: (i, COL_K)),
            pl.BlockSpec((TM, KV_W), lambda i: (i, COL_V)),
            cache_spec, cache_spec, vec, vec, tab, tab,
        ],
        out_specs=[pl.BlockSpec((TM, ATTN_W), lambda i: (i, 0)), new_cache, new_cache],
        out_shape=[
            jax.ShapeDtypeStruct((N_TOK, ATTN_W), BF16),
            jax.ShapeDtypeStruct((BATCH, SEQ, KV_W), F32),
            jax.ShapeDtypeStruct((BATCH, SEQ, KV_W), F32),
        ],
        scratch_shapes=[pltpu.VMEM((ATT_S, KV_W), BF16), pltpu.VMEM((N_KV_HEADS, HEAD_DIM, ATT_S), BF16)],
        compiler_params=_cparams(("arbitrary",)),
        name="attention",
    )(p, p, p, cache_k.reshape(DEC_BATCH, DEPTH, PAST_LEN, KV_W), cache_v.reshape(DEC_BATCH, DEPTH, PAST_LEN, KV_W),
      qg.reshape(DEPTH, 1, HEAD_DIM), kg.reshape(DEPTH, 1, HEAD_DIM), cos_t, sin_t)


def _log_sigmoid(x):
    return -(jnp.maximum(-x, 0.0) + jnp.log(1.0 + jnp.exp(-jnp.abs(x))))


def _decay_mask(qi, kj, lg_f, lg_b):
    dd = qi - kj
    log2e = math.log2(math.e)
    w = jnp.exp2(dd * jnp.where(dd > 0, lg_f * log2e, -lg_b * log2e))
    return jnp.where(dd == 0, 2.0, w)


def _retention_kernel(q_ref, k_ref, v_ref, g_ref, dl_ref, cos_ref, sin_ref, s0_ref, o_ref, sf_ref,
                      acc, qr_scr, kr_scr, vb_scr):
    i = pl.program_id(0)
    scale = HEAD_DIM ** -0.5
    lg = [[_log_sigmoid(dl_ref[d, h][0:1, :]) for h in range(RET_HEADS)] for d in range(2)]
    lg1 = [[lg[d][h][:, 0:1] for h in range(RET_HEADS)] for d in range(2)]
    hsl = [slice(h * HEAD_DIM, (h + 1) * HEAD_DIM) for h in range(RET_HEADS)]

    @pl.when(i < N_PT)
    def _():
        qi = lax.broadcasted_iota(jnp.int32, (SEQ, SEQ), 0).astype(F32)
        kj = lax.broadcasted_iota(jnp.int32, (SEQ, SEQ), 1).astype(F32)
        pos = lax.broadcasted_iota(jnp.int32, (SEQ, HEAD_DIM), 0).astype(F32)
        masks = [_decay_mask(qi, kj, lg1[0][h], lg1[1][h]) for h in range(RET_HEADS)]
        kdec_f = [jnp.exp((SEQ - 1.0 - pos) * lg[0][h]) for h in range(RET_HEADS)]
        kdec_b = [jnp.exp(pos * lg[1][h]) for h in range(RET_HEADS)]

        def seq_body(sq, carry):
            rows = pl.ds(pl.multiple_of(sq * SEQ, SEQ), SEQ)
            for h in range(RET_HEADS):
                q = q_ref[rows, hsl[h]].astype(BF16)
                k = k_ref[rows, hsl[h]] * scale
                v = v_ref[rows, hsl[h]].astype(BF16)
                a = (_dot_nt(q, k.astype(BF16)) * masks[h]).astype(BF16)
                acc[rows, hsl[h]] = _dot(a, v)
                sf_ref[sq, 0, h] = _dot((k * kdec_f[h]).T.astype(BF16), v)
                sf_ref[sq, 1, h] = _dot((k * kdec_b[h]).T.astype(BF16), v)
            return carry

        lax.fori_loop(0, SEQ_PER_TILE, seq_body, 0)

    @pl.when(i >= N_PT)
    def _():
        for h in range(RET_HEADS):
            qr_scr[:, hsl[h]] = _rope(q_ref[:, hsl[h]], cos_ref[...], sin_ref[...]).astype(BF16)
            kr_scr[:, hsl[h]] = _rope(k_ref[:, hsl[h]] * scale, cos_ref[...], sin_ref[...]).astype(BF16)
        vb_scr[...] = v_ref[...].astype(BF16)

        def q_body(qb, carry):
            row0 = qb * ATT_TQ
            rows = pl.ds(pl.multiple_of(row0, ATT_TQ), ATT_TQ)
            qi = (row0 + lax.broadcasted_iota(jnp.int32, (ATT_TQ, DEC_SEQ), 0)).astype(F32)
            kj = lax.broadcasted_iota(jnp.int32, (ATT_TQ, DEC_SEQ), 1).astype(F32)
            pos = (row0 + lax.broadcasted_iota(jnp.int32, (ATT_TQ, HEAD_DIM), 0)).astype(F32)
            for h in range(RET_HEADS):
                q = qr_scr[rows, hsl[h]]
                a = (_dot_nt(q, kr_scr[:, hsl[h]]) * _decay_mask(qi, kj, lg1[0][h], lg1[1][h])).astype(BF16)
                o = _dot(a, vb_scr[:, hsl[h]])
                o += _dot(q, s0_ref[0, h].astype(BF16)) * jnp.exp((pos + 1.0) * lg[0][h])
                o += _dot(q, s0_ref[1, h].astype(BF16)) * jnp.exp((DEC_SEQ - pos) * lg[1][h])
                acc[rows, hsl[h]] = o
            return carry

        lax.fori_loop(0, DEC_SEQ // ATT_TQ, q_body, 0)

    for h in range(RET_HEADS):
        o = acc[:, hsl[h]]
        o = o - jnp.mean(o, axis=-1, keepdims=True)
        o = o * lax.rsqrt(jnp.mean(o * o, axis=-1, keepdims=True) + EPS)
        o_ref[:, hsl[h]] = (o * _silu(g_ref[:, hsl[h]])).astype(BF16)


def _retention(p, dl, cos_t, sin_t, state_ret, l):
    tab = pl.BlockSpec((DEC_SEQ, HEAD_DIM), lambda i: (0, 0))
    st = (2, RET_HEADS, HEAD_DIM, HEAD_DIM)
    return pl.pallas_call(
        _retention_kernel,
        grid=(N_MT,),
        in_specs=[pl.BlockSpec((TM, RET_W), lambda i, k=k: (i, COL_R + k)) for k in range(4)] + [
            pl.BlockSpec((None, 2, RET_HEADS, 8, HEAD_DIM), lambda i: (l, 0, 0, 0, 0)),
            tab, tab,
            pl.BlockSpec((None, None) + st, lambda i: (_sample_idx(i), l, 0, 0, 0, 0)),
        ],
        out_specs=[
            pl.BlockSpec((TM, RET_W), lambda i: (i, 0)),
            pl.BlockSpec((SEQ_PER_TILE,) + st, lambda i: (_prompt_idx(i), 0, 0, 0, 0)),
        ],
        out_shape=[
            jax.ShapeDtypeStruct((N_TOK, RET_W), BF16),
            jax.ShapeDtypeStruct((BATCH,) + st, F32),
        ],
        scratch_shapes=[pltpu.VMEM((TM, RET_W), F32)] + [pltpu.VMEM((TM, RET_W), BF16)] * 3,
        compiler_params=_cparams(("arbitrary",)),
        name="retention",
    )(p, p, p, p, dl, cos_t, sin_t, state_ret)


def _s5_disc_kernel(are_ref, aim_ref, ldt_ref, bre_ref, bim_ref, cim_ref,
                    lre_ref, lim_ref, bbre_ref, bbim_ref, ncim_ref):
    ar = are_ref[...]
    ai = aim_ref[...]
    dt = jnp.exp(ldt_ref[...])
    mag = jnp.exp(ar * dt)
    lr = mag * jnp.cos(ai * dt)
    li = mag * jnp.sin(ai * dt)
    den = ar * ar + ai * ai
    nr = lr - 1.0
    cr = (nr * ar + li * ai) / den
    ci = (li * ar - nr * ai) / den
    br = bre_ref[...]
    bi = bim_ref[...]
    lre_ref[...] = lr
    lim_ref[...] = li
    bbre_ref[...] = cr * br - ci * bi
    bbim_ref[...] = cr * bi + ci * br
    ncim_ref[...] = -cim_ref[...]


def _s5_discretize(a_re, a_im, log_dt, b_re, b_im, c_im):
    rows = DEPTH * 2 * SSM_GROUPS
    cols = SSM_STATE * SSM_GROUP
    shp = (DEPTH, 2, SSM_GROUPS, SSM_STATE, SSM_GROUP)
    args = [
        jnp.broadcast_to(a_re[..., None], shp).reshape(rows, cols),
        jnp.broadcast_to(a_im[..., None], shp).reshape(rows, cols),
        jnp.broadcast_to(log_dt[..., None, None], shp).reshape(rows, cols),
        b_re.reshape(rows, cols), b_im.reshape(rows, cols), c_im.reshape(rows, cols),
    ]
    spec = pl.BlockSpec((rows, cols), lambda: (0, 0))
    outs = pl.pallas_call(
        _s5_disc_kernel,
        in_specs=[spec] * 6,
        out_specs=[spec] * 5,
        out_shape=[jax.ShapeDtypeStruct((rows, cols), F32)] * 5,
        name="s5_discretize",
    )(*args)
    lre, lim, bbre, bbim, ncim = outs
    lre = lre.reshape(shp)[..., 0]
    lim = lim.reshape(shp)[..., 0]
    return (lre, lim, bbre.reshape(shp), bbim.reshape(shp),
            ncim.reshape(DEPTH, 2, SSM_GROUPS, SSM_GROUP, SSM_STATE))


def _state_cols(x):
    lead = x.shape[:-3]
    x = x.reshape(lead + (N_STILE, 2, SSM_STATE, 2))
    x = jnp.moveaxis(x, -1, -3)
    return x.reshape(lead + (STATE_W,))


def _state_uncols(x):
    lead = x.shape[:-1]
    x = x.reshape(lead + (N_STILE, 2, 2, SSM_STATE))
    x = jnp.moveaxis(x, -3, -1)
    return x.reshape(lead + (SSM_GROUPS, SSM_STATE, 2))


def _s5_compact(lre, lim, bbre, bbim, c_re, ncim):
    n = DEPTH * 2
    bb = jnp.stack([bbre, bbim], axis=-1)
    bb = bb.reshape(n, N_STILE, 2, SSM_STATE, SSM_GROUP, 2)
    wb = jnp.transpose(bb, (0, 4, 1, 5, 2, 3)).reshape(DEPTH, 2, SSM_GROUP, STATE_W)
    cc = jnp.stack([c_re, ncim], axis=-1)
    cc = cc.reshape(n, N_STILE, 2, SSM_GROUP, SSM_STATE, 2)
    wc = jnp.transpose(cc, (0, 3, 1, 5, 2, 4)).reshape(DEPTH, 2, SSM_GROUP, STATE_W)
    lam = jnp.concatenate([lre.reshape(DEPTH, 2, N_STILE, 128), lim.reshape(DEPTH, 2, N_STILE, 128)], axis=2)
    return wb, wc, lam


def _group_mask(cols):
    row_g = lax.broadcasted_iota(jnp.int32, (256, cols), 0) // SSM_GROUP
    col = lax.broadcasted_iota(jnp.int32, (256, cols), 1)
    col_g = ((col // 256) % 8) * 2 + (col % 128) // SSM_STATE
    return row_g == col_g


def _s5_scan_kernel(u_ref, wb_ref, wc_ref, lam_ref, h0_ref, y_ref, hf_ref,
                    pm_scr, bm_scr, cm_scr, hbuf, y_scr, state, ends, *, segments):
    npass = 1 if segments == 1 else 2
    d = pl.program_id(0)
    ps = pl.program_id(1)
    ck = pl.program_id(2)
    nck = pl.num_programs(2)
    rows = SSM_TC * SSM_ROWS
    half = STATE_W // 2
    lam = lam_ref[...]

    @pl.when(jnp.logical_and(ps == 0, ck == 0))
    def _():
        a = lax.broadcasted_iota(jnp.int32, (rows, rows), 0)
        b = lax.broadcasted_iota(jnp.int32, (rows, rows), 1)
        same = jnp.logical_and(a // SSM_ROWS == b % SSM_TC, a % SSM_ROWS == b // SSM_TC)
        pm_scr[...] = jnp.where(same, 1.0, 0.0).astype(BF16)
        wb = jnp.tile(wb_ref[...], (256 // SSM_GROUP, 1))
        bm_scr[...] = jnp.where(_group_mask(STATE_W), wb, 0.0).astype(BF16)
        for n_ in range(2):
            wc = jnp.tile(wc_ref[:, n_ * half:(n_ + 1) * half], (256 // SSM_GROUP, 1))
            cm_scr[n_] = jnp.where(_group_mask(half), wc, 0.0).astype(BF16)

    @pl.when(ck == 0)
    def _():
        if npass == 1:
            state[...] = h0_ref[0]
        else:
            @pl.when(ps == 0)
            def _():
                state[...] = jnp.zeros_like(state)

            @pl.when(ps == 1)
            def _():
                ends[...] = state[...]
                state[...] = h0_ref[0]
                lr = lam[0:N_STILE]
                li = lam[N_STILE:]
                for _ in range(int(math.log2(SSM_SEG_LEN))):
                    lr, li = lr * lr - li * li, 2.0 * lr * li

                def carry(order, prev):
                    for sg in order:
                        for sq in range(SSM_ROWS // segments):
                            r = sq * segments + sg
                            q = r + prev
                            for j in range(N_STILE):
                                re = slice(j * 256, j * 256 + 128)
                                im = slice(j * 256 + 128, (j + 1) * 256)
                                pr = state[q:q + 1, re]
                                pi = state[q:q + 1, im]
                                ar = lr[j:j + 1]
                                ai = li[j:j + 1]
                                state[r:r + 1, re] = ends[q:q + 1, re] + ar * pr - ai * pi
                                state[r:r + 1, im] = ends[q:q + 1, im] + ar * pi + ai * pr

                @pl.when(d == 0)
                def _():
                    carry(range(1, segments), -1)

                @pl.when(d == 1)
                def _():
                    carry(range(segments - 2, -1, -1), 1)

    u = _dot(pm_scr[...], u_ref[...].reshape(rows, SSM_W).astype(BF16)).astype(BF16)
    for j in range(N_STILE):
        k0 = (j // 8) * 256
        hbuf[:, j * 256:(j + 1) * 256] = _dot(u[:, k0:k0 + 256], bm_scr[:, j * 256:(j + 1) * 256])

    JT = 4
    for jb in range(N_STILE // JT):
        tiles = list(range(jb * JT, (jb + 1) * JT))
        lrs = [jnp.broadcast_to(lam[j:j + 1], (SSM_ROWS, 128)) for j in tiles]
        lis = [jnp.broadcast_to(lam[N_STILE + j:N_STILE + j + 1], (SSM_ROWS, 128)) for j in tiles]
        init = tuple(state[:, j * 256:j * 256 + 128] for j in tiles) + \
            tuple(state[:, j * 256 + 128:(j + 1) * 256] for j in tiles)

        def body(t, carry_, tiles=tiles, lrs=lrs, lis=lis):
            tt = t + d * (SSM_TC - 1 - 2 * t)
            r = pl.ds(pl.multiple_of(tt * SSM_ROWS, SSM_ROWS), SSM_ROWS)
            out_r, out_i = [], []
            for n_, j in enumerate(tiles):
                hr, hi = carry_[n_], carry_[JT + n_]
                re = slice(j * 256, j * 256 + 128)
                im = slice(j * 256 + 128, (j + 1) * 256)
                nr = lrs[n_] * hr - lis[n_] * hi + hbuf[r, re]
                ni = lrs[n_] * hi + lis[n_] * hr + hbuf[r, im]
                hbuf[r, re] = nr
                hbuf[r, im] = ni
                out_r.append(nr)
                out_i.append(ni)
            return tuple(out_r) + tuple(out_i)

        fin = lax.fori_loop(0, SSM_TC, body, init, unroll=2)
        for n_, j in enumerate(tiles):
            state[:, j * 256:j * 256 + 128] = fin[n_]
            state[:, j * 256 + 128:(j + 1) * 256] = fin[JT + n_]

    @pl.when(ps == npass - 1)
    def _():
        hb = hbuf[...].astype(BF16)
        for n_ in range(2):
            y = _dot_nt(hb[:, n_ * half:(n_ + 1) * half], cm_scr[n_])
            y_scr[2 * n_] = y[:, :128]
            y_scr[2 * n_ + 1] = y[:, 128:]
        for r in range(SSM_ROWS):
            for c_ in range(SSM_W // 128):
                y_ref[r, :, c_ * 128:(c_ + 1) * 128] = y_scr[c_, pl.ds(r, SSM_TC, stride=SSM_ROWS), :]

    @pl.when(jnp.logical_and(ps == npass - 1, ck == nck - 1))
    def _():
        hf_ref[0] = state[...]


def _s5_scan(p3, row_block, wb, wc, lam, h0, l, *, segments):
    steps = p3.shape[1]
    nck = steps // SSM_TC
    npass = 1 if segments == 1 else 2

    def chunk(d, c):
        return c + d * (nck - 1 - 2 * c)

    def y_chunk(d, p, c):
        return jnp.where(p == npass - 1, chunk(d, c), chunk(d, 0))

    par = pl.BlockSpec((None, None, SSM_GROUP, STATE_W), lambda d, p, c: (l, d, 0, 0))
    st = pl.BlockSpec((1, SSM_ROWS, STATE_W), lambda d, p, c: (d, 0, 0))
    return pl.pallas_call(
        functools.partial(_s5_scan_kernel, segments=segments),
        grid=(2, npass, nck),
        in_specs=[
            pl.BlockSpec((SSM_ROWS, SSM_TC, SSM_W), lambda d, p, c: (row_block, chunk(d, c), COL_U)),
            par, par,
            pl.BlockSpec((None, None, 2 * N_STILE, 128), lambda d, p, c: (l, d, 0, 0)),
            st,
        ],
        out_specs=[
            pl.BlockSpec((None, SSM_ROWS, SSM_TC, SSM_W), lambda d, p, c: (d, 0, y_chunk(d, p, c), 0)),
            st,
        ],
        out_shape=[
            jax.ShapeDtypeStruct((2, SSM_ROWS, steps, SSM_W), F32),
            jax.ShapeDtypeStruct((2, SSM_ROWS, STATE_W), F32),
        ],
        scratch_shapes=[
            pltpu.VMEM((SSM_TC * SSM_ROWS, SSM_TC * SSM_ROWS), BF16),
            pltpu.VMEM((256, STATE_W), BF16),
            pltpu.VMEM((2, 256, STATE_W // 2), BF16),
            pltpu.VMEM((SSM_TC * SSM_ROWS, STATE_W), F32),
            pltpu.VMEM((SSM_W // 128, SSM_TC * SSM_ROWS, 128), F32),
            pltpu.VMEM((SSM_ROWS, STATE_W), F32),
            pltpu.VMEM((SSM_ROWS, STATE_W), F32),
        ],
        compiler_params=_cparams(("arbitrary", "arbitrary", "arbitrary")),
        name="s5_scan_seg%d" % segments,
    )(p3, wb, wc, lam, h0)


def _s5_glu_kernel(ypf_ref, ypb_ref, ysf_ref, ysb_ref, u_ref, d_ref, w_ref, o_ref):
    i = pl.program_id(0)
    w = w_ref[...].astype(BF16)

    def glu(y):
        y = y + d_ref[...] * u_ref[...]
        z = _dot(jax.nn.gelu(y).astype(BF16), w)
        o_ref[...] = (z[:, :SSM_W] * _sigmoid(z[:, SSM_W:])).astype(BF16)

    @pl.when(i < N_PT)
    def _():
        glu(ypf_ref[...] + ypb_ref[...])

    @pl.when(i >= N_PT)
    def _():
        glu(ysf_ref[...] + ysb_ref[...])


def _s5_glu(yp, ys, p, d, w, l):
    return pl.pallas_call(
        _s5_glu_kernel,
        grid=(N_MT,),
        in_specs=[
            pl.BlockSpec((None, TM, SSM_W), lambda i: (0, _prompt_idx(i), 0)),
            pl.BlockSpec((None, TM, SSM_W), lambda i: (1, _prompt_idx(i), 0)),
            pl.BlockSpec((None, TM, SSM_W), lambda i: (0, _sample_idx(i), 0)),
            pl.BlockSpec((None, TM, SSM_W), lambda i: (1, _sample_idx(i), 0)),
            pl.BlockSpec((TM, SSM_W), lambda i: (i, COL_U)),
            pl.BlockSpec((None, 1, SSM_W), lambda i: (l, 0, 0)),
            pl.BlockSpec((None, SSM_W, 2 * SSM_W), lambda i: (l, 0, 0)),
        ],
        out_specs=pl.BlockSpec((TM, SSM_W), lambda i: (i, 0)),
        out_shape=jax.ShapeDtypeStruct((N_TOK, SSM_W), BF16),
        compiler_params=_cparams(("arbitrary",)),
        name="s5_glu",
    )(yp, yp, ys, ys, p, d.reshape(DEPTH, 1, SSM_W), w)


def _s5_layer(p, s5p, st_l, ssm_d, w_ssm_glu, l):
    wb, wc, lam = s5p
    yp, hfin = _s5_scan(p.reshape(N_TOK // SEQ, SEQ, IN_W), 0, wb, wc, lam,
                        jnp.zeros((2, SSM_ROWS, STATE_W), F32), l, segments=1)
    h0 = _state_cols(jnp.moveaxis(st_l, 0, 1))
    h0 = jnp.stack([
        jnp.zeros((DEC_BATCH, SSM_SEGS, STATE_W), F32).at[:, 0].set(h0[0]),
        jnp.zeros((DEC_BATCH, SSM_SEGS, STATE_W), F32).at[:, SSM_SEGS - 1].set(h0[1]),
    ]).reshape(2, SSM_ROWS, STATE_W)
    ys, _ = _s5_scan(p.reshape(N_TOK // SSM_SEG_LEN, SSM_SEG_LEN, IN_W), N_PROMPT // SSM_SEG_LEN // SSM_ROWS,
                     wb, wc, lam, h0, l, segments=SSM_SEGS)
    out = _s5_glu(yp.reshape(2, N_PROMPT, SSM_W), ys.reshape(2, N_SAMPLE, SSM_W), p, ssm_d, w_ssm_glu, l)
    new_state = jnp.moveaxis(_state_uncols(hfin), 0, 1)
    return out, new_state


def _rope_tables():
    rows = DEC_SEQ // GRID_W
    r = jnp.repeat(jnp.arange(rows), GRID_W).astype(F32)
    col = jnp.tile(jnp.arange(GRID_W), rows).astype(F32)
    inv = ROPE_BASE ** (-jnp.arange(ROPE_FREQS, dtype=F32) / ROPE_FREQS)
    ang = jnp.stack([r, col], axis=-1)[:, :, None] * inv
    cos = jnp.cos(ang)
    sin = jnp.sin(ang)
    cos_t = jnp.concatenate([cos, cos], axis=-1).reshape(DEC_SEQ, HEAD_DIM)
    sin_t = jnp.concatenate([-sin, sin], axis=-1).reshape(DEC_SEQ, HEAD_DIM)
    return cos_t, sin_t


def kernel(x_prompt, x_sample, cache_k, cache_v, state_ssm, state_ret, c, c_ctx, w_mod, b_mod, norm_g,
           w_ffn_in, w_ffn_out, w_in, w_out, q_norm_g, k_norm_g, ssm_a_re, ssm_a_im, ssm_log_dt,
           ssm_b_re, ssm_b_im, ssm_c_re, ssm_c_im, ssm_d, w_ssm_glu, ret_decay_logit, final_norm_g):
    x = (x_prompt.reshape(N_PROMPT, D_MODEL), x_sample.reshape(N_SAMPLE, D_MODEL))
    cond8 = jnp.concatenate([c_ctx[None], c, jnp.zeros((5, D_MODEL), F32)], axis=0)
    mods = _adaln(cond8, w_mod, b_mod).reshape(DEPTH, 8, N_SUB * 3, D_MODEL)
    modt = jnp.transpose(mods[:, jnp.array(TILE_MOD, jnp.int32)], (0, 2, 1, 3))[:, :, :, None, :]
    gains = norm_g.reshape(DEPTH * N_SUB, 1, D_MODEL)
    cos_t, sin_t = _rope_tables()
    lre, lim, bbre, bbim, ncim = _s5_discretize(ssm_a_re, ssm_a_im, ssm_log_dt, ssm_b_re, ssm_b_im, ssm_c_im)
    s5p = _s5_compact(lre, lim, bbre, bbim, ssm_c_re, ncim)
    dl = jnp.broadcast_to(ret_decay_logit[:, :, :, None, None], (DEPTH, 2, RET_HEADS, 8, HEAD_DIM))
    ks_, vs_, hs_, ss_ = [], [], [], []
    for l in range(DEPTH):
        x = _ffn(x, gains, modt, w_ffn_in, w_ffn_out, l, 0)
        p = _inproj(x, gains, modt, w_in, l)
        attn, k_l, v_l = _attention(p, cache_k, cache_v, q_norm_g, k_norm_g, cos_t, sin_t, l)
        ssm, h_l = _s5_layer(p, s5p, state_ssm[:, l], ssm_d, w_ssm_glu, l)
        ret, s_l = _retention(p, dl, cos_t, sin_t, state_ret, l)
        x = _outproj(x, attn, ssm, ret, modt, w_out, l)
        x = _ffn(x, gains, modt, w_ffn_in, w_ffn_out, l, 1, final_g=final_norm_g if l == DEPTH - 1 else None)
        ks_.append(k_l.reshape(BATCH, SEQ, N_KV_HEADS, HEAD_DIM))
        vs_.append(v_l.reshape(BATCH, SEQ, N_KV_HEADS, HEAD_DIM))
        hs_.append(h_l)
        ss_.append(s_l)
    y_prompt = x[0].reshape(BATCH, SEQ, D_MODEL)
    y_sample = x[1].reshape(DEC_BATCH, DEC_SEQ, D_MODEL)
    return (y_prompt, y_sample, jnp.stack(ks_, axis=1), jnp.stack(vs_, axis=1),
            jnp.stack(hs_, axis=1), jnp.stack(ss_, axis=1))
```

```python
import functools
import math

import jax
import jax.numpy as jnp
from jax import lax
from jax.experimental import pallas as pl
from jax.experimental.pallas import tpu as pltpu

D_MODEL = 2048
BATCH = 16
SEQ = 256
DEPTH = 2
DEC_BATCH = 2
DEC_SEQ = 1024
PAST_LEN = 512
GRID_W = 64
HEAD_DIM = 128
N_Q_HEADS = 8
N_KV_HEADS = 2
Q_PER_KV = 4
ATTN_W = 1024
KV_W = 256
SSM_W = 512
SSM_GROUP = 16
SSM_GROUPS = 32
SSM_STATE = 64
RET_HEADS = 4
RET_W = 512
IN_W = 4096
D_FF = 5632
N_SUB = 3
RET_CHUNK = 128
ROPE_BASE = 10000.0
ROPE_FREQS = 32
EPS = 1e-6

N_PROMPT = BATCH * SEQ
N_SAMPLE = DEC_BATCH * DEC_SEQ
N_TOK = N_PROMPT + N_SAMPLE
TM = 1024
N_MT = N_TOK // TM
N_PT = N_PROMPT // TM
SEQ_PER_TILE = TM // SEQ
TILE_MOD = (0, 0, 0, 0, 1, 2)
ROW_CHUNK = 256
TF = 256
FFN_NSPLIT = 4
TN = 512
STATE_W = 2 * SSM_GROUPS * SSM_STATE
N_STILE = STATE_W // 256
SSM_ROWS = 16
SSM_SEGS = 8
SSM_SEG_LEN = DEC_SEQ // SSM_SEGS
SSM_TC = 64
ATT_TQ = 256
ATT_S = PAST_LEN + DEC_SEQ
VMEM_LIMIT = 56 * 1024 * 1024
VMEM_LIMIT_FFN = 60 * 1024 * 1024

COL_K = ATTN_W // KV_W
COL_V = COL_K + 1
COL_U = (ATTN_W + 2 * KV_W) // SSM_W
COL_R = (ATTN_W + 2 * KV_W + SSM_W) // RET_W

BF16 = jnp.bfloat16
F32 = jnp.float32


def _cparams(sem, limit=VMEM_LIMIT):
    return pltpu.CompilerParams(dimension_semantics=sem, vmem_limit_bytes=limit)


def _dot(a, b):
    return jnp.dot(a, b, preferred_element_type=F32)


def _dot_nt(a, b):
    return lax.dot_general(a, b, (((1,), (1,)), ((), ())), preferred_element_type=F32)


def _sigmoid(x):
    return 1.0 / (1.0 + jnp.exp(-x))


def _silu(x):
    return x * _sigmoid(x)


def _sample_idx(i):
    return jnp.clip(i - N_PT, 0, DEC_BATCH - 1)


def _prompt_idx(i):
    return jnp.minimum(i, N_PT - 1)


def _row_chunks(n_rows, body):
    def step(c, carry):
        body(pl.ds(pl.multiple_of(c * ROW_CHUNK, ROW_CHUNK), ROW_CHUNK))
        return carry

    lax.fori_loop(0, n_rows // ROW_CHUNK, step, 0)


def _mod_spec(l, k, width=D_MODEL, col=lambda j: 0, tile=lambda i: i):
    return pl.BlockSpec((None, None, 1, 1, width), lambda i, j: (l, k, tile(i), 0, col(j)))


def _gain_spec(l, sub):
    return pl.BlockSpec((None, 1, D_MODEL), lambda i, j: (l * N_SUB + sub, 0, 0))


def _adaln_kernel(c_ref, w_ref, b_ref, o_ref):
    a = _silu(c_ref[...]).astype(BF16)
    o_ref[0] = _dot(a, w_ref[0].astype(BF16)) + b_ref[0]


def _adaln(cond8, w_mod, b_mod):
    n = w_mod.shape[-1]
    tn = 1024
    return pl.pallas_call(
        _adaln_kernel,
        grid=(DEPTH, n // tn),
        in_specs=[
            pl.BlockSpec((8, D_MODEL), lambda l, j: (0, 0)),
            pl.BlockSpec((1, D_MODEL, tn), lambda l, j: (l, 0, j)),
            pl.BlockSpec((1, 1, tn), lambda l, j: (l, 0, j)),
        ],
        out_specs=pl.BlockSpec((1, 8, tn), lambda l, j: (l, 0, j)),
        out_shape=jax.ShapeDtypeStruct((DEPTH, 8, n), F32),
        compiler_params=_cparams(("arbitrary", "arbitrary")),
        name="adaln",
    )(cond8, w_mod, b_mod.reshape(DEPTH, 1, n))


def _norm_mod(x, g, shift, scale):
    ms = jnp.mean(x * x, axis=-1, keepdims=True)
    return x * lax.rsqrt(ms + EPS) * (g * (1.0 + scale)) + shift


def _ffn_kernel(*refs, n_x, n_out, final):
    refs = list(refs)
    x_refs = refs[:n_x]
    g_ref, sh_ref, sc_ref, gt_ref, wa_ref, wu_ref, wo_ref = refs[n_x:n_x + 7]
    rest = refs[n_x + 7:]
    fg_ref = rest.pop(0) if final else None
    o_refs, h_scr = rest[:n_out], rest[n_out]
    i = pl.program_id(0)
    j = pl.program_id(1)

    def run(x_ref, o_ref):
        @pl.when(j == 0)
        def _():
            def pre(rows):
                h = _norm_mod(x_ref[rows, :], g_ref[...], sh_ref[0], sc_ref[0])
                h_scr[rows, :] = h.astype(BF16)

            _row_chunks(TM, pre)
            o_ref[...] = jnp.zeros_like(o_ref)

        h = h_scr[...]
        a = _dot(h, wa_ref[...].astype(BF16))
        u = _dot(h, wu_ref[...].astype(BF16))
        mid = (_silu(a) * u).astype(BF16)
        wn = D_MODEL // FFN_NSPLIT
        for n in range(FFN_NSPLIT):
            cs = slice(n * wn, (n + 1) * wn)
            o_ref[:, cs] += _dot(mid, wo_ref[:, cs].astype(BF16))

        @pl.when(j == pl.num_programs(1) - 1)
        def _():
            def post(rows):
                out = x_ref[rows, :] + (0.5 * gt_ref[0]) * o_ref[rows, :]
                if final:
                    ms = jnp.mean(out * out, axis=-1, keepdims=True)
                    out = out * lax.rsqrt(ms + EPS) * fg_ref[...]
                o_ref[rows, :] = out

            _row_chunks(TM, post)

    if n_x == 1 and n_out == 1:
        run(x_refs[0], o_refs[0])
    else:
        @pl.when(i < N_PT)
        def _():
            run(x_refs[0], o_refs[0])

        @pl.when(i >= N_PT)
        def _():
            run(x_refs[-1], o_refs[-1])


def _split_tok_specs():
    return [pl.BlockSpec((TM, D_MODEL), lambda i, j: (_prompt_idx(i), 0)),
            pl.BlockSpec((TM, D_MODEL), lambda i, j: (_sample_idx(i), 0), pipeline_mode=pl.Buffered(1))]


def _ffn(x, gains, modt, w_in, w_out, l, f, final_g=None):
    final = final_g is not None
    xs = list(x) if isinstance(x, (tuple, list)) else [x]
    sub = 2 * f
    nf = D_FF // TF
    tok = pl.BlockSpec((TM, D_MODEL), lambda i, j: (i, 0))
    in_specs = (_split_tok_specs() if len(xs) == 2 else [tok]) + [
        _gain_spec(l, sub), _mod_spec(l, 3 * sub), _mod_spec(l, 3 * sub + 1), _mod_spec(l, 3 * sub + 2),
        pl.BlockSpec((None, None, D_MODEL, TF), lambda i, j: (l, f, 0, j)),
        pl.BlockSpec((None, None, D_MODEL, TF), lambda i, j: (l, f, 0, j + nf)),
        pl.BlockSpec((None, None, TF, D_MODEL), lambda i, j: (l, f, j, 0)),
    ]
    args = xs + [gains, modt, modt, modt, w_in, w_in, w_out]
    if final:
        in_specs.append(pl.BlockSpec((1, D_MODEL), lambda i, j: (0, 0)))
        args.append(final_g.reshape(1, D_MODEL))
        out_specs = _split_tok_specs()
        out_shape = [jax.ShapeDtypeStruct((N_PROMPT, D_MODEL), F32), jax.ShapeDtypeStruct((N_SAMPLE, D_MODEL), F32)]
    else:
        out_specs = tok
        out_shape = jax.ShapeDtypeStruct((N_TOK, D_MODEL), F32)
    return pl.pallas_call(
        functools.partial(_ffn_kernel, n_x=len(xs), n_out=2 if final else 1, final=final),
        grid=(N_MT, nf),
        in_specs=in_specs,
        out_specs=out_specs,
        out_shape=out_shape,
        scratch_shapes=[pltpu.VMEM((TM, D_MODEL), BF16)],
        compiler_params=_cparams(("arbitrary", "arbitrary"), VMEM_LIMIT_FFN),
        name="ffn_final" if final else ("ffn_first" if len(xs) == 2 else "ffn"),
    )(*args)


def _resident_weight_spec(rows, l, ncol):
    return pl.BlockSpec((None, rows, TN), lambda i, j: (l, 0, jnp.where(i == 0, j, ncol - 1)))


def _inproj_kernel(x_ref, g_ref, sh_ref, sc_ref, w_ref, o_ref, h_scr, w_scr):
    i = pl.program_id(0)
    j = pl.program_id(1)

    @pl.when(j == 0)
    def _():
        def pre(rows):
            h = _norm_mod(x_ref[rows, :], g_ref[...], sh_ref[0], sc_ref[0])
            h_scr[rows, :] = h.astype(BF16)

        _row_chunks(TM, pre)

    @pl.when(i == 0)
    def _():
        w_scr[j] = w_ref[...].astype(BF16)

    o_ref[...] = _dot(h_scr[...], w_scr[j])


def _inproj(x, gains, modt, w, l):
    ncol = IN_W // TN
    return pl.pallas_call(
        _inproj_kernel,
        grid=(N_MT, ncol),
        in_specs=[
            pl.BlockSpec((TM, D_MODEL), lambda i, j: (i, 0)),
            _gain_spec(l, 1), _mod_spec(l, 3), _mod_spec(l, 4),
            _resident_weight_spec(D_MODEL, l, ncol),
        ],
        out_specs=pl.BlockSpec((TM, TN), lambda i, j: (i, j)),
        out_shape=jax.ShapeDtypeStruct((N_TOK, IN_W), F32),
        scratch_shapes=[pltpu.VMEM((TM, D_MODEL), BF16), pltpu.VMEM((ncol, D_MODEL, TN), BF16)],
        compiler_params=_cparams(("arbitrary", "arbitrary")),
        name="inproj",
    )(x, gains, modt, modt, w)


def _outproj_kernel(x_ref, a_ref, s_ref, r_ref, gt_ref, w_ref, o_ref, w_scr):
    i = pl.program_id(0)
    j = pl.program_id(1)

    @pl.when(i == 0)
    def _():
        w_scr[j] = w_ref[...].astype(BF16)

    y = _dot(a_ref[...], w_scr[j, 0:ATTN_W, :])
    y += _dot(s_ref[...], w_scr[j, ATTN_W:ATTN_W + SSM_W, :])
    y += _dot(r_ref[...], w_scr[j, ATTN_W + SSM_W:, :])
    o_ref[...] = x_ref[...] + gt_ref[0] * y


def _outproj(x, attn, ssm, ret, modt, w, l):
    ncol = D_MODEL // TN
    return pl.pallas_call(
        _outproj_kernel,
        grid=(N_MT, ncol),
        in_specs=[
            pl.BlockSpec((TM, TN), lambda i, j: (i, j)),
            pl.BlockSpec((TM, ATTN_W), lambda i, j: (i, 0)),
            pl.BlockSpec((TM, SSM_W), lambda i, j: (i, 0)),
            pl.BlockSpec((TM, RET_W), lambda i, j: (i, 0)),
            _mod_spec(l, 5, TN, lambda j: j),
            _resident_weight_spec(D_MODEL, l, ncol),
        ],
        out_specs=pl.BlockSpec((TM, TN), lambda i, j: (i, j)),
        out_shape=jax.ShapeDtypeStruct((N_TOK, D_MODEL), F32),
        scratch_shapes=[pltpu.VMEM((ncol, D_MODEL, TN), BF16)],
        compiler_params=_cparams(("arbitrary", "arbitrary")),
        name="outproj",
    )(x, attn, ssm, ret, modt, w)


def _head_rms(x, g):
    ms = jnp.mean(x * x, axis=-1, keepdims=True)
    return x * lax.rsqrt(ms + EPS) * g


def _rope(x, cos, sin_signed):
    lane = lax.broadcasted_iota(jnp.int32, x.shape, 1)
    partner = jnp.where((lane % 64) < 32, pltpu.roll(x, 96, 1), pltpu.roll(x, 32, 1))
    return x * cos + partner * sin_signed


SOFTMAX_EXP2_SCALE = HEAD_DIM ** -0.5 * math.log2(math.e)


def _attend_short(qs, k, v):
    s = _dot_nt(qs.astype(BF16), k)
    p = jnp.exp2((s - jnp.max(s, axis=-1, keepdims=True)) * SOFTMAX_EXP2_SCALE)
    l = jnp.sum(p, axis=-1, keepdims=True)
    return _dot(p.astype(BF16), v) / l


def _attend(qs, k, vt):
    st = _dot_nt(k, qs.astype(BF16))
    p = jnp.exp2((st - jnp.max(st, axis=0, keepdims=True)) * SOFTMAX_EXP2_SCALE)
    l = jnp.sum(p, axis=0, keepdims=True)
    return (_dot(vt, p.astype(BF16)) / l).T


def _attn_kernel(q_ref, k_ref, v_ref, ck_ref, cv_ref, qg_ref, kg_ref, cos_ref, sin_ref,
                 o_ref, kc_ref, vc_ref, k_scr, vt_scr):
    i = pl.program_id(0)
    qg = qg_ref[...]
    kg = kg_ref[...]

    def heads(kv):
        return [slice((kv * Q_PER_KV + g) * HEAD_DIM, (kv * Q_PER_KV + g + 1) * HEAD_DIM) for g in range(Q_PER_KV)]

    @pl.when(i < N_PT)
    def _():
        def seq_body(sq, carry):
            rows = pl.ds(pl.multiple_of(sq * SEQ, SEQ), SEQ)
            v = v_ref[rows, :]
            vc_ref[sq] = v
            for kv in range(N_KV_HEADS):
                ksl = slice(kv * HEAD_DIM, (kv + 1) * HEAD_DIM)
                kn = _head_rms(k_ref[rows, ksl], kg)
                kc_ref[sq, :, ksl] = kn
                qs = jnp.concatenate([_head_rms(q_ref[rows, hs], qg) for hs in heads(kv)], axis=0)
                o = _attend_short(qs, kn.astype(BF16), v[:, ksl].astype(BF16))
                for g, hs in enumerate(heads(kv)):
                    o_ref[rows, hs] = o[g * SEQ:(g + 1) * SEQ].astype(BF16)
            return carry

        lax.fori_loop(0, SEQ_PER_TILE, seq_body, 0)

    @pl.when(i >= N_PT)
    def _():
        k_scr[0:PAST_LEN, :] = ck_ref[...].astype(BF16)
        for kv in range(N_KV_HEADS):
            ksl = slice(kv * HEAD_DIM, (kv + 1) * HEAD_DIM)
            vt_scr[kv, :, 0:PAST_LEN] = cv_ref[:, ksl].T.astype(BF16)
            vt_scr[kv, :, PAST_LEN:] = v_ref[:, ksl].T.astype(BF16)
            kn = _rope(_head_rms(k_ref[:, ksl], kg), cos_ref[...], sin_ref[...])
            k_scr[PAST_LEN:, ksl] = kn.astype(BF16)

        def q_body(qb, carry):
            rows = pl.ds(pl.multiple_of(qb * ATT_TQ, ATT_TQ), ATT_TQ)
            cos = cos_ref[rows, :]
            sin = sin_ref[rows, :]
            for kv in range(N_KV_HEADS):
                ksl = slice(kv * HEAD_DIM, (kv + 1) * HEAD_DIM)
                qs = jnp.concatenate([_rope(_head_rms(q_ref[rows, hs], qg), cos, sin) for hs in heads(kv)], axis=0)
                o = _attend(qs, k_scr[:, ksl], vt_scr[kv])
                for g, hs in enumerate(heads(kv)):
                    o_ref[rows, hs] = o[g * ATT_TQ:(g + 1) * ATT_TQ].astype(BF16)
            return carry

        lax.fori_loop(0, DEC_SEQ // ATT_TQ, q_body, 0)


def _attention(p, cache_k, cache_v, qg, kg, cos_t, sin_t, l):
    cache_spec = pl.BlockSpec((None, None, PAST_LEN, KV_W), lambda i: (_sample_idx(i), l, 0, 0))
    vec = pl.BlockSpec((None, 1, HEAD_DIM), lambda i: (l, 0, 0))
    tab = pl.BlockSpec((DEC_SEQ, HEAD_DIM), lambda i: (0, 0))
    new_cache = pl.BlockSpec((SEQ_PER_TILE, SEQ, KV_W), lambda i: (_prompt_idx(i), 0, 0))
    return pl.pallas_call(
        _attn_kernel,
        grid=(N_MT,),
        in_specs=[
            pl.BlockSpec((TM, ATTN_W), lambda i: (i, 0)),
            pl.BlockSpec((TM, KV_W), lambda i: (i, COL_K)),
            pl.BlockSpec((TM, KV_W), lambda i: (i, COL_V)),
            cache_spec, cache_spec, vec, vec, tab, tab,
        ],
        out_specs=[pl.BlockSpec((TM, ATTN_W), lambda i: (i, 0)), new_cache, new_cache],
        out_shape=[
            jax.ShapeDtypeStruct((N_TOK, ATTN_W), BF16),
            jax.ShapeDtypeStruct((BATCH, SEQ, KV_W), F32),
            jax.ShapeDtypeStruct((BATCH, SEQ, KV_W), F32),
        ],
        scratch_shapes=[pltpu.VMEM((ATT_S, KV_W), BF16), pltpu.VMEM((N_KV_HEADS, HEAD_DIM, ATT_S), BF16)],
        compiler_params=_cparams(("arbitrary",)),
        name="attention",
    )(p, p, p, cache_k.reshape(DEC_BATCH, DEPTH, PAST_LEN, KV_W), cache_v.reshape(DEC_BATCH, DEPTH, PAST_LEN, KV_W),
      qg.reshape(DEPTH, 1, HEAD_DIM), kg.reshape(DEPTH, 1, HEAD_DIM), cos_t, sin_t)


def _log_sigmoid(x):
    return -(jnp.maximum(-x, 0.0) + jnp.log(1.0 + jnp.exp(-jnp.abs(x))))


def _decay_mask(qi, kj, lg_f, lg_b):
    dd = qi - kj
    log2e = math.log2(math.e)
    w = jnp.exp2(dd * jnp.where(dd > 0, lg_f * log2e, -lg_b * log2e))
    return jnp.where(dd == 0, 2.0, w)


def _retention_kernel(q_ref, k_ref, v_ref, g_ref, dl_ref, cos_ref, sin_ref, s0_ref, o_ref, sf_ref,
                      acc, qr_scr, kr_scr, vb_scr):
    i = pl.program_id(0)
    scale = HEAD_DIM ** -0.5
    lg = [[_log_sigmoid(dl_ref[d, h][0:1, :]) for h in range(RET_HEADS)] for d in range(2)]
    lg1 = [[lg[d][h][:, 0:1] for h in range(RET_HEADS)] for d in range(2)]
    hsl = [slice(h * HEAD_DIM, (h + 1) * HEAD_DIM) for h in range(RET_HEADS)]

    @pl.when(i < N_PT)
    def _():
        qi = lax.broadcasted_iota(jnp.int32, (SEQ, SEQ), 0).astype(F32)
        kj = lax.broadcasted_iota(jnp.int32, (SEQ, SEQ), 1).astype(F32)
        pos = lax.broadcasted_iota(jnp.int32, (SEQ, HEAD_DIM), 0).astype(F32)
        masks = [_decay_mask(qi, kj, lg1[0][h], lg1[1][h]) for h in range(RET_HEADS)]
        kdec_f = [jnp.exp((SEQ - 1.0 - pos) * lg[0][h]) for h in range(RET_HEADS)]
        kdec_b = [jnp.exp(pos * lg[1][h]) for h in range(RET_HEADS)]

        def seq_body(sq, carry):
            rows = pl.ds(pl.multiple_of(sq * SEQ, SEQ), SEQ)
            for h in range(RET_HEADS):
                q = q_ref[rows, hsl[h]].astype(BF16)
                k = k_ref[rows, hsl[h]] * scale
                v = v_ref[rows, hsl[h]].astype(BF16)
                a = (_dot_nt(q, k.astype(BF16)) * masks[h]).astype(BF16)
                acc[rows, hsl[h]] = _dot(a, v)
                sf_ref[sq, 0, h] = _dot((k * kdec_f[h]).T.astype(BF16), v)
                sf_ref[sq, 1, h] = _dot((k * kdec_b[h]).T.astype(BF16), v)
            return carry

        lax.fori_loop(0, SEQ_PER_TILE, seq_body, 0)

    @pl.when(i >= N_PT)
    def _():
        for h in range(RET_HEADS):
            qr_scr[:, hsl[h]] = _rope(q_ref[:, hsl[h]], cos_ref[...], sin_ref[...]).astype(BF16)
            kr_scr[:, hsl[h]] = _rope(k_ref[:, hsl[h]] * scale, cos_ref[...], sin_ref[...]).astype(BF16)
        vb_scr[...] = v_ref[...].astype(BF16)

        def q_body(qb, carry):
            row0 = qb * ATT_TQ
            rows = pl.ds(pl.multiple_of(row0, ATT_TQ), ATT_TQ)
            qi = (row0 + lax.broadcasted_iota(jnp.int32, (ATT_TQ, DEC_SEQ), 0)).astype(F32)
            kj = lax.broadcasted_iota(jnp.int32, (ATT_TQ, DEC_SEQ), 1).astype(F32)
            pos = (row0 + lax.broadcasted_iota(jnp.int32, (ATT_TQ, HEAD_DIM), 0)).astype(F32)
            for h in range(RET_HEADS):
                q = qr_scr[rows, hsl[h]]
                a = (_dot_nt(q, kr_scr[:, hsl[h]]) * _decay_mask(qi, kj, lg1[0][h], lg1[1][h])).astype(BF16)
                o = _dot(a, vb_scr[:, hsl[h]])
                o += _dot(q, s0_ref[0, h].astype(BF16)) * jnp.exp((pos + 1.0) * lg[0][h])
                o += _dot(q, s0_ref[1, h].astype(BF16)) * jnp.exp((DEC_SEQ - pos) * lg[1][h])
                acc[rows, hsl[h]] = o
            return carry

        lax.fori_loop(0, DEC_SEQ // ATT_TQ, q_body, 0)

    for h in range(RET_HEADS):
        o = acc[:, hsl[h]]
        o = o - jnp.mean(o, axis=-1, keepdims=True)
        o = o * lax.rsqrt(jnp.mean(o * o, axis=-1, keepdims=True) + EPS)
        o_ref[:, hsl[h]] = (o * _silu(g_ref[:, hsl[h]])).astype(BF16)


def _retention(p, dl, cos_t, sin_t, state_ret, l):
    tab = pl.BlockSpec((DEC_SEQ, HEAD_DIM), lambda i: (0, 0))
    st = (2, RET_HEADS, HEAD_DIM, HEAD_DIM)
    return pl.pallas_call(
        _retention_kernel,
        grid=(N_MT,),
        in_specs=[pl.BlockSpec((TM, RET_W), lambda i, k=k: (i, COL_R + k)) for k in range(4)] + [
            pl.BlockSpec((None, 2, RET_HEADS, 8, HEAD_DIM), lambda i: (l, 0, 0, 0, 0)),
            tab, tab,
            pl.BlockSpec((None, None) + st, lambda i: (_sample_idx(i), l, 0, 0, 0, 0)),
        ],
        out_specs=[
            pl.BlockSpec((TM, RET_W), lambda i: (i, 0)),
            pl.BlockSpec((SEQ_PER_TILE,) + st, lambda i: (_prompt_idx(i), 0, 0, 0, 0)),
        ],
        out_shape=[
            jax.ShapeDtypeStruct((N_TOK, RET_W), BF16),
            jax.ShapeDtypeStruct((BATCH,) + st, F32),
        ],
        scratch_shapes=[pltpu.VMEM((TM, RET_W), F32)] + [pltpu.VMEM((TM, RET_W), BF16)] * 3,
        compiler_params=_cparams(("arbitrary",)),
        name="retention",
    )(p, p, p, p, dl, cos_t, sin_t, state_ret)


def _s5_disc_kernel(are_ref, aim_ref, ldt_ref, bre_ref, bim_ref, cim_ref,
                    lre_ref, lim_ref, bbre_ref, bbim_ref, ncim_ref):
    ar = are_ref[...]
    ai = aim_ref[...]
    dt = jnp.exp(ldt_ref[...])
    mag = jnp.exp(ar * dt)
    lr = mag * jnp.cos(ai * dt)
    li = mag * jnp.sin(ai * dt)
    den = ar * ar + ai * ai
    nr = lr - 1.0
    cr = (nr * ar + li * ai) / den
    ci = (li * ar - nr * ai) / den
    br = bre_ref[...]
    bi = bim_ref[...]
    lre_ref[...] = lr
    lim_ref[...] = li
    bbre_ref[...] = cr * br - ci * bi
    bbim_ref[...] = cr * bi + ci * br
    ncim_ref[...] = -cim_ref[...]


def _s5_discretize(a_re, a_im, log_dt, b_re, b_im, c_im):
    rows = DEPTH * 2 * SSM_GROUPS
    cols = SSM_STATE * SSM_GROUP
    shp = (DEPTH, 2, SSM_GROUPS, SSM_STATE, SSM_GROUP)
    args = [
        jnp.broadcast_to(a_re[..., None], shp).reshape(rows, cols),
        jnp.broadcast_to(a_im[..., None], shp).reshape(rows, cols),
        jnp.broadcast_to(log_dt[..., None, None], shp).reshape(rows, cols),
        b_re.reshape(rows, cols), b_im.reshape(rows, cols), c_im.reshape(rows, cols),
    ]
    spec = pl.BlockSpec((rows, cols), lambda: (0, 0))
    outs = pl.pallas_call(
        _s5_disc_kernel,
        in_specs=[spec] * 6,
        out_specs=[spec] * 5,
        out_shape=[jax.ShapeDtypeStruct((rows, cols), F32)] * 5,
        name="s5_discretize",
    )(*args)
    lre, lim, bbre, bbim, ncim = outs
    lre = lre.reshape(shp)[..., 0]
    lim = lim.reshape(shp)[..., 0]
    return (lre, lim, bbre.reshape(shp), bbim.reshape(shp),
            ncim.reshape(DEPTH, 2, SSM_GROUPS, SSM_GROUP, SSM_STATE))


def _state_cols(x):
    lead = x.shape[:-3]
    x = x.reshape(lead + (N_STILE, 2, SSM_STATE, 2))
    x = jnp.moveaxis(x, -1, -3)
    return x.reshape(lead + (STATE_W,))


def _state_uncols(x):
    lead = x.shape[:-1]
    x = x.reshape(lead + (N_STILE, 2, 2, SSM_STATE))
    x = jnp.moveaxis(x, -3, -1)
    return x.reshape(lead + (SSM_GROUPS, SSM_STATE, 2))


def _s5_compact(lre, lim, bbre, bbim, c_re, ncim):
    n = DEPTH * 2
    bb = jnp.stack([bbre, bbim], axis=-1)
    bb = bb.reshape(n, N_STILE, 2, SSM_STATE, SSM_GROUP, 2)
    wb = jnp.transpose(bb, (0, 4, 1, 5, 2, 3)).reshape(DEPTH, 2, SSM_GROUP, STATE_W)
    cc = jnp.stack([c_re, ncim], axis=-1)
    cc = cc.reshape(n, N_STILE, 2, SSM_GROUP, SSM_STATE, 2)
    wc = jnp.transpose(cc, (0, 3, 1, 5, 2, 4)).reshape(DEPTH, 2, SSM_GROUP, STATE_W)
    lam = jnp.concatenate([lre.reshape(DEPTH, 2, N_STILE, 128), lim.reshape(DEPTH, 2, N_STILE, 128)], axis=2)
    return wb, wc, lam


def _group_mask(cols):
    row_g = lax.broadcasted_iota(jnp.int32, (256, cols), 0) // SSM_GROUP
    col = lax.broadcasted_iota(jnp.int32, (256, cols), 1)
    col_g = ((col // 256) % 8) * 2 + (col % 128) // SSM_STATE
    return row_g == col_g


def _s5_scan_kernel(u_ref, wb_ref, wc_ref, lam_ref, h0_ref, y_ref, hf_ref,
                    pm_scr, bm_scr, cm_scr, hbuf, y_scr, state, ends, *, segments):
    npass = 1 if segments == 1 else 2
    d = pl.program_id(0)
    ps = pl.program_id(1)
    ck = pl.program_id(2)
    nck = pl.num_programs(2)
    rows = SSM_TC * SSM_ROWS
    half = STATE_W // 2
    lam = lam_ref[...]

    @pl.when(jnp.logical_and(ps == 0, ck == 0))
    def _():
        a = lax.broadcasted_iota(jnp.int32, (rows, rows), 0)
        b = lax.broadcasted_iota(jnp.int32, (rows, rows), 1)
        same = jnp.logical_and(a // SSM_ROWS == b % SSM_TC, a % SSM_ROWS == b // SSM_TC)
        pm_scr[...] = jnp.where(same, 1.0, 0.0).astype(BF16)
        wb = jnp.tile(wb_ref[...], (256 // SSM_GROUP, 1))
        bm_scr[...] = jnp.where(_group_mask(STATE_W), wb, 0.0).astype(BF16)
        for n_ in range(2):
            wc = jnp.tile(wc_ref[:, n_ * half:(n_ + 1) * half], (256 // SSM_GROUP, 1))
            cm_scr[n_] = jnp.where(_group_mask(half), wc, 0.0).astype(BF16)

    @pl.when(ck == 0)
    def _():
        if npass == 1:
            state[...] = h0_ref[0]
        else:
            @pl.when(ps == 0)
            def _():
                state[...] = jnp.zeros_like(state)

            @pl.when(ps == 1)
            def _():
                ends[...] = state[...]
                state[...] = h0_ref[0]
                lr = lam[0:N_STILE]
                li = lam[N_STILE:]
                for _ in range(int(math.log2(SSM_SEG_LEN))):
                    lr, li = lr * lr - li * li, 2.0 * lr * li

                def carry(order, prev):
                    for sg in order:
                        for sq in range(SSM_ROWS // segments):
                            r = sq * segments + sg
                            q = r + prev
                            for j in range(N_STILE):
                                re = slice(j * 256, j * 256 + 128)
                                im = slice(j * 256 + 128, (j + 1) * 256)
                                pr = state[q:q + 1, re]
                                pi = state[q:q + 1, im]
                                ar = lr[j:j + 1]
                                ai = li[j:j + 1]
                                state[r:r + 1, re] = ends[q:q + 1, re] + ar * pr - ai * pi
                                state[r:r + 1, im] = ends[q:q + 1, im] + ar * pi + ai * pr

                @pl.when(d == 0)
                def _():
                    carry(range(1, segments), -1)

                @pl.when(d == 1)
                def _():
                    carry(range(segments - 2, -1, -1), 1)

    u = _dot(pm_scr[...], u_ref[...].reshape(rows, SSM_W).astype(BF16)).astype(BF16)
    for j in range(N_STILE):
        k0 = (j // 8) * 256
        hbuf[:, j * 256:(j + 1) * 256] = _dot(u[:, k0:k0 + 256], bm_scr[:, j * 256:(j + 1) * 256])

    JT = 4
    for jb in range(N_STILE // JT):
        tiles = list(range(jb * JT, (jb + 1) * JT))
        lrs = [jnp.broadcast_to(lam[j:j + 1], (SSM_ROWS, 128)) for j in tiles]
        lis = [jnp.broadcast_to(lam[N_STILE + j:N_STILE + j + 1], (SSM_ROWS, 128)) for j in tiles]
        init = tuple(state[:, j * 256:j * 256 + 128] for j in tiles) + \
            tuple(state[:, j * 256 + 128:(j + 1) * 256] for j in tiles)

        def body(t, carry_, tiles=tiles, lrs=lrs, lis=lis):
            tt = t + d * (SSM_TC - 1 - 2 * t)
            r = pl.ds(pl.multiple_of(tt * SSM_ROWS, SSM_ROWS), SSM_ROWS)
            out_r, out_i = [], []
            for n_, j in enumerate(tiles):
                hr, hi = carry_[n_], carry_[JT + n_]
                re = slice(j * 256, j * 256 + 128)
                im = slice(j * 256 + 128, (j + 1) * 256)
                nr = lrs[n_] * hr - lis[n_] * hi + hbuf[r, re]
                ni = lrs[n_] * hi + lis[n_] * hr + hbuf[r, im]
                hbuf[r, re] = nr
                hbuf[r, im] = ni
                out_r.append(nr)
                out_i.append(ni)
            return tuple(out_r) + tuple(out_i)

        fin = lax.fori_loop(0, SSM_TC, body, init, unroll=2)
        for n_, j in enumerate(tiles):
            state[:, j * 256:j * 256 + 128] = fin[n_]
            state[:, j * 256 + 128:(j + 1) * 256] = fin[JT + n_]

    @pl.when(ps == npass - 1)
    def _():
        hb = hbuf[...].astype(BF16)
        for n_ in range(2):
            y = _dot_nt(hb[:, n_ * half:(n_ + 1) * half], cm_scr[n_])
            y_scr[2 * n_] = y[:, :128]
            y_scr[2 * n_ + 1] = y[:, 128:]
        for r in range(SSM_ROWS):
            for c_ in range(SSM_W // 128):
                y_ref[r, :, c_ * 128:(c_ + 1) * 128] = y_scr[c_, pl.ds(r, SSM_TC, stride=SSM_ROWS), :]

    @pl.when(jnp.logical_and(ps == npass - 1, ck == nck - 1))
    def _():
        hf_ref[0] = state[...]


def _s5_scan(p3, row_block, wb, wc, lam, h0, h0_layer, l, *, segments):
    steps = p3.shape[1]
    nck = steps // SSM_TC
    npass = 1 if segments == 1 else 2

    def chunk(d, c):
        return c + d * (nck - 1 - 2 * c)

    def y_chunk(d, p, c):
        return jnp.where(p == npass - 1, chunk(d, c), chunk(d, 0))

    par = pl.BlockSpec((None, None, SSM_GROUP, STATE_W), lambda d, p, c: (l, d, 0, 0))
    st = pl.BlockSpec((1, SSM_ROWS, STATE_W), lambda d, p, c: (d, 0, 0))
    return pl.pallas_call(
        functools.partial(_s5_scan_kernel, segments=segments),
        grid=(2, npass, nck),
        in_specs=[
            pl.BlockSpec((SSM_ROWS, SSM_TC, SSM_W), lambda d, p, c: (row_block, chunk(d, c), COL_U)),
            par, par,
            pl.BlockSpec((None, None, 2 * N_STILE, 128), lambda d, p, c: (l, d, 0, 0)),
            pl.BlockSpec((None, 1, SSM_ROWS, STATE_W), lambda d, p, c: (h0_layer, d, 0, 0)),
        ],
        out_specs=[
            pl.BlockSpec((None, SSM_ROWS, SSM_TC, SSM_W), lambda d, p, c: (d, 0, y_chunk(d, p, c), 0)),
            st,
        ],
        out_shape=[
            jax.ShapeDtypeStruct((2, SSM_ROWS, steps, SSM_W), F32),
            jax.ShapeDtypeStruct((2, SSM_ROWS, STATE_W), F32),
        ],
        scratch_shapes=[
            pltpu.VMEM((SSM_TC * SSM_ROWS, SSM_TC * SSM_ROWS), BF16),
            pltpu.VMEM((256, STATE_W), BF16),
            pltpu.VMEM((2, 256, STATE_W // 2), BF16),
            pltpu.VMEM((SSM_TC * SSM_ROWS, STATE_W), F32),
            pltpu.VMEM((SSM_W // 128, SSM_TC * SSM_ROWS, 128), F32),
            pltpu.VMEM((SSM_ROWS, STATE_W), F32),
            pltpu.VMEM((SSM_ROWS, STATE_W), F32),
        ],
        compiler_params=_cparams(("arbitrary", "arbitrary", "arbitrary")),
        name="s5_scan_seg%d" % segments,
    )(p3, wb, wc, lam, h0)


def _s5_glu_kernel(ypf_ref, ypb_ref, ysf_ref, ysb_ref, u_ref, d_ref, w_ref, o_ref):
    i = pl.program_id(0)
    w = w_ref[...].astype(BF16)

    def glu(y):
        y = y + d_ref[...] * u_ref[...]
        z = _dot(jax.nn.gelu(y).astype(BF16), w)
        o_ref[...] = (z[:, :SSM_W] * _sigmoid(z[:, SSM_W:])).astype(BF16)

    @pl.when(i < N_PT)
    def _():
        glu(ypf_ref[...] + ypb_ref[...])

    @pl.when(i >= N_PT)
    def _():
        glu(ysf_ref[...] + ysb_ref[...])


def _s5_glu(yp, ys, p, d, w, l):
    return pl.pallas_call(
        _s5_glu_kernel,
        grid=(N_MT,),
        in_specs=[
            pl.BlockSpec((None, TM, SSM_W), lambda i: (0, _prompt_idx(i), 0)),
            pl.BlockSpec((None, TM, SSM_W), lambda i: (1, _prompt_idx(i), 0)),
            pl.BlockSpec((None, TM, SSM_W), lambda i: (0, _sample_idx(i), 0)),
            pl.BlockSpec((None, TM, SSM_W), lambda i: (1, _sample_idx(i), 0)),
            pl.BlockSpec((TM, SSM_W), lambda i: (i, COL_U)),
            pl.BlockSpec((None, 1, SSM_W), lambda i: (l, 0, 0)),
            pl.BlockSpec((None, SSM_W, 2 * SSM_W), lambda i: (l, 0, 0)),
        ],
        out_specs=pl.BlockSpec((TM, SSM_W), lambda i: (i, 0)),
        out_shape=jax.ShapeDtypeStruct((N_TOK, SSM_W), BF16),
        compiler_params=_cparams(("arbitrary",)),
        name="s5_glu",
    )(yp, yp, ys, ys, p, d.reshape(DEPTH, 1, SSM_W), w)


def _s5_start_states(state_ssm):
    h0 = _state_cols(jnp.transpose(state_ssm, (1, 2, 0, 3, 4, 5)))
    rows = jnp.zeros((DEPTH, 2, DEC_BATCH, SSM_SEGS, STATE_W), F32)
    rows = rows.at[:, 0, :, 0].set(h0[:, 0]).at[:, 1, :, SSM_SEGS - 1].set(h0[:, 1])
    return rows.reshape(DEPTH, 2, SSM_ROWS, STATE_W)


def _s5_layer(p, s5p, h0_all, ssm_d, w_ssm_glu, l):
    wb, wc, lam = s5p
    yp, hfin = _s5_scan(p.reshape(N_TOK // SEQ, SEQ, IN_W), 0, wb, wc, lam,
                        jnp.zeros((1, 2, SSM_ROWS, STATE_W), F32), 0, l, segments=1)
    ys, _ = _s5_scan(p.reshape(N_TOK // SSM_SEG_LEN, SSM_SEG_LEN, IN_W), N_PROMPT // SSM_SEG_LEN // SSM_ROWS,
                     wb, wc, lam, h0_all, l, l, segments=SSM_SEGS)
    out = _s5_glu(yp.reshape(2, N_PROMPT, SSM_W), ys.reshape(2, N_SAMPLE, SSM_W), p, ssm_d, w_ssm_glu, l)
    return out, hfin


def _rope_tables():
    rows = DEC_SEQ // GRID_W
    r = jnp.repeat(jnp.arange(rows), GRID_W).astype(F32)
    col = jnp.tile(jnp.arange(GRID_W), rows).astype(F32)
    inv = ROPE_BASE ** (-jnp.arange(ROPE_FREQS, dtype=F32) / ROPE_FREQS)
    ang = jnp.stack([r, col], axis=-1)[:, :, None] * inv
    cos = jnp.cos(ang)
    sin = jnp.sin(ang)
    cos_t = jnp.concatenate([cos, cos], axis=-1).reshape(DEC_SEQ, HEAD_DIM)
    sin_t = jnp.concatenate([-sin, sin], axis=-1).reshape(DEC_SEQ, HEAD_DIM)
    return cos_t, sin_t


def kernel(x_prompt, x_sample, cache_k, cache_v, state_ssm, state_ret, c, c_ctx, w_mod, b_mod, norm_g,
           w_ffn_in, w_ffn_out, w_in, w_out, q_norm_g, k_norm_g, ssm_a_re, ssm_a_im, ssm_log_dt,
           ssm_b_re, ssm_b_im, ssm_c_re, ssm_c_im, ssm_d, w_ssm_glu, ret_decay_logit, final_norm_g):
    x = (x_prompt.reshape(N_PROMPT, D_MODEL), x_sample.reshape(N_SAMPLE, D_MODEL))
    cond8 = jnp.concatenate([c_ctx[None], c, jnp.zeros((5, D_MODEL), F32)], axis=0)
    mods = _adaln(cond8, w_mod, b_mod).reshape(DEPTH, 8, N_SUB * 3, D_MODEL)
    modt = jnp.transpose(mods[:, jnp.array(TILE_MOD, jnp.int32)], (0, 2, 1, 3))[:, :, :, None, :]
    gains = norm_g.reshape(DEPTH * N_SUB, 1, D_MODEL)
    cos_t, sin_t = _rope_tables()
    lre, lim, bbre, bbim, ncim = _s5_discretize(ssm_a_re, ssm_a_im, ssm_log_dt, ssm_b_re, ssm_b_im, ssm_c_im)
    s5p = _s5_compact(lre, lim, bbre, bbim, ssm_c_re, ncim)
    dl = jnp.broadcast_to(ret_decay_logit[:, :, :, None, None], (DEPTH, 2, RET_HEADS, 8, HEAD_DIM))
    h0_all = _s5_start_states(state_ssm)
    ks_, vs_, hs_, ss_ = [], [], [], []
    for l in range(DEPTH):
        x = _ffn(x, gains, modt, w_ffn_in, w_ffn_out, l, 0)
        p = _inproj(x, gains, modt, w_in, l)
        attn, k_l, v_l = _attention(p, cache_k, cache_v, q_norm_g, k_norm_g, cos_t, sin_t, l)
        ssm, h_l = _s5_layer(p, s5p, h0_all, ssm_d, w_ssm_glu, l)
        ret, s_l = _retention(p, dl, cos_t, sin_t, state_ret, l)
        x = _outproj(x, attn, ssm, ret, modt, w_out, l)
        x = _ffn(x, gains, modt, w_ffn_in, w_ffn_out, l, 1, final_g=final_norm_g if l == DEPTH - 1 else None)
        ks_.append(k_l)
        vs_.append(v_l)
        hs_.append(h_l)
        ss_.append(s_l)
    y_prompt = x[0].reshape(BATCH, SEQ, D_MODEL)
    y_sample = x[1].reshape(DEC_BATCH, DEC_SEQ, D_MODEL)
    new_k = jnp.stack(ks_, axis=1).reshape(BATCH, DEPTH, SEQ, N_KV_HEADS, HEAD_DIM)
    new_v = jnp.stack(vs_, axis=1).reshape(BATCH, DEPTH, SEQ, N_KV_HEADS, HEAD_DIM)
    new_h = jnp.transpose(_state_uncols(jnp.stack(hs_)), (2, 0, 1, 3, 4, 5))
    return y_prompt, y_sample, new_k, new_v, new_h, jnp.stack(ss_, axis=1)
```

```python
import functools
import math

import jax
import jax.numpy as jnp
from jax import lax
from jax.experimental import pallas as pl
from jax.experimental.pallas import tpu as pltpu

D_MODEL = 2048
BATCH = 16
SEQ = 256
DEPTH = 2
DEC_BATCH = 2
DEC_SEQ = 1024
PAST_LEN = 512
GRID_W = 64
HEAD_DIM = 128
N_Q_HEADS = 8
N_KV_HEADS = 2
Q_PER_KV = 4
ATTN_W = 1024
KV_W = 256
SSM_W = 512
SSM_GROUP = 16
SSM_GROUPS = 32
SSM_STATE = 64
RET_HEADS = 4
RET_W = 512
IN_W = 4096
D_FF = 5632
N_SUB = 3
RET_CHUNK = 128
ROPE_BASE = 10000.0
ROPE_FREQS = 32
EPS = 1e-6

N_PROMPT = BATCH * SEQ
N_SAMPLE = DEC_BATCH * DEC_SEQ
N_TOK = N_PROMPT + N_SAMPLE
TM = 1024
N_MT = N_TOK // TM
N_PT = N_PROMPT // TM
SEQ_PER_TILE = TM // SEQ
TILE_MOD = (0, 0, 0, 0, 1, 2)
ROW_CHUNK = 256
TF = 256
FFN_NSPLIT = 4
TN = 512
STATE_W = 2 * SSM_GROUPS * SSM_STATE
N_STILE = STATE_W // 256
SSM_ROWS = 16
SSM_SEGS = 8
SSM_SEG_LEN = DEC_SEQ // SSM_SEGS
SSM_TC = 32
SSM_TILE_GROUPS = 4
ATT_TQ = 256
ATT_S = PAST_LEN + DEC_SEQ
VMEM_LIMIT = 56 * 1024 * 1024
VMEM_LIMIT_FFN = 60 * 1024 * 1024

COL_K = ATTN_W // KV_W
COL_V = COL_K + 1
COL_U = (ATTN_W + 2 * KV_W) // SSM_W
COL_R = (ATTN_W + 2 * KV_W + SSM_W) // RET_W

BF16 = jnp.bfloat16
F32 = jnp.float32


def _cparams(sem, limit=VMEM_LIMIT):
    return pltpu.CompilerParams(dimension_semantics=sem, vmem_limit_bytes=limit)


def _dot(a, b):
    return jnp.dot(a, b, preferred_element_type=F32)


def _dot_nt(a, b):
    return lax.dot_general(a, b, (((1,), (1,)), ((), ())), preferred_element_type=F32)


def _sigmoid(x):
    return 1.0 / (1.0 + jnp.exp(-x))


def _silu(x):
    return x * _sigmoid(x)


def _sample_idx(i):
    return jnp.clip(i - N_PT, 0, DEC_BATCH - 1)


def _prompt_idx(i):
    return jnp.minimum(i, N_PT - 1)


def _row_chunks(n_rows, body):
    def step(c, carry):
        body(pl.ds(pl.multiple_of(c * ROW_CHUNK, ROW_CHUNK), ROW_CHUNK))
        return carry

    lax.fori_loop(0, n_rows // ROW_CHUNK, step, 0)


def _mod_spec(l, k, width=D_MODEL, col=lambda j: 0, tile=lambda i: i):
    return pl.BlockSpec((None, None, 1, 1, width), lambda i, j: (l, k, tile(i), 0, col(j)))


def _gain_spec(l, sub):
    return pl.BlockSpec((None, 1, D_MODEL), lambda i, j: (l * N_SUB + sub, 0, 0))


def _adaln_kernel(c_ref, w_ref, b_ref, o_ref):
    a = _silu(c_ref[...]).astype(BF16)
    o_ref[0] = _dot(a, w_ref[0].astype(BF16)) + b_ref[0]


def _adaln(cond8, w_mod, b_mod):
    n = w_mod.shape[-1]
    tn = 1024
    return pl.pallas_call(
        _adaln_kernel,
        grid=(DEPTH, n // tn),
        in_specs=[
            pl.BlockSpec((8, D_MODEL), lambda l, j: (0, 0)),
            pl.BlockSpec((1, D_MODEL, tn), lambda l, j: (l, 0, j)),
            pl.BlockSpec((1, 1, tn), lambda l, j: (l, 0, j)),
        ],
        out_specs=pl.BlockSpec((1, 8, tn), lambda l, j: (l, 0, j)),
        out_shape=jax.ShapeDtypeStruct((DEPTH, 8, n), F32),
        compiler_params=_cparams(("arbitrary", "arbitrary")),
        name="adaln",
    )(cond8, w_mod, b_mod.reshape(DEPTH, 1, n))


def _norm_mod(x, g, shift, scale):
    ms = jnp.mean(x * x, axis=-1, keepdims=True)
    return x * lax.rsqrt(ms + EPS) * (g * (1.0 + scale)) + shift


def _ffn_kernel(*refs, n_x, n_out, final):
    refs = list(refs)
    x_refs = refs[:n_x]
    g_ref, sh_ref, sc_ref, gt_ref, wa_ref, wu_ref, wo_ref = refs[n_x:n_x + 7]
    rest = refs[n_x + 7:]
    fg_ref = rest.pop(0) if final else None
    o_refs, h_scr = rest[:n_out], rest[n_out]
    i = pl.program_id(0)
    j = pl.program_id(1)

    def run(x_ref, o_ref):
        @pl.when(j == 0)
        def _():
            def pre(rows):
                h = _norm_mod(x_ref[rows, :], g_ref[...], sh_ref[0], sc_ref[0])
                h_scr[rows, :] = h.astype(BF16)

            _row_chunks(TM, pre)
            o_ref[...] = jnp.zeros_like(o_ref)

        h = h_scr[...]
        a = _dot(h, wa_ref[...].astype(BF16))
        u = _dot(h, wu_ref[...].astype(BF16))
        mid = (_silu(a) * u).astype(BF16)
        wn = D_MODEL // FFN_NSPLIT
        for n in range(FFN_NSPLIT):
            cs = slice(n * wn, (n + 1) * wn)
            o_ref[:, cs] += _dot(mid, wo_ref[:, cs].astype(BF16))

        @pl.when(j == pl.num_programs(1) - 1)
        def _():
            def post(rows):
                out = x_ref[rows, :] + (0.5 * gt_ref[0]) * o_ref[rows, :]
                if final:
                    ms = jnp.mean(out * out, axis=-1, keepdims=True)
                    out = out * lax.rsqrt(ms + EPS) * fg_ref[...]
                o_ref[rows, :] = out

            _row_chunks(TM, post)

    if n_x == 1 and n_out == 1:
        run(x_refs[0], o_refs[0])
    else:
        @pl.when(i < N_PT)
        def _():
            run(x_refs[0], o_refs[0])

        @pl.when(i >= N_PT)
        def _():
            run(x_refs[-1], o_refs[-1])


def _split_tok_specs():
    return [pl.BlockSpec((TM, D_MODEL), lambda i, j: (_prompt_idx(i), 0)),
            pl.BlockSpec((TM, D_MODEL), lambda i, j: (_sample_idx(i), 0), pipeline_mode=pl.Buffered(1))]


def _ffn(x, gains, modt, w_in, w_out, l, f, final_g=None):
    final = final_g is not None
    xs = list(x) if isinstance(x, (tuple, list)) else [x]
    sub = 2 * f
    nf = D_FF // TF
    tok = pl.BlockSpec((TM, D_MODEL), lambda i, j: (i, 0))
    in_specs = (_split_tok_specs() if len(xs) == 2 else [tok]) + [
        _gain_spec(l, sub), _mod_spec(l, 3 * sub), _mod_spec(l, 3 * sub + 1), _mod_spec(l, 3 * sub + 2),
        pl.BlockSpec((None, None, D_MODEL, TF), lambda i, j: (l, f, 0, j)),
        pl.BlockSpec((None, None, D_MODEL, TF), lambda i, j: (l, f, 0, j + nf)),
        pl.BlockSpec((None, None, TF, D_MODEL), lambda i, j: (l, f, j, 0)),
    ]
    args = xs + [gains, modt, modt, modt, w_in, w_in, w_out]
    if final:
        in_specs.append(pl.BlockSpec((1, D_MODEL), lambda i, j: (0, 0)))
        args.append(final_g.reshape(1, D_MODEL))
        out_specs = _split_tok_specs()
        out_shape = [jax.ShapeDtypeStruct((N_PROMPT, D_MODEL), F32), jax.ShapeDtypeStruct((N_SAMPLE, D_MODEL), F32)]
    else:
        out_specs = tok
        out_shape = jax.ShapeDtypeStruct((N_TOK, D_MODEL), F32)
    return pl.pallas_call(
        functools.partial(_ffn_kernel, n_x=len(xs), n_out=2 if final else 1, final=final),
        grid=(N_MT, nf),
        in_specs=in_specs,
        out_specs=out_specs,
        out_shape=out_shape,
        scratch_shapes=[pltpu.VMEM((TM, D_MODEL), BF16)],
        compiler_params=_cparams(("arbitrary", "arbitrary"), VMEM_LIMIT_FFN),
        name="ffn_final" if final else ("ffn_first" if len(xs) == 2 else "ffn"),
    )(*args)


def _resident_weight_spec(rows, l, ncol):
    return pl.BlockSpec((None, rows, TN), lambda i, j: (l, 0, jnp.where(i == 0, j, ncol - 1)))


def _inproj_kernel(x_ref, g_ref, sh_ref, sc_ref, w_ref, o_ref, h_scr, w_scr):
    i = pl.program_id(0)
    j = pl.program_id(1)

    @pl.when(j == 0)
    def _():
        def pre(rows):
            h = _norm_mod(x_ref[rows, :], g_ref[...], sh_ref[0], sc_ref[0])
            h_scr[rows, :] = h.astype(BF16)

        _row_chunks(TM, pre)

    @pl.when(i == 0)
    def _():
        w_scr[j] = w_ref[...].astype(BF16)

    o_ref[...] = _dot(h_scr[...], w_scr[j])


def _inproj(x, gains, modt, w, l):
    ncol = IN_W // TN
    return pl.pallas_call(
        _inproj_kernel,
        grid=(N_MT, ncol),
        in_specs=[
            pl.BlockSpec((TM, D_MODEL), lambda i, j: (i, 0)),
            _gain_spec(l, 1), _mod_spec(l, 3), _mod_spec(l, 4),
            _resident_weight_spec(D_MODEL, l, ncol),
        ],
        out_specs=pl.BlockSpec((TM, TN), lambda i, j: (i, j)),
        out_shape=jax.ShapeDtypeStruct((N_TOK, IN_W), F32),
        scratch_shapes=[pltpu.VMEM((TM, D_MODEL), BF16), pltpu.VMEM((ncol, D_MODEL, TN), BF16)],
        compiler_params=_cparams(("arbitrary", "arbitrary")),
        name="inproj",
    )(x, gains, modt, modt, w)


def _outproj_kernel(x_ref, a_ref, s_ref, r_ref, gt_ref, w_ref, o_ref, w_scr):
    i = pl.program_id(0)
    j = pl.program_id(1)

    @pl.when(i == 0)
    def _():
        w_scr[j] = w_ref[...].astype(BF16)

    y = _dot(a_ref[...], w_scr[j, 0:ATTN_W, :])
    y += _dot(s_ref[...], w_scr[j, ATTN_W:ATTN_W + SSM_W, :])
    y += _dot(r_ref[...], w_scr[j, ATTN_W + SSM_W:, :])
    o_ref[...] = x_ref[...] + gt_ref[0] * y


def _outproj(x, attn, ssm, ret, modt, w, l):
    ncol = D_MODEL // TN
    return pl.pallas_call(
        _outproj_kernel,
        grid=(N_MT, ncol),
        in_specs=[
            pl.BlockSpec((TM, TN), lambda i, j: (i, j)),
            pl.BlockSpec((TM, ATTN_W), lambda i, j: (i, 0)),
            pl.BlockSpec((TM, SSM_W), lambda i, j: (i, 0)),
            pl.BlockSpec((TM, RET_W), lambda i, j: (i, 0)),
            _mod_spec(l, 5, TN, lambda j: j),
            _resident_weight_spec(D_MODEL, l, ncol),
        ],
        out_specs=pl.BlockSpec((TM, TN), lambda i, j: (i, j)),
        out_shape=jax.ShapeDtypeStruct((N_TOK, D_MODEL), F32),
        scratch_shapes=[pltpu.VMEM((ncol, D_MODEL, TN), BF16)],
        compiler_params=_cparams(("arbitrary", "arbitrary")),
        name="outproj",
    )(x, attn, ssm, ret, modt, w)


def _head_rms(x, g):
    ms = jnp.mean(x * x, axis=-1, keepdims=True)
    return x * lax.rsqrt(ms + EPS) * g


def _rope(x, cos, sin_signed):
    lane = lax.broadcasted_iota(jnp.int32, x.shape, 1)
    partner = jnp.where((lane % 64) < 32, pltpu.roll(x, 96, 1), pltpu.roll(x, 32, 1))
    return x * cos + partner * sin_signed


SOFTMAX_EXP2_SCALE = HEAD_DIM ** -0.5 * math.log2(math.e)


def _attend_short(qs, k, v):
    s = _dot_nt(qs.astype(BF16), k)
    p = jnp.exp2((s - jnp.max(s, axis=-1, keepdims=True)) * SOFTMAX_EXP2_SCALE)
    l = jnp.sum(p, axis=-1, keepdims=True)
    return _dot(p.astype(BF16), v) / l


def _attend(qs, k, vt):
    st = _dot_nt(k, qs.astype(BF16))
    p = jnp.exp2((st - jnp.max(st, axis=0, keepdims=True)) * SOFTMAX_EXP2_SCALE)
    l = jnp.sum(p, axis=0, keepdims=True)
    return (_dot(vt, p.astype(BF16)) / l).T


def _attn_kernel(q_ref, k_ref, v_ref, ck_ref, cv_ref, qg_ref, kg_ref, cos_ref, sin_ref,
                 o_ref, kc_ref, vc_ref, k_scr, vt_scr):
    i = pl.program_id(0)
    qg = qg_ref[...]
    kg = kg_ref[...]

    def heads(kv):
        return [slice((kv * Q_PER_KV + g) * HEAD_DIM, (kv * Q_PER_KV + g + 1) * HEAD_DIM) for g in range(Q_PER_KV)]

    @pl.when(i < N_PT)
    def _():
        def seq_body(sq, carry):
            rows = pl.ds(pl.multiple_of(sq * SEQ, SEQ), SEQ)
            v = v_ref[rows, :]
            vc_ref[sq] = v
            for kv in range(N_KV_HEADS):
                ksl = slice(kv * HEAD_DIM, (kv + 1) * HEAD_DIM)
                kn = _head_rms(k_ref[rows, ksl], kg)
                kc_ref[sq, :, ksl] = kn
                qs = jnp.concatenate([_head_rms(q_ref[rows, hs], qg) for hs in heads(kv)], axis=0)
                o = _attend_short(qs, kn.astype(BF16), v[:, ksl].astype(BF16))
                for g, hs in enumerate(heads(kv)):
                    o_ref[rows, hs] = o[g * SEQ:(g + 1) * SEQ].astype(BF16)
            return carry

        lax.fori_loop(0, SEQ_PER_TILE, seq_body, 0)

    @pl.when(i >= N_PT)
    def _():
        k_scr[0:PAST_LEN, :] = ck_ref[...].astype(BF16)
        for kv in range(N_KV_HEADS):
            ksl = slice(kv * HEAD_DIM, (kv + 1) * HEAD_DIM)
            vt_scr[kv, :, 0:PAST_LEN] = cv_ref[:, ksl].T.astype(BF16)
            vt_scr[kv, :, PAST_LEN:] = v_ref[:, ksl].T.astype(BF16)
            kn = _rope(_head_rms(k_ref[:, ksl], kg), cos_ref[...], sin_ref[...])
            k_scr[PAST_LEN:, ksl] = kn.astype(BF16)

        def q_body(qb, carry):
            rows = pl.ds(pl.multiple_of(qb * ATT_TQ, ATT_TQ), ATT_TQ)
            cos = cos_ref[rows, :]
            sin = sin_ref[rows, :]
            for kv in range(N_KV_HEADS):
                ksl = slice(kv * HEAD_DIM, (kv + 1) * HEAD_DIM)
                qs = jnp.concatenate([_rope(_head_rms(q_ref[rows, hs], qg), cos, sin) for hs in heads(kv)], axis=0)
                o = _attend(qs, k_scr[:, ksl], vt_scr[kv])
                for g, hs in enumerate(heads(kv)):
                    o_ref[rows, hs] = o[g * ATT_TQ:(g + 1) * ATT_TQ].astype(BF16)
            return carry

        lax.fori_loop(0, DEC_SEQ // ATT_TQ, q_body, 0)


def _attention(p, cache_k, cache_v, qg, kg, cos_t, sin_t, l):
    cache_spec = pl.BlockSpec((None, None, PAST_LEN, KV_W), lambda i: (_sample_idx(i), l, 0, 0))
    vec = pl.BlockSpec((None, 1, HEAD_DIM), lambda i: (l, 0, 0))
    tab = pl.BlockSpec((DEC_SEQ, HEAD_DIM), lambda i: (0, 0))
    new_cache = pl.BlockSpec((SEQ_PER_TILE, SEQ, KV_W), lambda i: (_prompt_idx(i), 0, 0))
    return pl.pallas_call(
        _attn_kernel,
        grid=(N_MT,),
        in_specs=[
            pl.BlockSpec((TM, ATTN_W), lambda i: (i, 0)),
            pl.BlockSpec((TM, KV_W), lambda i: (i, COL_K)),
            pl.BlockSpec((TM, KV_W), lambda i: (i, COL_V)),
            cache_spec, cache_spec, vec, vec, tab, tab,
        ],
        out_specs=[pl.BlockSpec((TM, ATTN_W), lambda i: (i, 0)), new_cache, new_cache],
        out_shape=[
            jax.ShapeDtypeStruct((N_TOK, ATTN_W), BF16),
            jax.ShapeDtypeStruct((BATCH, SEQ, KV_W), F32),
            jax.ShapeDtypeStruct((BATCH, SEQ, KV_W), F32),
        ],
        scratch_shapes=[pltpu.VMEM((ATT_S, KV_W), BF16), pltpu.VMEM((N_KV_HEADS, HEAD_DIM, ATT_S), BF16)],
        compiler_params=_cparams(("arbitrary",)),
        name="attention",
    )(p, p, p, cache_k.reshape(DEC_BATCH, DEPTH, PAST_LEN, KV_W), cache_v.reshape(DEC_BATCH, DEPTH, PAST_LEN, KV_W),
      qg.reshape(DEPTH, 1, HEAD_DIM), kg.reshape(DEPTH, 1, HEAD_DIM), cos_t, sin_t)


def _log_sigmoid(x):
    return -(jnp.maximum(-x, 0.0) + jnp.log(1.0 + jnp.exp(-jnp.abs(x))))


def _decay_mask(qi, kj, lg_f, lg_b):
    dd = qi - kj
    log2e = math.log2(math.e)
    w = jnp.exp2(dd * jnp.where(dd > 0, lg_f * log2e, -lg_b * log2e))
    return jnp.where(dd == 0, 2.0, w)


def _retention_kernel(q_ref, k_ref, v_ref, g_ref, dl_ref, cos_ref, sin_ref, s0_ref, o_ref, sf_ref,
                      acc, qr_scr, kr_scr, vb_scr):
    i = pl.program_id(0)
    scale = HEAD_DIM ** -0.5
    lg = [[_log_sigmoid(dl_ref[d, h][0:1, :]) for h in range(RET_HEADS)] for d in range(2)]
    lg1 = [[lg[d][h][:, 0:1] for h in range(RET_HEADS)] for d in range(2)]
    hsl = [slice(h * HEAD_DIM, (h + 1) * HEAD_DIM) for h in range(RET_HEADS)]

    @pl.when(i < N_PT)
    def _():
        qi = lax.broadcasted_iota(jnp.int32, (SEQ, SEQ), 0).astype(F32)
        kj = lax.broadcasted_iota(jnp.int32, (SEQ, SEQ), 1).astype(F32)
        pos = lax.broadcasted_iota(jnp.int32, (SEQ, HEAD_DIM), 0).astype(F32)
        masks = [_decay_mask(qi, kj, lg1[0][h], lg1[1][h]) for h in range(RET_HEADS)]
        kdec_f = [jnp.exp((SEQ - 1.0 - pos) * lg[0][h]) for h in range(RET_HEADS)]
        kdec_b = [jnp.exp(pos * lg[1][h]) for h in range(RET_HEADS)]

        def seq_body(sq, carry):
            rows = pl.ds(pl.multiple_of(sq * SEQ, SEQ), SEQ)
            for h in range(RET_HEADS):
                q = q_ref[rows, hsl[h]].astype(BF16)
                k = k_ref[rows, hsl[h]] * scale
                v = v_ref[rows, hsl[h]].astype(BF16)
                a = (_dot_nt(q, k.astype(BF16)) * masks[h]).astype(BF16)
                acc[rows, hsl[h]] = _dot(a, v)
                sf_ref[sq, 0, h] = _dot((k * kdec_f[h]).T.astype(BF16), v)
                sf_ref[sq, 1, h] = _dot((k * kdec_b[h]).T.astype(BF16), v)
            return carry

        lax.fori_loop(0, SEQ_PER_TILE, seq_body, 0)

    @pl.when(i >= N_PT)
    def _():
        for h in range(RET_HEADS):
            qr_scr[:, hsl[h]] = _rope(q_ref[:, hsl[h]], cos_ref[...], sin_ref[...]).astype(BF16)
            kr_scr[:, hsl[h]] = _rope(k_ref[:, hsl[h]] * scale, cos_ref[...], sin_ref[...]).astype(BF16)
        vb_scr[...] = v_ref[...].astype(BF16)

        def q_body(qb, carry):
            row0 = qb * ATT_TQ
            rows = pl.ds(pl.multiple_of(row0, ATT_TQ), ATT_TQ)
            qi = (row0 + lax.broadcasted_iota(jnp.int32, (ATT_TQ, DEC_SEQ), 0)).astype(F32)
            kj = lax.broadcasted_iota(jnp.int32, (ATT_TQ, DEC_SEQ), 1).astype(F32)
            pos = (row0 + lax.broadcasted_iota(jnp.int32, (ATT_TQ, HEAD_DIM), 0)).astype(F32)
            for h in range(RET_HEADS):
                q = qr_scr[rows, hsl[h]]
                a = (_dot_nt(q, kr_scr[:, hsl[h]]) * _decay_mask(qi, kj, lg1[0][h], lg1[1][h])).astype(BF16)
                o = _dot(a, vb_scr[:, hsl[h]])
                o += _dot(q, s0_ref[0, h].astype(BF16)) * jnp.exp((pos + 1.0) * lg[0][h])
                o += _dot(q, s0_ref[1, h].astype(BF16)) * jnp.exp((DEC_SEQ - pos) * lg[1][h])
                acc[rows, hsl[h]] = o
            return carry

        lax.fori_loop(0, DEC_SEQ // ATT_TQ, q_body, 0)

    for h in range(RET_HEADS):
        o = acc[:, hsl[h]]
        o = o - jnp.mean(o, axis=-1, keepdims=True)
        o = o * lax.rsqrt(jnp.mean(o * o, axis=-1, keepdims=True) + EPS)
        o_ref[:, hsl[h]] = (o * _silu(g_ref[:, hsl[h]])).astype(BF16)


def _retention(p, dl, cos_t, sin_t, state_ret, l):
    tab = pl.BlockSpec((DEC_SEQ, HEAD_DIM), lambda i: (0, 0))
    st = (2, RET_HEADS, HEAD_DIM, HEAD_DIM)
    return pl.pallas_call(
        _retention_kernel,
        grid=(N_MT,),
        in_specs=[pl.BlockSpec((TM, RET_W), lambda i, k=k: (i, COL_R + k)) for k in range(4)] + [
            pl.BlockSpec((None, 2, RET_HEADS, 8, HEAD_DIM), lambda i: (l, 0, 0, 0, 0)),
            tab, tab,
            pl.BlockSpec((None, None) + st, lambda i: (_sample_idx(i), l, 0, 0, 0, 0)),
        ],
        out_specs=[
            pl.BlockSpec((TM, RET_W), lambda i: (i, 0)),
            pl.BlockSpec((SEQ_PER_TILE,) + st, lambda i: (_prompt_idx(i), 0, 0, 0, 0)),
        ],
        out_shape=[
            jax.ShapeDtypeStruct((N_TOK, RET_W), BF16),
            jax.ShapeDtypeStruct((BATCH,) + st, F32),
        ],
        scratch_shapes=[pltpu.VMEM((TM, RET_W), F32)] + [pltpu.VMEM((TM, RET_W), BF16)] * 3,
        compiler_params=_cparams(("arbitrary",)),
        name="retention",
    )(p, p, p, p, dl, cos_t, sin_t, state_ret)


def _s5_disc_kernel(are_ref, aim_ref, ldt_ref, bre_ref, bim_ref, cim_ref,
                    lre_ref, lim_ref, bbre_ref, bbim_ref, ncim_ref):
    ar = are_ref[...]
    ai = aim_ref[...]
    dt = jnp.exp(ldt_ref[...])
    mag = jnp.exp(ar * dt)
    lr = mag * jnp.cos(ai * dt)
    li = mag * jnp.sin(ai * dt)
    den = ar * ar + ai * ai
    nr = lr - 1.0
    cr = (nr * ar + li * ai) / den
    ci = (li * ar - nr * ai) / den
    br = bre_ref[...]
    bi = bim_ref[...]
    lre_ref[...] = lr
    lim_ref[...] = li
    bbre_ref[...] = cr * br - ci * bi
    bbim_ref[...] = cr * bi + ci * br
    ncim_ref[...] = -cim_ref[...]


def _s5_discretize(a_re, a_im, log_dt, b_re, b_im, c_im):
    rows = DEPTH * 2 * SSM_GROUPS
    cols = SSM_STATE * SSM_GROUP
    shp = (DEPTH, 2, SSM_GROUPS, SSM_STATE, SSM_GROUP)
    args = [
        jnp.broadcast_to(a_re[..., None], shp).reshape(rows, cols),
        jnp.broadcast_to(a_im[..., None], shp).reshape(rows, cols),
        jnp.broadcast_to(log_dt[..., None, None], shp).reshape(rows, cols),
        b_re.reshape(rows, cols), b_im.reshape(rows, cols), c_im.reshape(rows, cols),
    ]
    spec = pl.BlockSpec((rows, cols), lambda: (0, 0))
    outs = pl.pallas_call(
        _s5_disc_kernel,
        in_specs=[spec] * 6,
        out_specs=[spec] * 5,
        out_shape=[jax.ShapeDtypeStruct((rows, cols), F32)] * 5,
        name="s5_discretize",
    )(*args)
    lre, lim, bbre, bbim, ncim = outs
    lre = lre.reshape(shp)[..., 0]
    lim = lim.reshape(shp)[..., 0]
    return (lre, lim, bbre.reshape(shp), bbim.reshape(shp),
            ncim.reshape(DEPTH, 2, SSM_GROUPS, SSM_GROUP, SSM_STATE))


def _state_cols(x):
    lead = x.shape[:-3]
    x = x.reshape(lead + (N_STILE, 2, SSM_STATE, 2))
    x = jnp.moveaxis(x, -1, -3)
    return x.reshape(lead + (STATE_W,))


def _state_uncols(x):
    lead = x.shape[:-1]
    x = x.reshape(lead + (N_STILE, 2, 2, SSM_STATE))
    x = jnp.moveaxis(x, -3, -1)
    return x.reshape(lead + (SSM_GROUPS, SSM_STATE, 2))


def _s5_compact(lre, lim, bbre, bbim, c_re, ncim):
    n = DEPTH * 2
    bb = jnp.stack([bbre, bbim], axis=-1)
    bb = bb.reshape(n, N_STILE, 2, SSM_STATE, SSM_GROUP, 2)
    wb = jnp.transpose(bb, (0, 4, 1, 5, 2, 3)).reshape(DEPTH, 2, SSM_GROUP, STATE_W)
    cc = jnp.stack([c_re, ncim], axis=-1)
    cc = cc.reshape(n, N_STILE, 2, SSM_GROUP, SSM_STATE, 2)
    wc = jnp.transpose(cc, (0, 3, 1, 5, 2, 4)).reshape(DEPTH, 2, SSM_GROUP, STATE_W)
    lam = jnp.concatenate([lre.reshape(DEPTH, 2, N_STILE, 128), lim.reshape(DEPTH, 2, N_STILE, 128)], axis=2)
    return wb, wc, lam


def _group_mask(cols):
    row_g = lax.broadcasted_iota(jnp.int32, (256, cols), 0) // SSM_GROUP
    col = lax.broadcasted_iota(jnp.int32, (256, cols), 1)
    col_g = ((col // 256) % 8) * 2 + (col % 128) // SSM_STATE
    return row_g == col_g


def _s5_scan_kernel(u_ref, wb_ref, wc_ref, lam_ref, h0_ref, y_ref, hf_ref,
                    pm_scr, bm_scr, cm_scr, y_scr, state, ends, *group_scrs, segments):
    bu_scrs = group_scrs[:SSM_TILE_GROUPS]
    hs_scrs = group_scrs[SSM_TILE_GROUPS:]
    npass = 1 if segments == 1 else 2
    d = pl.program_id(0)
    ps = pl.program_id(1)
    ck = pl.program_id(2)
    nck = pl.num_programs(2)
    rows = SSM_TC * SSM_ROWS
    half = STATE_W // 2
    lam = lam_ref[...]

    @pl.when(jnp.logical_and(ps == 0, ck == 0))
    def _():
        a = lax.broadcasted_iota(jnp.int32, (rows, rows), 0)
        b = lax.broadcasted_iota(jnp.int32, (rows, rows), 1)
        same = jnp.logical_and(a // SSM_ROWS == b % SSM_TC, a % SSM_ROWS == b // SSM_TC)
        pm_scr[...] = jnp.where(same, 1.0, 0.0).astype(BF16)
        wb = jnp.tile(wb_ref[...], (256 // SSM_GROUP, 1))
        bm_scr[...] = jnp.where(_group_mask(STATE_W), wb, 0.0).astype(BF16)
        for n_ in range(2):
            wc = jnp.tile(wc_ref[:, n_ * half:(n_ + 1) * half], (256 // SSM_GROUP, 1))
            cm_scr[n_] = jnp.where(_group_mask(half), wc, 0.0).astype(BF16)

    @pl.when(ck == 0)
    def _():
        if npass == 1:
            state[...] = h0_ref[0]
        else:
            @pl.when(ps == 0)
            def _():
                state[...] = jnp.zeros_like(state)

            @pl.when(ps == 1)
            def _():
                ends[...] = state[...]
                state[...] = h0_ref[0]
                lr = lam[0:N_STILE]
                li = lam[N_STILE:]
                for _ in range(int(math.log2(SSM_SEG_LEN))):
                    lr, li = lr * lr - li * li, 2.0 * lr * li

                def carry(order, prev):
                    for sg in order:
                        for sq in range(SSM_ROWS // segments):
                            r = sq * segments + sg
                            q = r + prev
                            for j in range(N_STILE):
                                re = slice(j * 256, j * 256 + 128)
                                im = slice(j * 256 + 128, (j + 1) * 256)
                                pr = state[q:q + 1, re]
                                pi = state[q:q + 1, im]
                                ar = lr[j:j + 1]
                                ai = li[j:j + 1]
                                state[r:r + 1, re] = ends[q:q + 1, re] + ar * pr - ai * pi
                                state[r:r + 1, im] = ends[q:q + 1, im] + ar * pi + ai * pr

                @pl.when(d == 0)
                def _():
                    carry(range(1, segments), -1)

                @pl.when(d == 1)
                def _():
                    carry(range(segments - 2, -1, -1), 1)

    u = _dot(pm_scr[...], u_ref[...].reshape(rows, SSM_W).astype(BF16)).astype(BF16)

    def scan_step(emit):
        ngrp = len(bu_scrs)
        jt = N_STILE // ngrp
        gw = jt * 256

        def bproj(jb):
            k0 = (jb * jt // 8) * 256
            bu_scrs[jb][...] = _dot(u[:, k0:k0 + 256], bm_scr[:, jb * gw:(jb + 1) * gw])

        y_parts = [None, None]
        bproj(0)
        for jb in range(ngrp):
            if jb + 1 < ngrp:
                bproj(jb + 1)
            tiles = list(range(jb * jt, (jb + 1) * jt))
            bu, hs = bu_scrs[jb], hs_scrs[jb]
            lrs = [jnp.broadcast_to(lam[j:j + 1], (SSM_ROWS, 128)) for j in tiles]
            lis = [jnp.broadcast_to(lam[N_STILE + j:N_STILE + j + 1], (SSM_ROWS, 128)) for j in tiles]
            hr = [state[:, j * 256:j * 256 + 128] for j in tiles]
            hi = [state[:, j * 256 + 128:(j + 1) * 256] for j in tiles]
            for t in range(SSM_TC):
                tt = t + d * (SSM_TC - 1 - 2 * t)
                r = pl.ds(pl.multiple_of(tt * SSM_ROWS, SSM_ROWS), SSM_ROWS)
                for n_ in range(jt):
                    re = slice(n_ * 256, n_ * 256 + 128)
                    im = slice(n_ * 256 + 128, (n_ + 1) * 256)
                    nr = lrs[n_] * hr[n_] - lis[n_] * hi[n_] + bu[r, re]
                    ni = lrs[n_] * hi[n_] + lis[n_] * hr[n_] + bu[r, im]
                    if emit:
                        hs[r, re] = nr
                        hs[r, im] = ni
                    hr[n_], hi[n_] = nr, ni
            for n_, j in enumerate(tiles):
                state[:, j * 256:j * 256 + 128] = hr[n_]
                state[:, j * 256 + 128:(j + 1) * 256] = hi[n_]
            if emit:
                n_ = jb * gw // half
                off = jb * gw % half
                part = _dot_nt(hs[...].astype(BF16), cm_scr[n_, :, off:off + gw])
                y_parts[n_] = part if y_parts[n_] is None else y_parts[n_] + part
        if emit:
            for n_ in range(2):
                y_scr[2 * n_] = y_parts[n_][:, :128]
                y_scr[2 * n_ + 1] = y_parts[n_][:, 128:]
            for r in range(SSM_ROWS):
                for c_ in range(SSM_W // 128):
                    y_ref[r, :, c_ * 128:(c_ + 1) * 128] = y_scr[c_, pl.ds(r, SSM_TC, stride=SSM_ROWS), :]

    if npass == 1:
        scan_step(True)
    else:
        @pl.when(ps == 0)
        def _():
            scan_step(False)

        @pl.when(ps == 1)
        def _():
            scan_step(True)

    @pl.when(jnp.logical_and(ps == npass - 1, ck == nck - 1))
    def _():
        hf_ref[0] = state[...]


def _s5_scan(p3, row_block, wb, wc, lam, h0, l, *, segments):
    steps = p3.shape[1]
    nck = steps // SSM_TC
    npass = 1 if segments == 1 else 2

    def chunk(d, c):
        return c + d * (nck - 1 - 2 * c)

    def y_chunk(d, p, c):
        return jnp.where(p == npass - 1, chunk(d, c), chunk(d, 0))

    par = pl.BlockSpec((None, None, SSM_GROUP, STATE_W), lambda d, p, c: (l, d, 0, 0))
    st = pl.BlockSpec((1, SSM_ROWS, STATE_W), lambda d, p, c: (d, 0, 0))
    return pl.pallas_call(
        functools.partial(_s5_scan_kernel, segments=segments),
        grid=(2, npass, nck),
        in_specs=[
            pl.BlockSpec((SSM_ROWS, SSM_TC, SSM_W), lambda d, p, c: (row_block, chunk(d, c), COL_U)),
            par, par,
            pl.BlockSpec((None, None, 2 * N_STILE, 128), lambda d, p, c: (l, d, 0, 0)),
            st,
        ],
        out_specs=[
            pl.BlockSpec((None, SSM_ROWS, SSM_TC, SSM_W), lambda d, p, c: (d, 0, y_chunk(d, p, c), 0)),
            st,
        ],
        out_shape=[
            jax.ShapeDtypeStruct((2, SSM_ROWS, steps, SSM_W), F32),
            jax.ShapeDtypeStruct((2, SSM_ROWS, STATE_W), F32),
        ],
        scratch_shapes=[
            pltpu.VMEM((SSM_TC * SSM_ROWS, SSM_TC * SSM_ROWS), BF16),
            pltpu.VMEM((256, STATE_W), BF16),
            pltpu.VMEM((2, 256, STATE_W // 2), BF16),
            pltpu.VMEM((SSM_W // 128, SSM_TC * SSM_ROWS, 128), F32),
            pltpu.VMEM((SSM_ROWS, STATE_W), F32),
            pltpu.VMEM((SSM_ROWS, STATE_W), F32),
        ] + [pltpu.VMEM((SSM_TC * SSM_ROWS, STATE_W // SSM_TILE_GROUPS), F32)] * (2 * SSM_TILE_GROUPS),
        compiler_params=_cparams(("arbitrary", "arbitrary", "arbitrary")),
        name="s5_scan_seg%d" % segments,
    )(p3, wb, wc, lam, h0)


def _s5_glu_kernel(ypf_ref, ypb_ref, ysf_ref, ysb_ref, u_ref, d_ref, w_ref, o_ref):
    i = pl.program_id(0)
    w = w_ref[...].astype(BF16)

    def glu(y):
        y = y + d_ref[...] * u_ref[...]
        z = _dot(jax.nn.gelu(y).astype(BF16), w)
        o_ref[...] = (z[:, :SSM_W] * _sigmoid(z[:, SSM_W:])).astype(BF16)

    @pl.when(i < N_PT)
    def _():
        glu(ypf_ref[...] + ypb_ref[...])

    @pl.when(i >= N_PT)
    def _():
        glu(ysf_ref[...] + ysb_ref[...])


def _s5_glu(yp, ys, p, d, w, l):
    return pl.pallas_call(
        _s5_glu_kernel,
        grid=(N_MT,),
        in_specs=[
            pl.BlockSpec((None, TM, SSM_W), lambda i: (0, _prompt_idx(i), 0)),
            pl.BlockSpec((None, TM, SSM_W), lambda i: (1, _prompt_idx(i), 0)),
            pl.BlockSpec((None, TM, SSM_W), lambda i: (0, _sample_idx(i), 0)),
            pl.BlockSpec((None, TM, SSM_W), lambda i: (1, _sample_idx(i), 0)),
            pl.BlockSpec((TM, SSM_W), lambda i: (i, COL_U)),
            pl.BlockSpec((None, 1, SSM_W), lambda i: (l, 0, 0)),
            pl.BlockSpec((None, SSM_W, 2 * SSM_W), lambda i: (l, 0, 0)),
        ],
        out_specs=pl.BlockSpec((TM, SSM_W), lambda i: (i, 0)),
        out_shape=jax.ShapeDtypeStruct((N_TOK, SSM_W), BF16),
        compiler_params=_cparams(("arbitrary",)),
        name="s5_glu",
    )(yp, yp, ys, ys, p, d.reshape(DEPTH, 1, SSM_W), w)


def _s5_layer(p, s5p, st_l, ssm_d, w_ssm_glu, l):
    wb, wc, lam = s5p
    yp, hfin = _s5_scan(p.reshape(N_TOK // SEQ, SEQ, IN_W), 0, wb, wc, lam,
                        jnp.zeros((2, SSM_ROWS, STATE_W), F32), l, segments=1)
    h0 = _state_cols(jnp.moveaxis(st_l, 0, 1))
    h0 = jnp.stack([
        jnp.zeros((DEC_BATCH, SSM_SEGS, STATE_W), F32).at[:, 0].set(h0[0]),
        jnp.zeros((DEC_BATCH, SSM_SEGS, STATE_W), F32).at[:, SSM_SEGS - 1].set(h0[1]),
    ]).reshape(2, SSM_ROWS, STATE_W)
    ys, _ = _s5_scan(p.reshape(N_TOK // SSM_SEG_LEN, SSM_SEG_LEN, IN_W), N_PROMPT // SSM_SEG_LEN // SSM_ROWS,
                     wb, wc, lam, h0, l, segments=SSM_SEGS)
    out = _s5_glu(yp.reshape(2, N_PROMPT, SSM_W), ys.reshape(2, N_SAMPLE, SSM_W), p, ssm_d, w_ssm_glu, l)
    new_state = jnp.moveaxis(_state_uncols(hfin), 0, 1)
    return out, new_state


def _rope_tables():
    rows = DEC_SEQ // GRID_W
    r = jnp.repeat(jnp.arange(rows), GRID_W).astype(F32)
    col = jnp.tile(jnp.arange(GRID_W), rows).astype(F32)
    inv = ROPE_BASE ** (-jnp.arange(ROPE_FREQS, dtype=F32) / ROPE_FREQS)
    ang = jnp.stack([r, col], axis=-1)[:, :, None] * inv
    cos = jnp.cos(ang)
    sin = jnp.sin(ang)
    cos_t = jnp.concatenate([cos, cos], axis=-1).reshape(DEC_SEQ, HEAD_DIM)
    sin_t = jnp.concatenate([-sin, sin], axis=-1).reshape(DEC_SEQ, HEAD_DIM)
    return cos_t, sin_t


def kernel(x_prompt, x_sample, cache_k, cache_v, state_ssm, state_ret, c, c_ctx, w_mod, b_mod, norm_g,
           w_ffn_in, w_ffn_out, w_in, w_out, q_norm_g, k_norm_g, ssm_a_re, ssm_a_im, ssm_log_dt,
           ssm_b_re, ssm_b_im, ssm_c_re, ssm_c_im, ssm_d, w_ssm_glu, ret_decay_logit, final_norm_g):
    x = (x_prompt.reshape(N_PROMPT, D_MODEL), x_sample.reshape(N_SAMPLE, D_MODEL))
    cond8 = jnp.concatenate([c_ctx[None], c, jnp.zeros((5, D_MODEL), F32)], axis=0)
    mods = _adaln(cond8, w_mod, b_mod).reshape(DEPTH, 8, N_SUB * 3, D_MODEL)
    modt = jnp.transpose(mods[:, jnp.array(TILE_MOD, jnp.int32)], (0, 2, 1, 3))[:, :, :, None, :]
    gains = norm_g.reshape(DEPTH * N_SUB, 1, D_MODEL)
    cos_t, sin_t = _rope_tables()
    lre, lim, bbre, bbim, ncim = _s5_discretize(ssm_a_re, ssm_a_im, ssm_log_dt, ssm_b_re, ssm_b_im, ssm_c_im)
    s5p = _s5_compact(lre, lim, bbre, bbim, ssm_c_re, ncim)
    dl = jnp.broadcast_to(ret_decay_logit[:, :, :, None, None], (DEPTH, 2, RET_HEADS, 8, HEAD_DIM))
    ks_, vs_, hs_, ss_ = [], [], [], []
    for l in range(DEPTH):
        x = _ffn(x, gains, modt, w_ffn_in, w_ffn_out, l, 0)
        p = _inproj(x, gains, modt, w_in, l)
        attn, k_l, v_l = _attention(p, cache_k, cache_v, q_norm_g, k_norm_g, cos_t, sin_t, l)
        ssm, h_l = _s5_layer(p, s5p, state_ssm[:, l], ssm_d, w_ssm_glu, l)
        ret, s_l = _retention(p, dl, cos_t, sin_t, state_ret, l)
        x = _outproj(x, attn, ssm, ret, modt, w_out, l)
        x = _ffn(x, gains, modt, w_ffn_in, w_ffn_out, l, 1, final_g=final_norm_g if l == DEPTH - 1 else None)
        ks_.append(k_l.reshape(BATCH, SEQ, N_KV_HEADS, HEAD_DIM))
        vs_.append(v_l.reshape(BATCH, SEQ, N_KV_HEADS, HEAD_DIM))
        hs_.append(h_l)
        ss_.append(s_l)
    y_prompt = x[0].reshape(BATCH, SEQ, D_MODEL)
    y_sample = x[1].reshape(DEC_BATCH, DEC_SEQ, D_MODEL)
    return (y_prompt, y_sample, jnp.stack(ks_, axis=1), jnp.stack(vs_, axis=1),
            jnp.stack(hs_, axis=1), jnp.stack(ss_, axis=1))
```

```python
import functools
import math

import jax
import jax.numpy as jnp
from jax import lax
from jax.experimental import pallas as pl
from jax.experimental.pallas import tpu as pltpu

D_MODEL = 2048
BATCH = 16
SEQ = 256
DEPTH = 2
DEC_BATCH = 2
DEC_SEQ = 1024
PAST_LEN = 512
GRID_W = 64
HEAD_DIM = 128
N_Q_HEADS = 8
N_KV_HEADS = 2
Q_PER_KV = 4
ATTN_W = 1024
KV_W = 256
SSM_W = 512
SSM_GROUP = 16
SSM_GROUPS = 32
SSM_STATE = 64
RET_HEADS = 4
RET_W = 512
IN_W = 4096
D_FF = 5632
N_SUB = 3
RET_CHUNK = 128
ROPE_BASE = 10000.0
ROPE_FREQS = 32
EPS = 1e-6

N_PROMPT = BATCH * SEQ
N_SAMPLE = DEC_BATCH * DEC_SEQ
N_TOK = N_PROMPT + N_SAMPLE
TM = 1024
N_MT = N_TOK // TM
N_PT = N_PROMPT // TM
SEQ_PER_TILE = TM // SEQ
TILE_MOD = (0, 0, 0, 0, 1, 2)
ROW_CHUNK = 256
TF = 256
FFN_NSPLIT = 4
TN = 512
STATE_W = 2 * SSM_GROUPS * SSM_STATE
N_STILE = STATE_W // 256
SSM_ROWS = 16
SSM_SEGS = 8
SSM_SEG_LEN = DEC_SEQ // SSM_SEGS
SSM_TC = 32
SSM_TILE_GROUPS = 4
ATT_TQ = 256
ATT_S = PAST_LEN + DEC_SEQ
VMEM_LIMIT = 56 * 1024 * 1024
VMEM_LIMIT_FFN = 60 * 1024 * 1024

COL_K = ATTN_W // KV_W
COL_V = COL_K + 1
COL_U = (ATTN_W + 2 * KV_W) // SSM_W
COL_R = (ATTN_W + 2 * KV_W + SSM_W) // RET_W

BF16 = jnp.bfloat16
F32 = jnp.float32


def _cparams(sem, limit=VMEM_LIMIT):
    return pltpu.CompilerParams(dimension_semantics=sem, vmem_limit_bytes=limit)


def _dot(a, b):
    return jnp.dot(a, b, preferred_element_type=F32)


def _dot_nt(a, b):
    return lax.dot_general(a, b, (((1,), (1,)), ((), ())), preferred_element_type=F32)


def _sigmoid(x):
    return 1.0 / (1.0 + jnp.exp(-x))


def _silu(x):
    return x * _sigmoid(x)


def _sample_idx(i):
    return jnp.clip(i - N_PT, 0, DEC_BATCH - 1)


def _prompt_idx(i):
    return jnp.minimum(i, N_PT - 1)


def _row_chunks(n_rows, body):
    def step(c, carry):
        body(pl.ds(pl.multiple_of(c * ROW_CHUNK, ROW_CHUNK), ROW_CHUNK))
        return carry

    lax.fori_loop(0, n_rows // ROW_CHUNK, step, 0)


def _mod_spec(l, k, width=D_MODEL, col=lambda j: 0, tile=lambda i: i):
    return pl.BlockSpec((None, None, 1, 1, width), lambda i, j: (l, k, tile(i), 0, col(j)))


def _gain_spec(l, sub):
    return pl.BlockSpec((None, 1, D_MODEL), lambda i, j: (l * N_SUB + sub, 0, 0))


def _adaln_kernel(c_ref, w_ref, b_ref, o_ref):
    a = _silu(c_ref[...]).astype(BF16)
    o_ref[0] = _dot(a, w_ref[0].astype(BF16)) + b_ref[0]


def _adaln(cond8, w_mod, b_mod):
    n = w_mod.shape[-1]
    tn = 1024
    return pl.pallas_call(
        _adaln_kernel,
        grid=(DEPTH, n // tn),
        in_specs=[
            pl.BlockSpec((8, D_MODEL), lambda l, j: (0, 0)),
            pl.BlockSpec((1, D_MODEL, tn), lambda l, j: (l, 0, j)),
            pl.BlockSpec((1, 1, tn), lambda l, j: (l, 0, j)),
        ],
        out_specs=pl.BlockSpec((1, 8, tn), lambda l, j: (l, 0, j)),
        out_shape=jax.ShapeDtypeStruct((DEPTH, 8, n), F32),
        compiler_params=_cparams(("arbitrary", "arbitrary")),
        name="adaln",
    )(cond8, w_mod, b_mod.reshape(DEPTH, 1, n))


def _norm_mod(x, g, shift, scale):
    ms = jnp.mean(x * x, axis=-1, keepdims=True)
    return x * lax.rsqrt(ms + EPS) * (g * (1.0 + scale)) + shift


def _ffn_kernel(*refs, n_x, n_out, final):
    refs = list(refs)
    x_refs = refs[:n_x]
    g_ref, sh_ref, sc_ref, gt_ref, wa_ref, wu_ref, wo_ref = refs[n_x:n_x + 7]
    rest = refs[n_x + 7:]
    fg_ref = rest.pop(0) if final else None
    o_refs, h_scr = rest[:n_out], rest[n_out]
    i = pl.program_id(0)
    j = pl.program_id(1)

    def run(x_ref, o_ref):
        @pl.when(j == 0)
        def _():
            def pre(rows):
                h = _norm_mod(x_ref[rows, :], g_ref[...], sh_ref[0], sc_ref[0])
                h_scr[rows, :] = h.astype(BF16)

            _row_chunks(TM, pre)
            o_ref[...] = jnp.zeros_like(o_ref)

        h = h_scr[...]
        a = _dot(h, wa_ref[...].astype(BF16))
        u = _dot(h, wu_ref[...].astype(BF16))
        mid = (_silu(a) * u).astype(BF16)
        wn = D_MODEL // FFN_NSPLIT
        for n in range(FFN_NSPLIT):
            cs = slice(n * wn, (n + 1) * wn)
            o_ref[:, cs] += _dot(mid, wo_ref[:, cs].astype(BF16))

        @pl.when(j == pl.num_programs(1) - 1)
        def _():
            def post(rows):
                out = x_ref[rows, :] + (0.5 * gt_ref[0]) * o_ref[rows, :]
                if final:
                    ms = jnp.mean(out * out, axis=-1, keepdims=True)
                    out = out * lax.rsqrt(ms + EPS) * fg_ref[...]
                o_ref[rows, :] = out

            _row_chunks(TM, post)

    if n_x == 1 and n_out == 1:
        run(x_refs[0], o_refs[0])
    else:
        @pl.when(i < N_PT)
        def _():
            run(x_refs[0], o_refs[0])

        @pl.when(i >= N_PT)
        def _():
            run(x_refs[-1], o_refs[-1])


def _split_tok_specs():
    return [pl.BlockSpec((TM, D_MODEL), lambda i, j: (_prompt_idx(i), 0)),
            pl.BlockSpec((TM, D_MODEL), lambda i, j: (_sample_idx(i), 0), pipeline_mode=pl.Buffered(1))]


def _ffn(x, gains, modt, w_in, w_out, l, f, final_g=None):
    final = final_g is not None
    xs = list(x) if isinstance(x, (tuple, list)) else [x]
    sub = 2 * f
    nf = D_FF // TF
    tok = pl.BlockSpec((TM, D_MODEL), lambda i, j: (i, 0))
    in_specs = (_split_tok_specs() if len(xs) == 2 else [tok]) + [
        _gain_spec(l, sub), _mod_spec(l, 3 * sub), _mod_spec(l, 3 * sub + 1), _mod_spec(l, 3 * sub + 2),
        pl.BlockSpec((None, None, D_MODEL, TF), lambda i, j: (l, f, 0, j)),
        pl.BlockSpec((None, None, D_MODEL, TF), lambda i, j: (l, f, 0, j + nf)),
        pl.BlockSpec((None, None, TF, D_MODEL), lambda i, j: (l, f, j, 0)),
    ]
    args = xs + [gains, modt, modt, modt, w_in, w_in, w_out]
    if final:
        in_specs.append(pl.BlockSpec((1, D_MODEL), lambda i, j: (0, 0)))
        args.append(final_g.reshape(1, D_MODEL))
        out_specs = _split_tok_specs()
        out_shape = [jax.ShapeDtypeStruct((N_PROMPT, D_MODEL), F32), jax.ShapeDtypeStruct((N_SAMPLE, D_MODEL), F32)]
    else:
        out_specs = tok
        out_shape = jax.ShapeDtypeStruct((N_TOK, D_MODEL), F32)
    return pl.pallas_call(
        functools.partial(_ffn_kernel, n_x=len(xs), n_out=2 if final else 1, final=final),
        grid=(N_MT, nf),
        in_specs=in_specs,
        out_specs=out_specs,
        out_shape=out_shape,
        scratch_shapes=[pltpu.VMEM((TM, D_MODEL), BF16)],
        compiler_params=_cparams(("arbitrary", "arbitrary"), VMEM_LIMIT_FFN),
        name="ffn_final" if final else ("ffn_first" if len(xs) == 2 else "ffn"),
    )(*args)


def _resident_weight_spec(rows, l, ncol):
    return pl.BlockSpec((None, rows, TN), lambda i, j: (l, 0, jnp.where(i == 0, j, ncol - 1)))


INPROJ_CHUNK = TM // (IN_W // TN)


def _inproj_kernel(x_ref, g_ref, sh_ref, sc_ref, shn_ref, scn_ref, w_ref, o_ref, h_a, h_b, w_scr):
    i = pl.program_id(0)
    j = pl.program_id(1)
    nchunk = TM // INPROJ_CHUNK

    def norm_chunk(c, sh_r, sc_r, dst):
        rows = pl.ds(pl.multiple_of(c * INPROJ_CHUNK, INPROJ_CHUNK), INPROJ_CHUNK)
        h = _norm_mod(x_ref[rows, :], g_ref[...], sh_r[0], sc_r[0])
        dst[rows, :] = h.astype(BF16)

    def step(h_cur, h_nxt):
        @pl.when(j == 0)
        def _():
            @pl.when(i == 0)
            def _():
                def first(c, carry):
                    norm_chunk(c, sh_ref, sc_ref, h_cur)
                    return carry

                lax.fori_loop(0, nchunk - 1, first, 0)

            norm_chunk(nchunk - 1, sh_ref, sc_ref, h_cur)

        @pl.when(i == 0)
        def _():
            w_scr[j] = w_ref[...].astype(BF16)

        norm_chunk((j + nchunk - 1) % nchunk, shn_ref, scn_ref, h_nxt)
        o_ref[...] = _dot(h_cur[...], w_scr[j])

    @pl.when(i % 2 == 0)
    def _():
        step(h_a, h_b)

    @pl.when(i % 2 == 1)
    def _():
        step(h_b, h_a)


def _inproj(x, gains, modt, w, l):
    ncol = IN_W // TN

    def nxt(i):
        return jnp.minimum(i + 1, N_MT - 1)

    return pl.pallas_call(
        _inproj_kernel,
        grid=(N_MT, ncol),
        in_specs=[
            pl.BlockSpec((TM, D_MODEL), lambda i, j: (jnp.where(j == 0, i, nxt(i)), 0)),
            _gain_spec(l, 1), _mod_spec(l, 3), _mod_spec(l, 4), _mod_spec(l, 3, tile=nxt), _mod_spec(l, 4, tile=nxt),
            _resident_weight_spec(D_MODEL, l, ncol),
        ],
        out_specs=pl.BlockSpec((TM, TN), lambda i, j: (i, j)),
        out_shape=jax.ShapeDtypeStruct((N_TOK, IN_W), F32),
        scratch_shapes=[pltpu.VMEM((TM, D_MODEL), BF16), pltpu.VMEM((TM, D_MODEL), BF16),
                        pltpu.VMEM((ncol, D_MODEL, TN), BF16)],
        compiler_params=_cparams(("arbitrary", "arbitrary"), VMEM_LIMIT_FFN),
        name="inproj",
    )(x, gains, modt, modt, modt, modt, w)


def _outproj_kernel(x_ref, a_ref, s_ref, r_ref, gt_ref, w_ref, o_ref, w_scr):
    i = pl.program_id(0)
    j = pl.program_id(1)

    @pl.when(i == 0)
    def _():
        w_scr[j] = w_ref[...].astype(BF16)

    y = _dot(a_ref[...], w_scr[j, 0:ATTN_W, :])
    y += _dot(s_ref[...], w_scr[j, ATTN_W:ATTN_W + SSM_W, :])
    y += _dot(r_ref[...], w_scr[j, ATTN_W + SSM_W:, :])
    o_ref[...] = x_ref[...] + gt_ref[0] * y


def _outproj(x, attn, ssm, ret, modt, w, l):
    ncol = D_MODEL // TN
    return pl.pallas_call(
        _outproj_kernel,
        grid=(N_MT, ncol),
        in_specs=[
            pl.BlockSpec((TM, TN), lambda i, j: (i, j)),
            pl.BlockSpec((TM, ATTN_W), lambda i, j: (i, 0)),
            pl.BlockSpec((TM, SSM_W), lambda i, j: (i, 0)),
            pl.BlockSpec((TM, RET_W), lambda i, j: (i, 0)),
            _mod_spec(l, 5, TN, lambda j: j),
            _resident_weight_spec(D_MODEL, l, ncol),
        ],
        out_specs=pl.BlockSpec((TM, TN), lambda i, j: (i, j)),
        out_shape=jax.ShapeDtypeStruct((N_TOK, D_MODEL), F32),
        scratch_shapes=[pltpu.VMEM((ncol, D_MODEL, TN), BF16)],
        compiler_params=_cparams(("arbitrary", "arbitrary")),
        name="outproj",
    )(x, attn, ssm, ret, modt, w)


def _head_rms(x, g):
    ms = jnp.mean(x * x, axis=-1, keepdims=True)
    return x * lax.rsqrt(ms + EPS) * g


def _rope(x, cos, sin_signed):
    lane = lax.broadcasted_iota(jnp.int32, x.shape, 1)
    partner = jnp.where((lane % 64) < 32, pltpu.roll(x, 96, 1), pltpu.roll(x, 32, 1))
    return x * cos + partner * sin_signed


SOFTMAX_EXP2_SCALE = HEAD_DIM ** -0.5 * math.log2(math.e)


def _attend_short(qs, k, v):
    s = _dot_nt(qs.astype(BF16), k)
    p = jnp.exp2((s - jnp.max(s, axis=-1, keepdims=True)) * SOFTMAX_EXP2_SCALE)
    l = jnp.sum(p, axis=-1, keepdims=True)
    return _dot(p.astype(BF16), v) / l


def _attend(qs, k, vt):
    st = _dot_nt(k, qs.astype(BF16))
    p = jnp.exp2((st - jnp.max(st, axis=0, keepdims=True)) * SOFTMAX_EXP2_SCALE)
    l = jnp.sum(p, axis=0, keepdims=True)
    return (_dot(vt, p.astype(BF16)) / l).T


def _attn_kernel(q_ref, k_ref, v_ref, ck_ref, cv_ref, qg_ref, kg_ref, cos_ref, sin_ref,
                 o_ref, kc_ref, vc_ref, k_scr, vt_scr):
    i = pl.program_id(0)
    qg = qg_ref[...]
    kg = kg_ref[...]

    def heads(kv):
        return [slice((kv * Q_PER_KV + g) * HEAD_DIM, (kv * Q_PER_KV + g + 1) * HEAD_DIM) for g in range(Q_PER_KV)]

    @pl.when(i < N_PT)
    def _():
        def seq_body(sq, carry):
            rows = pl.ds(pl.multiple_of(sq * SEQ, SEQ), SEQ)
            v = v_ref[rows, :]
            for kv_ in range(N_KV_HEADS):
                vc_ref[sq, :, kv_, :] = v[:, kv_ * HEAD_DIM:(kv_ + 1) * HEAD_DIM]
            for kv in range(N_KV_HEADS):
                ksl = slice(kv * HEAD_DIM, (kv + 1) * HEAD_DIM)
                kn = _head_rms(k_ref[rows, ksl], kg)
                kc_ref[sq, :, kv, :] = kn
                qs = jnp.concatenate([_head_rms(q_ref[rows, hs], qg) for hs in heads(kv)], axis=0)
                o = _attend_short(qs, kn.astype(BF16), v[:, ksl].astype(BF16))
                for g, hs in enumerate(heads(kv)):
                    o_ref[rows, hs] = o[g * SEQ:(g + 1) * SEQ].astype(BF16)
            return carry

        lax.fori_loop(0, SEQ_PER_TILE, seq_body, 0)

    @pl.when(i >= N_PT)
    def _():
        k_scr[0:PAST_LEN, :] = ck_ref[...].astype(BF16)
        for kv in range(N_KV_HEADS):
            ksl = slice(kv * HEAD_DIM, (kv + 1) * HEAD_DIM)
            vt_scr[kv, :, 0:PAST_LEN] = cv_ref[:, ksl].T.astype(BF16)
            vt_scr[kv, :, PAST_LEN:] = v_ref[:, ksl].T.astype(BF16)
            kn = _rope(_head_rms(k_ref[:, ksl], kg), cos_ref[...], sin_ref[...])
            k_scr[PAST_LEN:, ksl] = kn.astype(BF16)

        def q_body(qb, carry):
            rows = pl.ds(pl.multiple_of(qb * ATT_TQ, ATT_TQ), ATT_TQ)
            cos = cos_ref[rows, :]
            sin = sin_ref[rows, :]
            for kv in range(N_KV_HEADS):
                ksl = slice(kv * HEAD_DIM, (kv + 1) * HEAD_DIM)
                qs = jnp.concatenate([_rope(_head_rms(q_ref[rows, hs], qg), cos, sin) for hs in heads(kv)], axis=0)
                o = _attend(qs, k_scr[:, ksl], vt_scr[kv])
                for g, hs in enumerate(heads(kv)):
                    o_ref[rows, hs] = o[g * ATT_TQ:(g + 1) * ATT_TQ].astype(BF16)
            return carry

        lax.fori_loop(0, DEC_SEQ // ATT_TQ, q_body, 0)


def _attention(p, cache_k, cache_v, qg, kg, cos_t, sin_t, l):
    cache_spec = pl.BlockSpec((None, None, PAST_LEN, KV_W), lambda i: (_sample_idx(i), l, 0, 0))
    vec = pl.BlockSpec((None, 1, HEAD_DIM), lambda i: (l, 0, 0))
    tab = pl.BlockSpec((DEC_SEQ, HEAD_DIM), lambda i: (0, 0))
    new_cache = pl.BlockSpec((SEQ_PER_TILE, SEQ, N_KV_HEADS, HEAD_DIM), lambda i: (_prompt_idx(i), 0, 0, 0))
    return pl.pallas_call(
        _attn_kernel,
        grid=(N_MT,),
        in_specs=[
            pl.BlockSpec((TM, ATTN_W), lambda i: (i, 0)),
            pl.BlockSpec((TM, KV_W), lambda i: (i, COL_K)),
            pl.BlockSpec((TM, KV_W), lambda i: (i, COL_V)),
            cache_spec, cache_spec, vec, vec, tab, tab,
        ],
        out_specs=[pl.BlockSpec((TM, ATTN_W), lambda i: (i, 0)), new_cache, new_cache],
        out_shape=[
            jax.ShapeDtypeStruct((N_TOK, ATTN_W), BF16),
            jax.ShapeDtypeStruct((BATCH, SEQ, N_KV_HEADS, HEAD_DIM), F32),
            jax.ShapeDtypeStruct((BATCH, SEQ, N_KV_HEADS, HEAD_DIM), F32),
        ],
        scratch_shapes=[pltpu.VMEM((ATT_S, KV_W), BF16), pltpu.VMEM((N_KV_HEADS, HEAD_DIM, ATT_S), BF16)],
        compiler_params=_cparams(("arbitrary",)),
        name="attention",
    )(p, p, p, cache_k.reshape(DEC_BATCH, DEPTH, PAST_LEN, KV_W), cache_v.reshape(DEC_BATCH, DEPTH, PAST_LEN, KV_W),
      qg.reshape(DEPTH, 1, HEAD_DIM), kg.reshape(DEPTH, 1, HEAD_DIM), cos_t, sin_t)


def _log_sigmoid(x):
    return -(jnp.maximum(-x, 0.0) + jnp.log(1.0 + jnp.exp(-jnp.abs(x))))


def _decay_mask(qi, kj, lg_f, lg_b):
    dd = qi - kj
    log2e = math.log2(math.e)
    w = jnp.exp2(dd * jnp.where(dd > 0, lg_f * log2e, -lg_b * log2e))
    return jnp.where(dd == 0, 2.0, w)


def _retention_kernel(q_ref, k_ref, v_ref, g_ref, dl_ref, cos_ref, sin_ref, s0_ref, o_ref, sf_ref,
                      acc, qr_scr, kr_scr, vb_scr):
    i = pl.program_id(0)
    scale = HEAD_DIM ** -0.5
    lg = [[_log_sigmoid(dl_ref[d, h][0:1, :]) for h in range(RET_HEADS)] for d in range(2)]
    lg1 = [[lg[d][h][:, 0:1] for h in range(RET_HEADS)] for d in range(2)]
    hsl = [slice(h * HEAD_DIM, (h + 1) * HEAD_DIM) for h in range(RET_HEADS)]

    @pl.when(i < N_PT)
    def _():
        qi = lax.broadcasted_iota(jnp.int32, (SEQ, SEQ), 0).astype(F32)
        kj = lax.broadcasted_iota(jnp.int32, (SEQ, SEQ), 1).astype(F32)
        pos = lax.broadcasted_iota(jnp.int32, (SEQ, HEAD_DIM), 0).astype(F32)
        masks = [_decay_mask(qi, kj, lg1[0][h], lg1[1][h]) for h in range(RET_HEADS)]
        kdec_f = [jnp.exp((SEQ - 1.0 - pos) * lg[0][h]) for h in range(RET_HEADS)]
        kdec_b = [jnp.exp(pos * lg[1][h]) for h in range(RET_HEADS)]

        def seq_body(sq, carry):
            rows = pl.ds(pl.multiple_of(sq * SEQ, SEQ), SEQ)
            for h in range(RET_HEADS):
                q = q_ref[rows, hsl[h]].astype(BF16)
                k = k_ref[rows, hsl[h]] * scale
                v = v_ref[rows, hsl[h]].astype(BF16)
                a = (_dot_nt(q, k.astype(BF16)) * masks[h]).astype(BF16)
                acc[rows, hsl[h]] = _dot(a, v)
                sf_ref[sq, 0, h] = _dot((k * kdec_f[h]).T.astype(BF16), v)
                sf_ref[sq, 1, h] = _dot((k * kdec_b[h]).T.astype(BF16), v)
            return carry

        lax.fori_loop(0, SEQ_PER_TILE, seq_body, 0)

    @pl.when(i >= N_PT)
    def _():
        for h in range(RET_HEADS):
            qr_scr[:, hsl[h]] = _rope(q_ref[:, hsl[h]], cos_ref[...], sin_ref[...]).astype(BF16)
            kr_scr[:, hsl[h]] = _rope(k_ref[:, hsl[h]] * scale, cos_ref[...], sin_ref[...]).astype(BF16)
        vb_scr[...] = v_ref[...].astype(BF16)

        def q_body(qb, carry):
            row0 = qb * ATT_TQ
            rows = pl.ds(pl.multiple_of(row0, ATT_TQ), ATT_TQ)
            qi = (row0 + lax.broadcasted_iota(jnp.int32, (ATT_TQ, DEC_SEQ), 0)).astype(F32)
            kj = lax.broadcasted_iota(jnp.int32, (ATT_TQ, DEC_SEQ), 1).astype(F32)
            pos = (row0 + lax.broadcasted_iota(jnp.int32, (ATT_TQ, HEAD_DIM), 0)).astype(F32)
            for h in range(RET_HEADS):
                q = qr_scr[rows, hsl[h]]
                a = (_dot_nt(q, kr_scr[:, hsl[h]]) * _decay_mask(qi, kj, lg1[0][h], lg1[1][h])).astype(BF16)
                o = _dot(a, vb_scr[:, hsl[h]])
                o += _dot(q, s0_ref[0, h].astype(BF16)) * jnp.exp((pos + 1.0) * lg[0][h])
                o += _dot(q, s0_ref[1, h].astype(BF16)) * jnp.exp((DEC_SEQ - pos) * lg[1][h])
                acc[rows, hsl[h]] = o
            return carry

        lax.fori_loop(0, DEC_SEQ // ATT_TQ, q_body, 0)

    for h in range(RET_HEADS):
        o = acc[:, hsl[h]]
        o = o - jnp.mean(o, axis=-1, keepdims=True)
        o = o * lax.rsqrt(jnp.mean(o * o, axis=-1, keepdims=True) + EPS)
        o_ref[:, hsl[h]] = (o * _silu(g_ref[:, hsl[h]])).astype(BF16)


def _retention(p, dl, cos_t, sin_t, state_ret, l):
    tab = pl.BlockSpec((DEC_SEQ, HEAD_DIM), lambda i: (0, 0))
    st = (2, RET_HEADS, HEAD_DIM, HEAD_DIM)
    return pl.pallas_call(
        _retention_kernel,
        grid=(N_MT,),
        in_specs=[pl.BlockSpec((TM, RET_W), lambda i, k=k: (i, COL_R + k)) for k in range(4)] + [
            pl.BlockSpec((None, 2, RET_HEADS, 8, HEAD_DIM), lambda i: (l, 0, 0, 0, 0)),
            tab, tab,
            pl.BlockSpec((None, None) + st, lambda i: (_sample_idx(i), l, 0, 0, 0, 0)),
        ],
        out_specs=[
            pl.BlockSpec((TM, RET_W), lambda i: (i, 0)),
            pl.BlockSpec((SEQ_PER_TILE,) + st, lambda i: (_prompt_idx(i), 0, 0, 0, 0)),
        ],
        out_shape=[
            jax.ShapeDtypeStruct((N_TOK, RET_W), BF16),
            jax.ShapeDtypeStruct((BATCH,) + st, F32),
        ],
        scratch_shapes=[pltpu.VMEM((TM, RET_W), F32)] + [pltpu.VMEM((TM, RET_W), BF16)] * 3,
        compiler_params=_cparams(("arbitrary",)),
        name="retention",
    )(p, p, p, p, dl, cos_t, sin_t, state_ret)


def _s5_disc_kernel(are_ref, aim_ref, ldt_ref, bre_ref, bim_ref, cim_ref,
                    lre_ref, lim_ref, bbre_ref, bbim_ref, ncim_ref):
    ar = are_ref[...]
    ai = aim_ref[...]
    dt = jnp.exp(ldt_ref[...])
    mag = jnp.exp(ar * dt)
    lr = mag * jnp.cos(ai * dt)
    li = mag * jnp.sin(ai * dt)
    den = ar * ar + ai * ai
    nr = lr - 1.0
    cr = (nr * ar + li * ai) / den
    ci = (li * ar - nr * ai) / den
    br = bre_ref[...]
    bi = bim_ref[...]
    lre_ref[...] = lr
    lim_ref[...] = li
    bbre_ref[...] = cr * br - ci * bi
    bbim_ref[...] = cr * bi + ci * br
    ncim_ref[...] = -cim_ref[...]


def _s5_discretize(a_re, a_im, log_dt, b_re, b_im, c_im):
    rows = DEPTH * 2 * SSM_GROUPS
    cols = SSM_STATE * SSM_GROUP
    shp = (DEPTH, 2, SSM_GROUPS, SSM_STATE, SSM_GROUP)
    args = [
        jnp.broadcast_to(a_re[..., None], shp).reshape(rows, cols),
        jnp.broadcast_to(a_im[..., None], shp).reshape(rows, cols),
        jnp.broadcast_to(log_dt[..., None, None], shp).reshape(rows, cols),
        b_re.reshape(rows, cols), b_im.reshape(rows, cols), c_im.reshape(rows, cols),
    ]
    spec = pl.BlockSpec((rows, cols), lambda: (0, 0))
    outs = pl.pallas_call(
        _s5_disc_kernel,
        in_specs=[spec] * 6,
        out_specs=[spec] * 5,
        out_shape=[jax.ShapeDtypeStruct((rows, cols), F32)] * 5,
        name="s5_discretize",
    )(*args)
    lre, lim, bbre, bbim, ncim = outs
    lre = lre.reshape(shp)[..., 0]
    lim = lim.reshape(shp)[..., 0]
    return (lre, lim, bbre.reshape(shp), bbim.reshape(shp),
            ncim.reshape(DEPTH, 2, SSM_GROUPS, SSM_GROUP, SSM_STATE))


def _state_cols(x):
    lead = x.shape[:-3]
    x = x.reshape(lead + (N_STILE, 2, SSM_STATE, 2))
    x = jnp.moveaxis(x, -1, -3)
    return x.reshape(lead + (STATE_W,))


def _state_uncols(x):
    lead = x.shape[:-1]
    x = x.reshape(lead + (N_STILE, 2, 2, SSM_STATE))
    x = jnp.moveaxis(x, -3, -1)
    return x.reshape(lead + (SSM_GROUPS, SSM_STATE, 2))


def _s5_compact(lre, lim, bbre, bbim, c_re, ncim):
    n = DEPTH * 2
    bb = jnp.stack([bbre, bbim], axis=-1)
    bb = bb.reshape(n, N_STILE, 2, SSM_STATE, SSM_GROUP, 2)
    wb = jnp.transpose(bb, (0, 4, 1, 5, 2, 3)).reshape(DEPTH, 2, SSM_GROUP, STATE_W)
    cc = jnp.stack([c_re, ncim], axis=-1)
    cc = cc.reshape(n, N_STILE, 2, SSM_GROUP, SSM_STATE, 2)
    wc = jnp.transpose(cc, (0, 3, 1, 5, 2, 4)).reshape(DEPTH, 2, SSM_GROUP, STATE_W)
    lam = jnp.concatenate([lre.reshape(DEPTH, 2, N_STILE, 128), lim.reshape(DEPTH, 2, N_STILE, 128)], axis=2)
    return wb, wc, lam


def _group_mask(cols):
    row_g = lax.broadcasted_iota(jnp.int32, (256, cols), 0) // SSM_GROUP
    col = lax.broadcasted_iota(jnp.int32, (256, cols), 1)
    col_g = ((col // 256) % 8) * 2 + (col % 128) // SSM_STATE
    return row_g == col_g


def _s5_scan_kernel(u_ref, wb_ref, wc_ref, lam_ref, h0_ref, y_ref, hf_ref,
                    pm_scr, bm_scr, cm_scr, y_scr, state, ends, *group_scrs, segments):
    bu_scrs = group_scrs[:SSM_TILE_GROUPS]
    hs_scrs = group_scrs[SSM_TILE_GROUPS:]
    npass = 1 if segments == 1 else 2
    d = pl.program_id(0)
    ps = pl.program_id(1)
    ck = pl.program_id(2)
    nck = pl.num_programs(2)
    rows = SSM_TC * SSM_ROWS
    half = STATE_W // 2
    lam = lam_ref[...]

    @pl.when(jnp.logical_and(ps == 0, ck == 0))
    def _():
        a = lax.broadcasted_iota(jnp.int32, (rows, rows), 0)
        b = lax.broadcasted_iota(jnp.int32, (rows, rows), 1)
        same = jnp.logical_and(a // SSM_ROWS == b % SSM_TC, a % SSM_ROWS == b // SSM_TC)
        pm_scr[...] = jnp.where(same, 1.0, 0.0).astype(BF16)
        wb = jnp.tile(wb_ref[...], (256 // SSM_GROUP, 1))
        bm_scr[...] = jnp.where(_group_mask(STATE_W), wb, 0.0).astype(BF16)
        for n_ in range(2):
            wc = jnp.tile(wc_ref[:, n_ * half:(n_ + 1) * half], (256 // SSM_GROUP, 1))
            cm_scr[n_] = jnp.where(_group_mask(half), wc, 0.0).astype(BF16)

    @pl.when(ck == 0)
    def _():
        if npass == 1:
            state[...] = h0_ref[0]
        else:
            @pl.when(ps == 0)
            def _():
                state[...] = jnp.zeros_like(state)

            @pl.when(ps == 1)
            def _():
                ends[...] = state[...]
                state[...] = h0_ref[0]
                lr = lam[0:N_STILE]
                li = lam[N_STILE:]
                for _ in range(int(math.log2(SSM_SEG_LEN))):
                    lr, li = lr * lr - li * li, 2.0 * lr * li

                def carry(order, prev):
                    for sg in order:
                        for sq in range(SSM_ROWS // segments):
                            r = sq * segments + sg
                            q = r + prev
                            for j in range(N_STILE):
                                re = slice(j * 256, j * 256 + 128)
                                im = slice(j * 256 + 128, (j + 1) * 256)
                                pr = state[q:q + 1, re]
                                pi = state[q:q + 1, im]
                                ar = lr[j:j + 1]
                                ai = li[j:j + 1]
                                state[r:r + 1, re] = ends[q:q + 1, re] + ar * pr - ai * pi
                                state[r:r + 1, im] = ends[q:q + 1, im] + ar * pi + ai * pr

                @pl.when(d == 0)
                def _():
                    carry(range(1, segments), -1)

                @pl.when(d == 1)
                def _():
                    carry(range(segments - 2, -1, -1), 1)

    u = _dot(pm_scr[...], u_ref[...].reshape(rows, SSM_W).astype(BF16)).astype(BF16)

    def scan_step(emit):
        ngrp = len(bu_scrs)
        jt = N_STILE // ngrp
        gw = jt * 256

        def bproj(jb):
            k0 = (jb * jt // 8) * 256
            bu_scrs[jb][...] = _dot(u[:, k0:k0 + 256], bm_scr[:, jb * gw:(jb + 1) * gw])

        y_parts = [None, None]
        bproj(0)
        for jb in range(ngrp):
            if jb + 1 < ngrp:
                bproj(jb + 1)
            tiles = list(range(jb * jt, (jb + 1) * jt))
            bu, hs = bu_scrs[jb], hs_scrs[jb]
            lrs = [jnp.broadcast_to(lam[j:j + 1], (SSM_ROWS, 128)) for j in tiles]
            lis = [jnp.broadcast_to(lam[N_STILE + j:N_STILE + j + 1], (SSM_ROWS, 128)) for j in tiles]
            hr = [state[:, j * 256:j * 256 + 128] for j in tiles]
            hi = [state[:, j * 256 + 128:(j + 1) * 256] for j in tiles]
            for t in range(SSM_TC):
                tt = t + d * (SSM_TC - 1 - 2 * t)
                r = pl.ds(pl.multiple_of(tt * SSM_ROWS, SSM_ROWS), SSM_ROWS)
                for n_ in range(jt):
                    re = slice(n_ * 256, n_ * 256 + 128)
                    im = slice(n_ * 256 + 128, (n_ + 1) * 256)
                    nr = lrs[n_] * hr[n_] - lis[n_] * hi[n_] + bu[r, re]
                    ni = lrs[n_] * hi[n_] + lis[n_] * hr[n_] + bu[r, im]
                    if emit:
                        hs[r, re] = nr
                        hs[r, im] = ni
                    hr[n_], hi[n_] = nr, ni
            for n_, j in enumerate(tiles):
                state[:, j * 256:j * 256 + 128] = hr[n_]
                state[:, j * 256 + 128:(j + 1) * 256] = hi[n_]
            if emit:
                n_ = jb * gw // half
                off = jb * gw % half
                part = _dot_nt(hs[...].astype(BF16), cm_scr[n_, :, off:off + gw])
                y_parts[n_] = part if y_parts[n_] is None else y_parts[n_] + part
        if emit:
            for n_ in range(2):
                y_scr[2 * n_] = y_parts[n_][:, :128]
                y_scr[2 * n_ + 1] = y_parts[n_][:, 128:]
            for r in range(SSM_ROWS):
                for c_ in range(SSM_W // 128):
                    y_ref[r, :, c_ * 128:(c_ + 1) * 128] = y_scr[c_, pl.ds(r, SSM_TC, stride=SSM_ROWS), :]

    if npass == 1:
        scan_step(True)
    else:
        @pl.when(ps == 0)
        def _():
            scan_step(False)

        @pl.when(ps == 1)
        def _():
            scan_step(True)

    @pl.when(jnp.logical_and(ps == npass - 1, ck == nck - 1))
    def _():
        hf_ref[0] = state[...]


def _s5_scan(p3, row_block, wb, wc, lam, h0, l, *, segments):
    steps = p3.shape[1]
    nck = steps // SSM_TC
    npass = 1 if segments == 1 else 2

    def chunk(d, c):
        return c + d * (nck - 1 - 2 * c)

    def y_chunk(d, p, c):
        return jnp.where(p == npass - 1, chunk(d, c), chunk(d, 0))

    par = pl.BlockSpec((None, None, SSM_GROUP, STATE_W), lambda d, p, c: (l, d, 0, 0))
    st = pl.BlockSpec((1, SSM_ROWS, STATE_W), lambda d, p, c: (d, 0, 0))
    return pl.pallas_call(
        functools.partial(_s5_scan_kernel, segments=segments),
        grid=(2, npass, nck),
        in_specs=[
            pl.BlockSpec((SSM_ROWS, SSM_TC, SSM_W), lambda d, p, c: (row_block, chunk(d, c), COL_U)),
            par, par,
            pl.BlockSpec((None, None, 2 * N_STILE, 128), lambda d, p, c: (l, d, 0, 0)),
            st,
        ],
        out_specs=[
            pl.BlockSpec((None, SSM_ROWS, SSM_TC, SSM_W), lambda d, p, c: (d, 0, y_chunk(d, p, c), 0)),
            st,
        ],
        out_shape=[
            jax.ShapeDtypeStruct((2, SSM_ROWS, steps, SSM_W), F32),
            jax.ShapeDtypeStruct((2, SSM_ROWS, STATE_W), F32),
        ],
        scratch_shapes=[
            pltpu.VMEM((SSM_TC * SSM_ROWS, SSM_TC * SSM_ROWS), BF16),
            pltpu.VMEM((256, STATE_W), BF16),
            pltpu.VMEM((2, 256, STATE_W // 2), BF16),
            pltpu.VMEM((SSM_W // 128, SSM_TC * SSM_ROWS, 128), F32),
            pltpu.VMEM((SSM_ROWS, STATE_W), F32),
            pltpu.VMEM((SSM_ROWS, STATE_W), F32),
        ] + [pltpu.VMEM((SSM_TC * SSM_ROWS, STATE_W // SSM_TILE_GROUPS), F32)] * (2 * SSM_TILE_GROUPS),
        compiler_params=_cparams(("arbitrary", "arbitrary", "arbitrary")),
        name="s5_scan_seg%d" % segments,
    )(p3, wb, wc, lam, h0)


def _s5_glu_kernel(ypf_ref, ypb_ref, ysf_ref, ysb_ref, u_ref, d_ref, w_ref, o_ref):
    i = pl.program_id(0)
    w = w_ref[...].astype(BF16)

    def glu(y):
        y = y + d_ref[...] * u_ref[...]
        z = _dot(jax.nn.gelu(y).astype(BF16), w)
        o_ref[...] = (z[:, :SSM_W] * _sigmoid(z[:, SSM_W:])).astype(BF16)

    @pl.when(i < N_PT)
    def _():
        glu(ypf_ref[...] + ypb_ref[...])

    @pl.when(i >= N_PT)
    def _():
        glu(ysf_ref[...] + ysb_ref[...])


def _s5_glu(yp, ys, p, d, w, l):
    return pl.pallas_call(
        _s5_glu_kernel,
        grid=(N_MT,),
        in_specs=[
            pl.BlockSpec((None, TM, SSM_W), lambda i: (0, _prompt_idx(i), 0)),
            pl.BlockSpec((None, TM, SSM_W), lambda i: (1, _prompt_idx(i), 0)),
            pl.BlockSpec((None, TM, SSM_W), lambda i: (0, _sample_idx(i), 0)),
            pl.BlockSpec((None, TM, SSM_W), lambda i: (1, _sample_idx(i), 0)),
            pl.BlockSpec((TM, SSM_W), lambda i: (i, COL_U)),
            pl.BlockSpec((None, 1, SSM_W), lambda i: (l, 0, 0)),
            pl.BlockSpec((None, SSM_W, 2 * SSM_W), lambda i: (l, 0, 0)),
        ],
        out_specs=pl.BlockSpec((TM, SSM_W), lambda i: (i, 0)),
        out_shape=jax.ShapeDtypeStruct((N_TOK, SSM_W), BF16),
        compiler_params=_cparams(("arbitrary",)),
        name="s5_glu",
    )(yp, yp, ys, ys, p, d.reshape(DEPTH, 1, SSM_W), w)


def _s5_layer(p, s5p, st_l, ssm_d, w_ssm_glu, l):
    wb, wc, lam = s5p
    yp, hfin = _s5_scan(p.reshape(N_TOK // SEQ, SEQ, IN_W), 0, wb, wc, lam,
                        jnp.zeros((2, SSM_ROWS, STATE_W), F32), l, segments=1)
    h0 = _state_cols(jnp.moveaxis(st_l, 0, 1))
    h0 = jnp.stack([
        jnp.zeros((DEC_BATCH, SSM_SEGS, STATE_W), F32).at[:, 0].set(h0[0]),
        jnp.zeros((DEC_BATCH, SSM_SEGS, STATE_W), F32).at[:, SSM_SEGS - 1].set(h0[1]),
    ]).reshape(2, SSM_ROWS, STATE_W)
    ys, _ = _s5_scan(p.reshape(N_TOK // SSM_SEG_LEN, SSM_SEG_LEN, IN_W), N_PROMPT // SSM_SEG_LEN // SSM_ROWS,
                     wb, wc, lam, h0, l, segments=SSM_SEGS)
    out = _s5_glu(yp.reshape(2, N_PROMPT, SSM_W), ys.reshape(2, N_SAMPLE, SSM_W), p, ssm_d, w_ssm_glu, l)
    new_state = jnp.moveaxis(_state_uncols(hfin), 0, 1)
    return out, new_state


def _rope_tables():
    rows = DEC_SEQ // GRID_W
    r = jnp.repeat(jnp.arange(rows), GRID_W).astype(F32)
    col = jnp.tile(jnp.arange(GRID_W), rows).astype(F32)
    inv = ROPE_BASE ** (-jnp.arange(ROPE_FREQS, dtype=F32) / ROPE_FREQS)
    ang = jnp.stack([r, col], axis=-1)[:, :, None] * inv
    cos = jnp.cos(ang)
    sin = jnp.sin(ang)
    cos_t = jnp.concatenate([cos, cos], axis=-1).reshape(DEC_SEQ, HEAD_DIM)
    sin_t = jnp.concatenate([-sin, sin], axis=-1).reshape(DEC_SEQ, HEAD_DIM)
    return cos_t, sin_t


def kernel(x_prompt, x_sample, cache_k, cache_v, state_ssm, state_ret, c, c_ctx, w_mod, b_mod, norm_g,
           w_ffn_in, w_ffn_out, w_in, w_out, q_norm_g, k_norm_g, ssm_a_re, ssm_a_im, ssm_log_dt,
           ssm_b_re, ssm_b_im, ssm_c_re, ssm_c_im, ssm_d, w_ssm_glu, ret_decay_logit, final_norm_g):
    x = (x_prompt.reshape(N_PROMPT, D_MODEL), x_sample.reshape(N_SAMPLE, D_MODEL))
    cond8 = jnp.concatenate([c_ctx[None], c, jnp.zeros((5, D_MODEL), F32)], axis=0)
    mods = _adaln(cond8, w_mod, b_mod).reshape(DEPTH, 8, N_SUB * 3, D_MODEL)
    modt = jnp.transpose(mods[:, jnp.array(TILE_MOD, jnp.int32)], (0, 2, 1, 3))[:, :, :, None, :]
    gains = norm_g.reshape(DEPTH * N_SUB, 1, D_MODEL)
    cos_t, sin_t = _rope_tables()
    lre, lim, bbre, bbim, ncim = _s5_discretize(ssm_a_re, ssm_a_im, ssm_log_dt, ssm_b_re, ssm_b_im, ssm_c_im)
    s5p = _s5_compact(lre, lim, bbre, bbim, ssm_c_re, ncim)
    dl = jnp.broadcast_to(ret_decay_logit[:, :, :, None, None], (DEPTH, 2, RET_HEADS, 8, HEAD_DIM))
    ks_, vs_, hs_, ss_ = [], [], [], []
    for l in range(DEPTH):
        x = _ffn(x, gains, modt, w_ffn_in, w_ffn_out, l, 0)
        p = _inproj(x, gains, modt, w_in, l)
        attn, k_l, v_l = _attention(p, cache_k, cache_v, q_norm_g, k_norm_g, cos_t, sin_t, l)
        ssm, h_l = _s5_layer(p, s5p, state_ssm[:, l], ssm_d, w_ssm_glu, l)
        ret, s_l = _retention(p, dl, cos_t, sin_t, state_ret, l)
        x = _outproj(x, attn, ssm, ret, modt, w_out, l)
        x = _ffn(x, gains, modt, w_ffn_in, w_ffn_out, l, 1, final_g=final_norm_g if l == DEPTH - 1 else None)
        ks_.append(k_l)
        vs_.append(v_l)
        hs_.append(h_l)
        ss_.append(s_l)
    y_prompt = x[0].reshape(BATCH, SEQ, D_MODEL)
    y_sample = x[1].reshape(DEC_BATCH, DEC_SEQ, D_MODEL)
    return (y_prompt, y_sample, jnp.stack(ks_, axis=1), jnp.stack(vs_, axis=1),
            jnp.stack(hs_, axis=1), jnp.stack(ss_, axis=1))
```

```python
import functools
import math

import jax
import jax.numpy as jnp
from jax import lax
from jax.experimental import pallas as pl
from jax.experimental.pallas import tpu as pltpu

D_MODEL = 2048
BATCH = 16
SEQ = 256
DEPTH = 2
DEC_BATCH = 2
DEC_SEQ = 1024
PAST_LEN = 512
GRID_W = 64
HEAD_DIM = 128
N_Q_HEADS = 8
N_KV_HEADS = 2
Q_PER_KV = 4
ATTN_W = 1024
KV_W = 256
SSM_W = 512
SSM_GROUP = 16
SSM_GROUPS = 32
SSM_STATE = 64
RET_HEADS = 4
RET_W = 512
IN_W = 4096
D_FF = 5632
N_SUB = 3
ROPE_BASE = 10000.0
ROPE_FREQS = 32
EPS = 1e-6

N_PROMPT = BATCH * SEQ
N_SAMPLE = DEC_BATCH * DEC_SEQ
N_TOK = N_PROMPT + N_SAMPLE
TM = 1024
N_MT = N_TOK // TM
N_PT = N_PROMPT // TM
SEQ_PER_TILE = TM // SEQ
TILE_MOD = (0, 0, 0, 0, 1, 2)
ROW_CHUNK = 256
TF = 256
FFN_NSPLIT = 4
TN = 512
STATE_W = 2 * SSM_GROUPS * SSM_STATE
N_STILE = STATE_W // 256
SSM_ROWS = 16
SSM_SEGS = 8
SSM_SEG_LEN = DEC_SEQ // SSM_SEGS
SSM_TC = 32
SSM_TILE_GROUPS = 4
ATT_TQ = 256
ATT_S = PAST_LEN + DEC_SEQ
VMEM_LIMIT = 56 * 1024 * 1024
VMEM_LIMIT_FFN = 60 * 1024 * 1024

COL_K = ATTN_W // KV_W
COL_V = COL_K + 1
COL_U = (ATTN_W + 2 * KV_W) // SSM_W
COL_R = (ATTN_W + 2 * KV_W + SSM_W) // RET_W

BF16 = jnp.bfloat16
F32 = jnp.float32


def _cparams(sem, limit=VMEM_LIMIT):
    return pltpu.CompilerParams(dimension_semantics=sem, vmem_limit_bytes=limit)


def _dot(a, b):
    return jnp.dot(a, b, preferred_element_type=F32)


def _dot_nt(a, b):
    return lax.dot_general(a, b, (((1,), (1,)), ((), ())), preferred_element_type=F32)


def _sigmoid(x):
    return 1.0 / (1.0 + jnp.exp(-x))


def _silu(x):
    return x * _sigmoid(x)


def _sample_idx(i):
    return jnp.clip(i - N_PT, 0, DEC_BATCH - 1)


def _prompt_idx(i):
    return jnp.minimum(i, N_PT - 1)


def _row_chunks(n_rows, body):
    def step(c, carry):
        body(pl.ds(pl.multiple_of(c * ROW_CHUNK, ROW_CHUNK), ROW_CHUNK))
        return carry

    lax.fori_loop(0, n_rows // ROW_CHUNK, step, 0)


def _mod_spec(l, k, width=D_MODEL, col=lambda j: 0, tile=lambda i: i):
    return pl.BlockSpec((None, None, 1, 1, width), lambda i, j: (l, k, tile(i), 0, col(j)))


def _gain_spec(l, sub):
    return pl.BlockSpec((None, 1, D_MODEL), lambda i, j: (l * N_SUB + sub, 0, 0))


def _adaln_kernel(c_ref, w_ref, b_ref, o_ref):
    a = _silu(c_ref[...]).astype(BF16)
    o_ref[0] = _dot(a, w_ref[0].astype(BF16)) + b_ref[0]


def _adaln(cond8, w_mod, b_mod):
    n = w_mod.shape[-1]
    tn = 1024
    return pl.pallas_call(
        _adaln_kernel,
        grid=(DEPTH, n // tn),
        in_specs=[
            pl.BlockSpec((8, D_MODEL), lambda l, j: (0, 0)),
            pl.BlockSpec((1, D_MODEL, tn), lambda l, j: (l, 0, j)),
            pl.BlockSpec((1, 1, tn), lambda l, j: (l, 0, j)),
        ],
        out_specs=pl.BlockSpec((1, 8, tn), lambda l, j: (l, 0, j)),
        out_shape=jax.ShapeDtypeStruct((DEPTH, 8, n), F32),
        compiler_params=_cparams(("arbitrary", "arbitrary")),
        name="adaln",
    )(cond8, w_mod, b_mod.reshape(DEPTH, 1, n))


def _norm_mod(x, g, shift, scale):
    ms = jnp.mean(x * x, axis=-1, keepdims=True)
    return x * lax.rsqrt(ms + EPS) * (g * (1.0 + scale)) + shift


def _ffn_kernel(*refs, n_x, n_out, final):
    refs = list(refs)
    x_refs = refs[:n_x]
    g_ref, sh_ref, sc_ref, gt_ref, wa_ref, wu_ref, wo_ref = refs[n_x:n_x + 7]
    rest = refs[n_x + 7:]
    fg_ref = rest.pop(0) if final else None
    o_refs, h_scr = rest[:n_out], rest[n_out]
    i = pl.program_id(0)
    j = pl.program_id(1)

    def run(x_ref, o_ref):
        @pl.when(j == 0)
        def _():
            def pre(rows):
                h = _norm_mod(x_ref[rows, :], g_ref[...], sh_ref[0], sc_ref[0])
                h_scr[rows, :] = h.astype(BF16)

            _row_chunks(TM, pre)
            o_ref[...] = jnp.zeros_like(o_ref)

        h = h_scr[...]
        a = _dot(h, wa_ref[...].astype(BF16))
        u = _dot(h, wu_ref[...].astype(BF16))
        mid = (_silu(a) * u).astype(BF16)
        wn = D_MODEL // FFN_NSPLIT
        for n in range(FFN_NSPLIT):
            cs = slice(n * wn, (n + 1) * wn)
            o_ref[:, cs] += _dot(mid, wo_ref[:, cs].astype(BF16))

        @pl.when(j == pl.num_programs(1) - 1)
        def _():
            def post(rows):
                out = x_ref[rows, :] + (0.5 * gt_ref[0]) * o_ref[rows, :]
                if final:
                    ms = jnp.mean(out * out, axis=-1, keepdims=True)
                    out = out * lax.rsqrt(ms + EPS) * fg_ref[...]
                o_ref[rows, :] = out

            _row_chunks(TM, post)

    if n_x == 1 and n_out == 1:
        run(x_refs[0], o_refs[0])
    else:
        @pl.when(i < N_PT)
        def _():
            run(x_refs[0], o_refs[0])

        @pl.when(i >= N_PT)
        def _():
            run(x_refs[-1], o_refs[-1])


def _split_tok_specs():
    return [pl.BlockSpec((TM, D_MODEL), lambda i, j: (_prompt_idx(i), 0)),
            pl.BlockSpec((TM, D_MODEL), lambda i, j: (_sample_idx(i), 0), pipeline_mode=pl.Buffered(1))]


def _ffn(x, gains, modt, w_in, w_out, l, f, final_g=None):
    final = final_g is not None
    xs = list(x) if isinstance(x, (tuple, list)) else [x]
    sub = 2 * f
    nf = D_FF // TF
    tok = pl.BlockSpec((TM, D_MODEL), lambda i, j: (i, 0))
    in_specs = (_split_tok_specs() if len(xs) == 2 else [tok]) + [
        _gain_spec(l, sub), _mod_spec(l, 3 * sub), _mod_spec(l, 3 * sub + 1), _mod_spec(l, 3 * sub + 2),
        pl.BlockSpec((None, None, D_MODEL, TF), lambda i, j: (l, f, 0, j)),
        pl.BlockSpec((None, None, D_MODEL, TF), lambda i, j: (l, f, 0, j + nf)),
        pl.BlockSpec((None, None, TF, D_MODEL), lambda i, j: (l, f, j, 0)),
    ]
    args = xs + [gains, modt, modt, modt, w_in, w_in, w_out]
    if final:
        in_specs.append(pl.BlockSpec((1, D_MODEL), lambda i, j: (0, 0)))
        args.append(final_g.reshape(1, D_MODEL))
        out_specs = _split_tok_specs()
        out_shape = [jax.ShapeDtypeStruct((N_PROMPT, D_MODEL), F32), jax.ShapeDtypeStruct((N_SAMPLE, D_MODEL), F32)]
    else:
        out_specs = tok
        out_shape = jax.ShapeDtypeStruct((N_TOK, D_MODEL), F32)
    return pl.pallas_call(
        functools.partial(_ffn_kernel, n_x=len(xs), n_out=2 if final else 1, final=final),
        grid=(N_MT, nf),
        in_specs=in_specs,
        out_specs=out_specs,
        out_shape=out_shape,
        scratch_shapes=[pltpu.VMEM((TM, D_MODEL), BF16)],
        compiler_params=_cparams(("arbitrary", "arbitrary"), VMEM_LIMIT_FFN),
        name="ffn_final" if final else ("ffn_first" if len(xs) == 2 else "ffn"),
    )(*args)


def _resident_weight_spec(rows, l, ncol):
    return pl.BlockSpec((None, rows, TN), lambda i, j: (l, 0, jnp.where(i == 0, j, ncol - 1)))


def _inproj_kernel(x_ref, g_ref, sh_ref, sc_ref, w_ref, o_ref, h_scr, w_scr):
    i = pl.program_id(0)
    j = pl.program_id(1)

    @pl.when(j == 0)
    def _():
        def pre(rows):
            h = _norm_mod(x_ref[rows, :], g_ref[...], sh_ref[0], sc_ref[0])
            h_scr[rows, :] = h.astype(BF16)

        _row_chunks(TM, pre)

    @pl.when(i == 0)
    def _():
        w_scr[j] = w_ref[...].astype(BF16)

    o_ref[...] = _dot(h_scr[...], w_scr[j])


def _inproj(x, gains, modt, w, l):
    ncol = IN_W // TN
    return pl.pallas_call(
        _inproj_kernel,
        grid=(N_MT, ncol),
        in_specs=[
            pl.BlockSpec((TM, D_MODEL), lambda i, j: (i, 0)),
            _gain_spec(l, 1), _mod_spec(l, 3), _mod_spec(l, 4),
            _resident_weight_spec(D_MODEL, l, ncol),
        ],
        out_specs=pl.BlockSpec((TM, TN), lambda i, j: (i, j)),
        out_shape=jax.ShapeDtypeStruct((N_TOK, IN_W), F32),
        scratch_shapes=[pltpu.VMEM((TM, D_MODEL), BF16), pltpu.VMEM((ncol, D_MODEL, TN), BF16)],
        compiler_params=_cparams(("arbitrary", "arbitrary")),
        name="inproj",
    )(x, gains, modt, modt, w)


def _outproj_kernel(x_ref, a_ref, s_ref, r_ref, gt_ref, w_ref, o_ref, w_scr):
    i = pl.program_id(0)
    j = pl.program_id(1)

    @pl.when(i == 0)
    def _():
        w_scr[j] = w_ref[...].astype(BF16)

    y = _dot(a_ref[...], w_scr[j, 0:ATTN_W, :])
    y += _dot(s_ref[...], w_scr[j, ATTN_W:ATTN_W + SSM_W, :])
    y += _dot(r_ref[...], w_scr[j, ATTN_W + SSM_W:, :])
    o_ref[...] = x_ref[...] + gt_ref[0] * y


def _outproj(x, attn, ssm, ret, modt, w, l):
    ncol = D_MODEL // TN
    return pl.pallas_call(
        _outproj_kernel,
        grid=(N_MT, ncol),
        in_specs=[
            pl.BlockSpec((TM, TN), lambda i, j: (i, j)),
            pl.BlockSpec((TM, ATTN_W), lambda i, j: (i, 0)),
            pl.BlockSpec((TM, SSM_W), lambda i, j: (i, 0)),
            pl.BlockSpec((TM, RET_W), lambda i, j: (i, 0)),
            _mod_spec(l, 5, TN, lambda j: j),
            _resident_weight_spec(D_MODEL, l, ncol),
        ],
        out_specs=pl.BlockSpec((TM, TN), lambda i, j: (i, j)),
        out_shape=jax.ShapeDtypeStruct((N_TOK, D_MODEL), F32),
        scratch_shapes=[pltpu.VMEM((ncol, D_MODEL, TN), BF16)],
        compiler_params=_cparams(("arbitrary", "arbitrary")),
        name="outproj",
    )(x, attn, ssm, ret, modt, w)


def _head_rms(x, g):
    ms = jnp.mean(x * x, axis=-1, keepdims=True)
    return x * lax.rsqrt(ms + EPS) * g


def _rope(x, cos, sin_signed):
    lane = lax.broadcasted_iota(jnp.int32, x.shape, 1)
    partner = jnp.where((lane % 64) < 32, pltpu.roll(x, 96, 1), pltpu.roll(x, 32, 1))
    return x * cos + partner * sin_signed


SOFTMAX_EXP2_SCALE = HEAD_DIM ** -0.5 * math.log2(math.e)


def _attend_short(qs, k, v):
    s = _dot_nt(qs.astype(BF16), k)
    p = jnp.exp2((s - jnp.max(s, axis=-1, keepdims=True)) * SOFTMAX_EXP2_SCALE)
    l = jnp.sum(p, axis=-1, keepdims=True)
    return _dot(p.astype(BF16), v) / l


def _attend(qs, k, vt):
    st = _dot_nt(k, qs.astype(BF16))
    p = jnp.exp2((st - jnp.max(st, axis=0, keepdims=True)) * SOFTMAX_EXP2_SCALE)
    l = jnp.sum(p, axis=0, keepdims=True)
    return (_dot(vt, p.astype(BF16)) / l).T


def _attn_kernel(q_ref, k_ref, v_ref, ck_ref, cv_ref, qg_ref, kg_ref, cos_ref, sin_ref,
                 o_ref, kc_ref, vc_ref, k_scr, vt_scr):
    i = pl.program_id(0)
    qg = qg_ref[...]
    kg = kg_ref[...]

    def heads(kv):
        return [slice((kv * Q_PER_KV + g) * HEAD_DIM, (kv * Q_PER_KV + g + 1) * HEAD_DIM) for g in range(Q_PER_KV)]

    @pl.when(i < N_PT)
    def _():
        def seq_body(sq, carry):
            rows = pl.ds(pl.multiple_of(sq * SEQ, SEQ), SEQ)
            v = v_ref[rows, :]
            for kv_ in range(N_KV_HEADS):
                vc_ref[sq, :, kv_, :] = v[:, kv_ * HEAD_DIM:(kv_ + 1) * HEAD_DIM]
            for kv in range(N_KV_HEADS):
                ksl = slice(kv * HEAD_DIM, (kv + 1) * HEAD_DIM)
                kn = _head_rms(k_ref[rows, ksl], kg)
                kc_ref[sq, :, kv, :] = kn
                qs = jnp.concatenate([_head_rms(q_ref[rows, hs], qg) for hs in heads(kv)], axis=0)
                o = _attend_short(qs, kn.astype(BF16), v[:, ksl].astype(BF16))
                for g, hs in enumerate(heads(kv)):
                    o_ref[rows, hs] = o[g * SEQ:(g + 1) * SEQ].astype(BF16)
            return carry

        lax.fori_loop(0, SEQ_PER_TILE, seq_body, 0)

    @pl.when(i >= N_PT)
    def _():
        k_scr[0:PAST_LEN, :] = ck_ref[...].astype(BF16)
        for kv in range(N_KV_HEADS):
            ksl = slice(kv * HEAD_DIM, (kv + 1) * HEAD_DIM)
            vt_scr[kv, :, 0:PAST_LEN] = cv_ref[:, ksl].T.astype(BF16)
            vt_scr[kv, :, PAST_LEN:] = v_ref[:, ksl].T.astype(BF16)
            kn = _rope(_head_rms(k_ref[:, ksl], kg), cos_ref[...], sin_ref[...])
            k_scr[PAST_LEN:, ksl] = kn.astype(BF16)

        def q_body(qb, carry):
            rows = pl.ds(pl.multiple_of(qb * ATT_TQ, ATT_TQ), ATT_TQ)
            cos = cos_ref[rows, :]
            sin = sin_ref[rows, :]
            for kv in range(N_KV_HEADS):
                ksl = slice(kv * HEAD_DIM, (kv + 1) * HEAD_DIM)
                qs = jnp.concatenate([_rope(_head_rms(q_ref[rows, hs], qg), cos, sin) for hs in heads(kv)], axis=0)
                o = _attend(qs, k_scr[:, ksl], vt_scr[kv])
                for g, hs in enumerate(heads(kv)):
                    o_ref[rows, hs] = o[g * ATT_TQ:(g + 1) * ATT_TQ].astype(BF16)
            return carry

        lax.fori_loop(0, DEC_SEQ // ATT_TQ, q_body, 0)


def _attention(p, cache_k, cache_v, qg, kg, cos_t, sin_t, l):
    cache_spec = pl.BlockSpec((None, None, PAST_LEN, KV_W), lambda i: (_sample_idx(i), l, 0, 0))
    vec = pl.BlockSpec((None, 1, HEAD_DIM), lambda i: (l, 0, 0))
    tab = pl.BlockSpec((DEC_SEQ, HEAD_DIM), lambda i: (0, 0))
    new_cache = pl.BlockSpec((SEQ_PER_TILE, SEQ, N_KV_HEADS, HEAD_DIM), lambda i: (_prompt_idx(i), 0, 0, 0))
    return pl.pallas_call(
        _attn_kernel,
        grid=(N_MT,),
        in_specs=[
            pl.BlockSpec((TM, ATTN_W), lambda i: (i, 0)),
            pl.BlockSpec((TM, KV_W), lambda i: (i, COL_K)),
            pl.BlockSpec((TM, KV_W), lambda i: (i, COL_V)),
            cache_spec, cache_spec, vec, vec, tab, tab,
        ],
        out_specs=[pl.BlockSpec((TM, ATTN_W), lambda i: (i, 0)), new_cache, new_cache],
        out_shape=[
            jax.ShapeDtypeStruct((N_TOK, ATTN_W), BF16),
            jax.ShapeDtypeStruct((BATCH, SEQ, N_KV_HEADS, HEAD_DIM), F32),
            jax.ShapeDtypeStruct((BATCH, SEQ, N_KV_HEADS, HEAD_DIM), F32),
        ],
        scratch_shapes=[pltpu.VMEM((ATT_S, KV_W), BF16), pltpu.VMEM((N_KV_HEADS, HEAD_DIM, ATT_S), BF16)],
        compiler_params=_cparams(("arbitrary",)),
        name="attention",
    )(p, p, p, cache_k.reshape(DEC_BATCH, DEPTH, PAST_LEN, KV_W), cache_v.reshape(DEC_BATCH, DEPTH, PAST_LEN, KV_W),
      qg.reshape(DEPTH, 1, HEAD_DIM), kg.reshape(DEPTH, 1, HEAD_DIM), cos_t, sin_t)


def _log_sigmoid(x):
    return -(jnp.maximum(-x, 0.0) + jnp.log(1.0 + jnp.exp(-jnp.abs(x))))


def _decay_mask(qi, kj, lg_f, lg_b):
    dd = qi - kj
    log2e = math.log2(math.e)
    w = jnp.exp2(dd * jnp.where(dd > 0, lg_f * log2e, -lg_b * log2e))
    return jnp.where(dd == 0, 2.0, w)


def _retention_kernel(q_ref, k_ref, v_ref, g_ref, dl_ref, cos_ref, sin_ref, s0_ref, o_ref, sf_ref,
                      acc, qr_scr, kr_scr, vb_scr):
    i = pl.program_id(0)
    scale = HEAD_DIM ** -0.5
    lg = [[_log_sigmoid(dl_ref[d, h][0:1, :]) for h in range(RET_HEADS)] for d in range(2)]
    lg1 = [[lg[d][h][:, 0:1] for h in range(RET_HEADS)] for d in range(2)]
    hsl = [slice(h * HEAD_DIM, (h + 1) * HEAD_DIM) for h in range(RET_HEADS)]

    @pl.when(i < N_PT)
    def _():
        qi = lax.broadcasted_iota(jnp.int32, (SEQ, SEQ), 0).astype(F32)
        kj = lax.broadcasted_iota(jnp.int32, (SEQ, SEQ), 1).astype(F32)
        pos = lax.broadcasted_iota(jnp.int32, (SEQ, HEAD_DIM), 0).astype(F32)
        masks = [_decay_mask(qi, kj, lg1[0][h], lg1[1][h]) for h in range(RET_HEADS)]
        kdec_f = [jnp.exp((SEQ - 1.0 - pos) * lg[0][h]) for h in range(RET_HEADS)]
        kdec_b = [jnp.exp(pos * lg[1][h]) for h in range(RET_HEADS)]

        def seq_body(sq, carry):
            rows = pl.ds(pl.multiple_of(sq * SEQ, SEQ), SEQ)
            for h in range(RET_HEADS):
                q = q_ref[rows, hsl[h]].astype(BF16)
                k = k_ref[rows, hsl[h]] * scale
                v = v_ref[rows, hsl[h]].astype(BF16)
                a = (_dot_nt(q, k.astype(BF16)) * masks[h]).astype(BF16)
                acc[rows, hsl[h]] = _dot(a, v)
                sf_ref[sq, 0, h] = _dot((k * kdec_f[h]).T.astype(BF16), v)
                sf_ref[sq, 1, h] = _dot((k * kdec_b[h]).T.astype(BF16), v)
            return carry

        lax.fori_loop(0, SEQ_PER_TILE, seq_body, 0)

    @pl.when(i >= N_PT)
    def _():
        for h in range(RET_HEADS):
            qr_scr[:, hsl[h]] = _rope(q_ref[:, hsl[h]], cos_ref[...], sin_ref[...]).astype(BF16)
            kr_scr[:, hsl[h]] = _rope(k_ref[:, hsl[h]] * scale, cos_ref[...], sin_ref[...]).astype(BF16)
        vb_scr[...] = v_ref[...].astype(BF16)

        def q_body(qb, carry):
            row0 = qb * ATT_TQ
            rows = pl.ds(pl.multiple_of(row0, ATT_TQ), ATT_TQ)
            qi = (row0 + lax.broadcasted_iota(jnp.int32, (ATT_TQ, DEC_SEQ), 0)).astype(F32)
            kj = lax.broadcasted_iota(jnp.int32, (ATT_TQ, DEC_SEQ), 1).astype(F32)
            pos = (row0 + lax.broadcasted_iota(jnp.int32, (ATT_TQ, HEAD_DIM), 0)).astype(F32)
            for h in range(RET_HEADS):
                q = qr_scr[rows, hsl[h]]
                a = (_dot_nt(q, kr_scr[:, hsl[h]]) * _decay_mask(qi, kj, lg1[0][h], lg1[1][h])).astype(BF16)
                o = _dot(a, vb_scr[:, hsl[h]])
                o += _dot(q, s0_ref[0, h].astype(BF16)) * jnp.exp((pos + 1.0) * lg[0][h])
                o += _dot(q, s0_ref[1, h].astype(BF16)) * jnp.exp((DEC_SEQ - pos) * lg[1][h])
                acc[rows, hsl[h]] = o
            return carry

        lax.fori_loop(0, DEC_SEQ // ATT_TQ, q_body, 0)

    for h in range(RET_HEADS):
        o = acc[:, hsl[h]]
        o = o - jnp.mean(o, axis=-1, keepdims=True)
        o = o * lax.rsqrt(jnp.mean(o * o, axis=-1, keepdims=True) + EPS)
        o_ref[:, hsl[h]] = (o * _silu(g_ref[:, hsl[h]])).astype(BF16)


def _retention(p, dl, cos_t, sin_t, state_ret, l):
    tab = pl.BlockSpec((DEC_SEQ, HEAD_DIM), lambda i: (0, 0))
    st = (2, RET_HEADS, HEAD_DIM, HEAD_DIM)
    return pl.pallas_call(
        _retention_kernel,
        grid=(N_MT,),
        in_specs=[pl.BlockSpec((TM, RET_W), lambda i, k=k: (i, COL_R + k)) for k in range(4)] + [
            pl.BlockSpec((None, 2, RET_HEADS, 8, HEAD_DIM), lambda i: (l, 0, 0, 0, 0)),
            tab, tab,
            pl.BlockSpec((None, None) + st, lambda i: (_sample_idx(i), l, 0, 0, 0, 0)),
        ],
        out_specs=[
            pl.BlockSpec((TM, RET_W), lambda i: (i, 0)),
            pl.BlockSpec((SEQ_PER_TILE,) + st, lambda i: (_prompt_idx(i), 0, 0, 0, 0)),
        ],
        out_shape=[
            jax.ShapeDtypeStruct((N_TOK, RET_W), BF16),
            jax.ShapeDtypeStruct((BATCH,) + st, F32),
        ],
        scratch_shapes=[pltpu.VMEM((TM, RET_W), F32)] + [pltpu.VMEM((TM, RET_W), BF16)] * 3,
        compiler_params=_cparams(("arbitrary",)),
        name="retention",
    )(p, p, p, p, dl, cos_t, sin_t, state_ret)


def _s5_disc_kernel(are_ref, aim_ref, ldt_ref, bre_ref, bim_ref, cim_ref,
                    lre_ref, lim_ref, bbre_ref, bbim_ref, ncim_ref):
    ar = are_ref[...]
    ai = aim_ref[...]
    dt = jnp.exp(ldt_ref[...])
    mag = jnp.exp(ar * dt)
    lr = mag * jnp.cos(ai * dt)
    li = mag * jnp.sin(ai * dt)
    den = ar * ar + ai * ai
    nr = lr - 1.0
    cr = (nr * ar + li * ai) / den
    ci = (li * ar - nr * ai) / den
    br = bre_ref[...]
    bi = bim_ref[...]
    lre_ref[...] = lr
    lim_ref[...] = li
    bbre_ref[...] = cr * br - ci * bi
    bbim_ref[...] = cr * bi + ci * br
    ncim_ref[...] = -cim_ref[...]


def _s5_discretize(a_re, a_im, log_dt, b_re, b_im, c_im):
    rows = DEPTH * 2 * SSM_GROUPS
    cols = SSM_STATE * SSM_GROUP
    shp = (DEPTH, 2, SSM_GROUPS, SSM_STATE, SSM_GROUP)
    args = [
        jnp.broadcast_to(a_re[..., None], shp).reshape(rows, cols),
        jnp.broadcast_to(a_im[..., None], shp).reshape(rows, cols),
        jnp.broadcast_to(log_dt[..., None, None], shp).reshape(rows, cols),
        b_re.reshape(rows, cols), b_im.reshape(rows, cols), c_im.reshape(rows, cols),
    ]
    spec = pl.BlockSpec((rows, cols), lambda: (0, 0))
    outs = pl.pallas_call(
        _s5_disc_kernel,
        in_specs=[spec] * 6,
        out_specs=[spec] * 5,
        out_shape=[jax.ShapeDtypeStruct((rows, cols), F32)] * 5,
        name="s5_discretize",
    )(*args)
    lre, lim, bbre, bbim, ncim = outs
    lre = lre.reshape(shp)[..., 0]
    lim = lim.reshape(shp)[..., 0]
    return (lre, lim, bbre.reshape(shp), bbim.reshape(shp),
            ncim.reshape(DEPTH, 2, SSM_GROUPS, SSM_GROUP, SSM_STATE))


def _state_cols(x):
    lead = x.shape[:-3]
    x = x.reshape(lead + (N_STILE, 2, SSM_STATE, 2))
    x = jnp.moveaxis(x, -1, -3)
    return x.reshape(lead + (STATE_W,))


def _state_uncols(x):
    lead = x.shape[:-1]
    x = x.reshape(lead + (N_STILE, 2, 2, SSM_STATE))
    x = jnp.moveaxis(x, -3, -1)
    return x.reshape(lead + (SSM_GROUPS, SSM_STATE, 2))


def _s5_compact(lre, lim, bbre, bbim, c_re, ncim):
    n = DEPTH * 2
    bb = jnp.stack([bbre, bbim], axis=-1)
    bb = bb.reshape(n, N_STILE, 2, SSM_STATE, SSM_GROUP, 2)
    wb = jnp.transpose(bb, (0, 4, 1, 5, 2, 3)).reshape(DEPTH, 2, SSM_GROUP, STATE_W)
    cc = jnp.stack([c_re, ncim], axis=-1)
    cc = cc.reshape(n, N_STILE, 2, SSM_GROUP, SSM_STATE, 2)
    wc = jnp.transpose(cc, (0, 3, 1, 5, 2, 4)).reshape(DEPTH, 2, SSM_GROUP, STATE_W)
    lam = jnp.concatenate([lre.reshape(DEPTH, 2, N_STILE, 128), lim.reshape(DEPTH, 2, N_STILE, 128)], axis=2)
    return wb, wc, lam


def _group_mask(cols):
    row_g = lax.broadcasted_iota(jnp.int32, (256, cols), 0) // SSM_GROUP
    col = lax.broadcasted_iota(jnp.int32, (256, cols), 1)
    col_g = ((col // 256) % 8) * 2 + (col % 128) // SSM_STATE
    return row_g == col_g


def _s5_scan_kernel(u_ref, wb_ref, wc_ref, lam_ref, h0_ref, y_ref, hf_ref,
                    pm_scr, bm_scr, cm_scr, y_scr, state, ends, *group_scrs, segments):
    bu_scrs = group_scrs[:SSM_TILE_GROUPS]
    hs_scrs = group_scrs[SSM_TILE_GROUPS:]
    npass = 1 if segments == 1 else 2
    d = pl.program_id(0)
    ps = pl.program_id(1)
    ck = pl.program_id(2)
    nck = pl.num_programs(2)
    rows = SSM_TC * SSM_ROWS
    half = STATE_W // 2
    lam = lam_ref[...]

    @pl.when(jnp.logical_and(ps == 0, ck == 0))
    def _():
        a = lax.broadcasted_iota(jnp.int32, (rows, rows), 0)
        b = lax.broadcasted_iota(jnp.int32, (rows, rows), 1)
        same = jnp.logical_and(a // SSM_ROWS == b % SSM_TC, a % SSM_ROWS == b // SSM_TC)
        pm_scr[...] = jnp.where(same, 1.0, 0.0).astype(BF16)
        wb = jnp.tile(wb_ref[...], (256 // SSM_GROUP, 1))
        bm_scr[...] = jnp.where(_group_mask(STATE_W), wb, 0.0).astype(BF16)
        for n_ in range(2):
            wc = jnp.tile(wc_ref[:, n_ * half:(n_ + 1) * half], (256 // SSM_GROUP, 1))
            cm_scr[n_] = jnp.where(_group_mask(half), wc, 0.0).astype(BF16)

    @pl.when(ck == 0)
    def _():
        if npass == 1:
            state[...] = h0_ref[0]
        else:
            @pl.when(ps == 0)
            def _():
                state[...] = jnp.zeros_like(state)

            @pl.when(ps == 1)
            def _():
                ends[...] = state[...]
                state[...] = h0_ref[0]
                lr = lam[0:N_STILE]
                li = lam[N_STILE:]
                for _ in range(int(math.log2(SSM_SEG_LEN))):
                    lr, li = lr * lr - li * li, 2.0 * lr * li

                def carry(order, prev):
                    for sg in order:
                        for sq in range(SSM_ROWS // segments):
                            r = sq * segments + sg
                            q = r + prev
                            for j in range(N_STILE):
                                re = slice(j * 256, j * 256 + 128)
                                im = slice(j * 256 + 128, (j + 1) * 256)
                                pr = state[q:q + 1, re]
                                pi = state[q:q + 1, im]
                                ar = lr[j:j + 1]
                                ai = li[j:j + 1]
                                state[r:r + 1, re] = ends[q:q + 1, re] + ar * pr - ai * pi
                                state[r:r + 1, im] = ends[q:q + 1, im] + ar * pi + ai * pr

                @pl.when(d == 0)
                def _():
                    carry(range(1, segments), -1)

                @pl.when(d == 1)
                def _():
                    carry(range(segments - 2, -1, -1), 1)

    u = _dot(pm_scr[...], u_ref[...].reshape(rows, SSM_W).astype(BF16)).astype(BF16)

    def scan_step(emit):
        ngrp = len(bu_scrs)
        jt = N_STILE // ngrp
        gw = jt * 256

        def bproj(jb):
            k0 = (jb * jt // 8) * 256
            bu_scrs[jb][...] = _dot(u[:, k0:k0 + 256], bm_scr[:, jb * gw:(jb + 1) * gw])

        y_parts = [None, None]
        bproj(0)
        for jb in range(ngrp):
            if jb + 1 < ngrp:
                bproj(jb + 1)
            tiles = list(range(jb * jt, (jb + 1) * jt))
            bu, hs = bu_scrs[jb], hs_scrs[jb]
            lrs = [jnp.broadcast_to(lam[j:j + 1], (SSM_ROWS, 128)) for j in tiles]
            lis = [jnp.broadcast_to(lam[N_STILE + j:N_STILE + j + 1], (SSM_ROWS, 128)) for j in tiles]
            hr = [state[:, j * 256:j * 256 + 128] for j in tiles]
            hi = [state[:, j * 256 + 128:(j + 1) * 256] for j in tiles]
            for t in range(SSM_TC):
                tt = t + d * (SSM_TC - 1 - 2 * t)
                r = pl.ds(pl.multiple_of(tt * SSM_ROWS, SSM_ROWS), SSM_ROWS)
                for n_ in range(jt):
                    re = slice(n_ * 256, n_ * 256 + 128)
                    im = slice(n_ * 256 + 128, (n_ + 1) * 256)
                    nr = lrs[n_] * hr[n_] - lis[n_] * hi[n_] + bu[r, re]
                    ni = lrs[n_] * hi[n_] + lis[n_] * hr[n_] + bu[r, im]
                    if emit:
                        hs[r, re] = nr
                        hs[r, im] = ni
                    hr[n_], hi[n_] = nr, ni
            for n_, j in enumerate(tiles):
                state[:, j * 256:j * 256 + 128] = hr[n_]
                state[:, j * 256 + 128:(j + 1) * 256] = hi[n_]
            if emit:
                n_ = jb * gw // half
                off = jb * gw % half
                part = _dot_nt(hs[...].astype(BF16), cm_scr[n_, :, off:off + gw])
                y_parts[n_] = part if y_parts[n_] is None else y_parts[n_] + part
        if emit:
            for n_ in range(2):
                y_scr[2 * n_] = y_parts[n_][:, :128]
                y_scr[2 * n_ + 1] = y_parts[n_][:, 128:]
            for r in range(SSM_ROWS):
                for c_ in range(SSM_W // 128):
                    y_ref[r, :, c_ * 128:(c_ + 1) * 128] = y_scr[c_, pl.ds(r, SSM_TC, stride=SSM_ROWS), :]

    if npass == 1:
        scan_step(True)
    else:
        @pl.when(ps == 0)
        def _():
            scan_step(False)

        @pl.when(ps == 1)
        def _():
            scan_step(True)

    @pl.when(jnp.logical_and(ps == npass - 1, ck == nck - 1))
    def _():
        hf_ref[0] = state[...]


def _s5_scan(p3, row_block, wb, wc, lam, h0, l, *, segments):
    steps = p3.shape[1]
    nck = steps // SSM_TC
    npass = 1 if segments == 1 else 2

    def chunk(d, c):
        return c + d * (nck - 1 - 2 * c)

    def y_chunk(d, p, c):
        return jnp.where(p == npass - 1, chunk(d, c), chunk(d, 0))

    par = pl.BlockSpec((None, None, SSM_GROUP, STATE_W), lambda d, p, c: (l, d, 0, 0))
    st = pl.BlockSpec((1, SSM_ROWS, STATE_W), lambda d, p, c: (d, 0, 0))
    return pl.pallas_call(
        functools.partial(_s5_scan_kernel, segments=segments),
        grid=(2, npass, nck),
        in_specs=[
            pl.BlockSpec((SSM_ROWS, SSM_TC, SSM_W), lambda d, p, c: (row_block, chunk(d, c), COL_U)),
            par, par,
            pl.BlockSpec((None, None, 2 * N_STILE, 128), lambda d, p, c: (l, d, 0, 0)),
            st,
        ],
        out_specs=[
            pl.BlockSpec((None, SSM_ROWS, SSM_TC, SSM_W), lambda d, p, c: (d, 0, y_chunk(d, p, c), 0)),
            st,
        ],
        out_shape=[
            jax.ShapeDtypeStruct((2, SSM_ROWS, steps, SSM_W), F32),
            jax.ShapeDtypeStruct((2, SSM_ROWS, STATE_W), F32),
        ],
        scratch_shapes=[
            pltpu.VMEM((SSM_TC * SSM_ROWS, SSM_TC * SSM_ROWS), BF16),
            pltpu.VMEM((256, STATE_W), BF16),
            pltpu.VMEM((2, 256, STATE_W // 2), BF16),
            pltpu.VMEM((SSM_W // 128, SSM_TC * SSM_ROWS, 128), F32),
            pltpu.VMEM((SSM_ROWS, STATE_W), F32),
            pltpu.VMEM((SSM_ROWS, STATE_W), F32),
        ] + [pltpu.VMEM((SSM_TC * SSM_ROWS, STATE_W // SSM_TILE_GROUPS), F32)] * (2 * SSM_TILE_GROUPS),
        compiler_params=_cparams(("arbitrary", "arbitrary", "arbitrary")),
        name="s5_scan_seg%d" % segments,
    )(p3, wb, wc, lam, h0)


def _s5_glu_kernel(ypf_ref, ypb_ref, ysf_ref, ysb_ref, u_ref, d_ref, w_ref, o_ref):
    i = pl.program_id(0)
    w = w_ref[...].astype(BF16)

    def glu(y):
        y = y + d_ref[...] * u_ref[...]
        z = _dot(jax.nn.gelu(y).astype(BF16), w)
        o_ref[...] = (z[:, :SSM_W] * _sigmoid(z[:, SSM_W:])).astype(BF16)

    @pl.when(i < N_PT)
    def _():
        glu(ypf_ref[...] + ypb_ref[...])

    @pl.when(i >= N_PT)
    def _():
        glu(ysf_ref[...] + ysb_ref[...])


def _s5_glu(yp, ys, p, d, w, l):
    return pl.pallas_call(
        _s5_glu_kernel,
        grid=(N_MT,),
        in_specs=[
            pl.BlockSpec((None, TM, SSM_W), lambda i: (0, _prompt_idx(i), 0)),
            pl.BlockSpec((None, TM, SSM_W), lambda i: (1, _prompt_idx(i), 0)),
            pl.BlockSpec((None, TM, SSM_W), lambda i: (0, _sample_idx(i), 0)),
            pl.BlockSpec((None, TM, SSM_W), lambda i: (1, _sample_idx(i), 0)),
            pl.BlockSpec((TM, SSM_W), lambda i: (i, COL_U)),
            pl.BlockSpec((None, 1, SSM_W), lambda i: (l, 0, 0)),
            pl.BlockSpec((None, SSM_W, 2 * SSM_W), lambda i: (l, 0, 0)),
        ],
        out_specs=pl.BlockSpec((TM, SSM_W), lambda i: (i, 0)),
        out_shape=jax.ShapeDtypeStruct((N_TOK, SSM_W), BF16),
        compiler_params=_cparams(("arbitrary",)),
        name="s5_glu",
    )(yp, yp, ys, ys, p, d.reshape(DEPTH, 1, SSM_W), w)


def _s5_layer(p, s5p, st_l, ssm_d, w_ssm_glu, l):
    wb, wc, lam = s5p
    yp, hfin = _s5_scan(p.reshape(N_TOK // SEQ, SEQ, IN_W), 0, wb, wc, lam,
                        jnp.zeros((2, SSM_ROWS, STATE_W), F32), l, segments=1)
    h0 = _state_cols(jnp.moveaxis(st_l, 0, 1))
    h0 = jnp.stack([
        jnp.zeros((DEC_BATCH, SSM_SEGS, STATE_W), F32).at[:, 0].set(h0[0]),
        jnp.zeros((DEC_BATCH, SSM_SEGS, STATE_W), F32).at[:, SSM_SEGS - 1].set(h0[1]),
    ]).reshape(2, SSM_ROWS, STATE_W)
    ys, _ = _s5_scan(p.reshape(N_TOK // SSM_SEG_LEN, SSM_SEG_LEN, IN_W), N_PROMPT // SSM_SEG_LEN // SSM_ROWS,
                     wb, wc, lam, h0, l, segments=SSM_SEGS)
    out = _s5_glu(yp.reshape(2, N_PROMPT, SSM_W), ys.reshape(2, N_SAMPLE, SSM_W), p, ssm_d, w_ssm_glu, l)
    new_state = jnp.moveaxis(_state_uncols(hfin), 0, 1)
    return out, new_state


def _rope_tables():
    rows = DEC_SEQ // GRID_W
    r = jnp.repeat(jnp.arange(rows), GRID_W).astype(F32)
    col = jnp.tile(jnp.arange(GRID_W), rows).astype(F32)
    inv = ROPE_BASE ** (-jnp.arange(ROPE_FREQS, dtype=F32) / ROPE_FREQS)
    ang = jnp.stack([r, col], axis=-1)[:, :, None] * inv
    cos = jnp.cos(ang)
    sin = jnp.sin(ang)
    cos_t = jnp.concatenate([cos, cos], axis=-1).reshape(DEC_SEQ, HEAD_DIM)
    sin_t = jnp.concatenate([-sin, sin], axis=-1).reshape(DEC_SEQ, HEAD_DIM)
    return cos_t, sin_t


def kernel(x_prompt, x_sample, cache_k, cache_v, state_ssm, state_ret, c, c_ctx, w_mod, b_mod, norm_g,
           w_ffn_in, w_ffn_out, w_in, w_out, q_norm_g, k_norm_g, ssm_a_re, ssm_a_im, ssm_log_dt,
           ssm_b_re, ssm_b_im, ssm_c_re, ssm_c_im, ssm_d, w_ssm_glu, ret_decay_logit, final_norm_g):
    x = (x_prompt.reshape(N_PROMPT, D_MODEL), x_sample.reshape(N_SAMPLE, D_MODEL))
    cond8 = jnp.concatenate([c_ctx[None], c, jnp.zeros((5, D_MODEL), F32)], axis=0)
    mods = _adaln(cond8, w_mod, b_mod).reshape(DEPTH, 8, N_SUB * 3, D_MODEL)
    modt = jnp.transpose(mods[:, jnp.array(TILE_MOD, jnp.int32)], (0, 2, 1, 3))[:, :, :, None, :]
    gains = norm_g.reshape(DEPTH * N_SUB, 1, D_MODEL)
    cos_t, sin_t = _rope_tables()
    lre, lim, bbre, bbim, ncim = _s5_discretize(ssm_a_re, ssm_a_im, ssm_log_dt, ssm_b_re, ssm_b_im, ssm_c_im)
    s5p = _s5_compact(lre, lim, bbre, bbim, ssm_c_re, ncim)
    dl = jnp.broadcast_to(ret_decay_logit[:, :, :, None, None], (DEPTH, 2, RET_HEADS, 8, HEAD_DIM))
    ks_, vs_, hs_, ss_ = [], [], [], []
    for l in range(DEPTH):
        x = _ffn(x, gains, modt, w_ffn_in, w_ffn_out, l, 0)
        p = _inproj(x, gains, modt, w_in, l)
        attn, k_l, v_l = _attention(p, cache_k, cache_v, q_norm_g, k_norm_g, cos_t, sin_t, l)
        ssm, h_l = _s5_layer(p, s5p, state_ssm[:, l], ssm_d, w_ssm_glu, l)
        ret, s_l = _retention(p, dl, cos_t, sin_t, state_ret, l)
        x = _outproj(x, attn, ssm, ret, modt, w_out, l)
        x = _ffn(x, gains, modt, w_ffn_in, w_ffn_out, l, 1, final_g=final_norm_g if l == DEPTH - 1 else None)
        ks_.append(k_l)
        vs_.append(v_l)
        hs_.append(h_l)
        ss_.append(s_l)
    y_prompt = x[0].reshape(BATCH, SEQ, D_MODEL)
    y_sample = x[1].reshape(DEC_BATCH, DEC_SEQ, D_MODEL)
    return (y_prompt, y_sample, jnp.stack(ks_, axis=1), jnp.stack(vs_, axis=1),
            jnp.stack(hs_, axis=1), jnp.stack(ss_, axis=1))
```

```python
import functools
import math

import jax
import jax.numpy as jnp
from jax import lax
from jax.experimental import pallas as pl
from jax.experimental.pallas import tpu as pltpu

D_MODEL = 2048
BATCH = 16
SEQ = 256
DEPTH = 2
DEC_BATCH = 2
DEC_SEQ = 1024
PAST_LEN = 512
GRID_W = 64
HEAD_DIM = 128
N_Q_HEADS = 8
N_KV_HEADS = 2
Q_PER_KV = 4
ATTN_W = 1024
KV_W = 256
SSM_W = 512
SSM_GROUP = 16
SSM_GROUPS = 32
SSM_STATE = 64
RET_HEADS = 4
RET_W = 512
IN_W = 4096
D_FF = 5632
N_SUB = 3
RET_CHUNK = 128
ROPE_BASE = 10000.0
ROPE_FREQS = 32
EPS = 1e-6

N_PROMPT = BATCH * SEQ
N_SAMPLE = DEC_BATCH * DEC_SEQ
N_TOK = N_PROMPT + N_SAMPLE
TM = 1024
N_MT = N_TOK // TM
N_PT = N_PROMPT // TM
SEQ_PER_TILE = TM // SEQ
TILE_MOD = (0, 0, 0, 0, 1, 2)
ROW_CHUNK = 256
TF = 256
FFN_NSPLIT = 4
TN = 512
STATE_W = 2 * SSM_GROUPS * SSM_STATE
N_STILE = STATE_W // 256
SSM_ROWS = 16
SSM_SEGS = 8
SSM_SEG_LEN = DEC_SEQ // SSM_SEGS
SSM_TC = 32
SSM_TILE_GROUPS = 4
ATT_TQ = 256
ATT_TQ_ATTN = 512
ATT_S = PAST_LEN + DEC_SEQ
VMEM_LIMIT = 56 * 1024 * 1024
VMEM_LIMIT_FFN = 60 * 1024 * 1024

COL_K = ATTN_W // KV_W
COL_V = COL_K + 1
COL_U = (ATTN_W + 2 * KV_W) // SSM_W
COL_R = (ATTN_W + 2 * KV_W + SSM_W) // RET_W

BF16 = jnp.bfloat16
F32 = jnp.float32


def _cparams(sem, limit=VMEM_LIMIT):
    return pltpu.CompilerParams(dimension_semantics=sem, vmem_limit_bytes=limit)


def _dot(a, b):
    return jnp.dot(a, b, preferred_element_type=F32)


def _dot_nt(a, b):
    return lax.dot_general(a, b, (((1,), (1,)), ((), ())), preferred_element_type=F32)


def _sigmoid(x):
    return 1.0 / (1.0 + jnp.exp(-x))


def _silu(x):
    return x * _sigmoid(x)


def _sample_idx(i):
    return jnp.clip(i - N_PT, 0, DEC_BATCH - 1)


def _prompt_idx(i):
    return jnp.minimum(i, N_PT - 1)


def _row_chunks(n_rows, body):
    def step(c, carry):
        body(pl.ds(pl.multiple_of(c * ROW_CHUNK, ROW_CHUNK), ROW_CHUNK))
        return carry

    lax.fori_loop(0, n_rows // ROW_CHUNK, step, 0)


def _mod_spec(l, k, width=D_MODEL, col=lambda j: 0, tile=lambda i: i):
    return pl.BlockSpec((None, None, 1, 1, width), lambda i, j: (l, k, tile(i), 0, col(j)))


def _gain_spec(l, sub):
    return pl.BlockSpec((None, 1, D_MODEL), lambda i, j: (l * N_SUB + sub, 0, 0))


def _adaln_kernel(c_ref, w_ref, b_ref, o_ref):
    a = _silu(c_ref[...]).astype(BF16)
    o_ref[0] = _dot(a, w_ref[0].astype(BF16)) + b_ref[0]


def _adaln(cond8, w_mod, b_mod):
    n = w_mod.shape[-1]
    tn = 1024
    return pl.pallas_call(
        _adaln_kernel,
        grid=(DEPTH, n // tn),
        in_specs=[
            pl.BlockSpec((8, D_MODEL), lambda l, j: (0, 0)),
            pl.BlockSpec((1, D_MODEL, tn), lambda l, j: (l, 0, j)),
            pl.BlockSpec((1, 1, tn), lambda l, j: (l, 0, j)),
        ],
        out_specs=pl.BlockSpec((1, 8, tn), lambda l, j: (l, 0, j)),
        out_shape=jax.ShapeDtypeStruct((DEPTH, 8, n), F32),
        compiler_params=_cparams(("arbitrary", "arbitrary")),
        name="adaln",
    )(cond8, w_mod, b_mod.reshape(DEPTH, 1, n))


def _norm_mod(x, g, shift, scale):
    ms = jnp.mean(x * x, axis=-1, keepdims=True)
    return x * lax.rsqrt(ms + EPS) * (g * (1.0 + scale)) + shift


def _ffn_kernel(*refs, n_x, n_out, final):
    refs = list(refs)
    x_refs = refs[:n_x]
    g_ref, sh_ref, sc_ref, gt_ref, wa_ref, wu_ref, wo_ref = refs[n_x:n_x + 7]
    rest = refs[n_x + 7:]
    fg_ref = rest.pop(0) if final else None
    o_refs, h_scr = rest[:n_out], rest[n_out]
    i = pl.program_id(0)
    j = pl.program_id(1)

    def run(x_ref, o_ref):
        @pl.when(j == 0)
        def _():
            def pre(rows):
                h = _norm_mod(x_ref[rows, :], g_ref[...], sh_ref[0], sc_ref[0])
                h_scr[rows, :] = h.astype(BF16)

            _row_chunks(TM, pre)
            o_ref[...] = jnp.zeros_like(o_ref)

        h = h_scr[...]
        a = _dot(h, wa_ref[...].astype(BF16))
        u = _dot(h, wu_ref[...].astype(BF16))
        mid = (_silu(a) * u).astype(BF16)
        wn = D_MODEL // FFN_NSPLIT
        for n in range(FFN_NSPLIT):
            cs = slice(n * wn, (n + 1) * wn)
            o_ref[:, cs] += _dot(mid, wo_ref[:, cs].astype(BF16))

        @pl.when(j == pl.num_programs(1) - 1)
        def _():
            def post(rows):
                out = x_ref[rows, :] + (0.5 * gt_ref[0]) * o_ref[rows, :]
                if final:
                    ms = jnp.mean(out * out, axis=-1, keepdims=True)
                    out = out * lax.rsqrt(ms + EPS) * fg_ref[...]
                o_ref[rows, :] = out

            _row_chunks(TM, post)

    if n_x == 1 and n_out == 1:
        run(x_refs[0], o_refs[0])
    else:
        @pl.when(i < N_PT)
        def _():
            run(x_refs[0], o_refs[0])

        @pl.when(i >= N_PT)
        def _():
            run(x_refs[-1], o_refs[-1])


def _split_tok_specs():
    return [pl.BlockSpec((TM, D_MODEL), lambda i, j: (_prompt_idx(i), 0)),
            pl.BlockSpec((TM, D_MODEL), lambda i, j: (_sample_idx(i), 0), pipeline_mode=pl.Buffered(1))]


def _ffn(x, gains, modt, w_in, w_out, l, f, final_g=None):
    final = final_g is not None
    xs = list(x) if isinstance(x, (tuple, list)) else [x]
    sub = 2 * f
    nf = D_FF // TF
    tok = pl.BlockSpec((TM, D_MODEL), lambda i, j: (i, 0))
    in_specs = (_split_tok_specs() if len(xs) == 2 else [tok]) + [
        _gain_spec(l, sub), _mod_spec(l, 3 * sub), _mod_spec(l, 3 * sub + 1), _mod_spec(l, 3 * sub + 2),
        pl.BlockSpec((None, None, D_MODEL, TF), lambda i, j: (l, f, 0, j)),
        pl.BlockSpec((None, None, D_MODEL, TF), lambda i, j: (l, f, 0, j + nf)),
        pl.BlockSpec((None, None, TF, D_MODEL), lambda i, j: (l, f, j, 0)),
    ]
    args = xs + [gains, modt, modt, modt, w_in, w_in, w_out]
    if final:
        in_specs.append(pl.BlockSpec((1, D_MODEL), lambda i, j: (0, 0)))
        args.append(final_g.reshape(1, D_MODEL))
        out_specs = _split_tok_specs()
        out_shape = [jax.ShapeDtypeStruct((N_PROMPT, D_MODEL), F32), jax.ShapeDtypeStruct((N_SAMPLE, D_MODEL), F32)]
    else:
        out_specs = tok
        out_shape = jax.ShapeDtypeStruct((N_TOK, D_MODEL), F32)
    return pl.pallas_call(
        functools.partial(_ffn_kernel, n_x=len(xs), n_out=2 if final else 1, final=final),
        grid=(N_MT, nf),
        in_specs=in_specs,
        out_specs=out_specs,
        out_shape=out_shape,
        scratch_shapes=[pltpu.VMEM((TM, D_MODEL), BF16)],
        compiler_params=_cparams(("arbitrary", "arbitrary"), VMEM_LIMIT_FFN),
        name="ffn_final" if final else ("ffn_first" if len(xs) == 2 else "ffn"),
    )(*args)


def _resident_weight_spec(rows, l, ncol):
    return pl.BlockSpec((None, rows, TN), lambda i, j: (l, 0, jnp.where(i == 0, j, ncol - 1)))


INPROJ_CHUNK = TM // (IN_W // TN)


def _inproj_kernel(x_ref, g_ref, sh_ref, sc_ref, shn_ref, scn_ref, w_ref, o_ref, h_a, h_b, w_scr):
    i = pl.program_id(0)
    j = pl.program_id(1)
    nchunk = TM // INPROJ_CHUNK

    def norm_chunk(c, sh_r, sc_r, dst):
        rows = pl.ds(pl.multiple_of(c * INPROJ_CHUNK, INPROJ_CHUNK), INPROJ_CHUNK)
        h = _norm_mod(x_ref[rows, :], g_ref[...], sh_r[0], sc_r[0])
        dst[rows, :] = h.astype(BF16)

    def step(h_cur, h_nxt):
        @pl.when(j == 0)
        def _():
            @pl.when(i == 0)
            def _():
                def first(c, carry):
                    norm_chunk(c, sh_ref, sc_ref, h_cur)
                    return carry

                lax.fori_loop(0, nchunk - 1, first, 0)

            norm_chunk(nchunk - 1, sh_ref, sc_ref, h_cur)

        @pl.when(i == 0)
        def _():
            w_scr[j] = w_ref[...].astype(BF16)

        norm_chunk((j + nchunk - 1) % nchunk, shn_ref, scn_ref, h_nxt)
        o_ref[...] = _dot(h_cur[...], w_scr[j])

    @pl.when(i % 2 == 0)
    def _():
        step(h_a, h_b)

    @pl.when(i % 2 == 1)
    def _():
        step(h_b, h_a)


def _inproj(x, gains, modt, w, l):
    ncol = IN_W // TN

    def nxt(i):
        return jnp.minimum(i + 1, N_MT - 1)

    return pl.pallas_call(
        _inproj_kernel,
        grid=(N_MT, ncol),
        in_specs=[
            pl.BlockSpec((TM, D_MODEL), lambda i, j: (jnp.where(j == 0, i, nxt(i)), 0)),
            _gain_spec(l, 1), _mod_spec(l, 3), _mod_spec(l, 4), _mod_spec(l, 3, tile=nxt), _mod_spec(l, 4, tile=nxt),
            _resident_weight_spec(D_MODEL, l, ncol),
        ],
        out_specs=pl.BlockSpec((TM, TN), lambda i, j: (i, j)),
        out_shape=jax.ShapeDtypeStruct((N_TOK, IN_W), F32),
        scratch_shapes=[pltpu.VMEM((TM, D_MODEL), BF16), pltpu.VMEM((TM, D_MODEL), BF16),
                        pltpu.VMEM((ncol, D_MODEL, TN), BF16)],
        compiler_params=_cparams(("arbitrary", "arbitrary"), VMEM_LIMIT_FFN),
        name="inproj",
    )(x, gains, modt, modt, modt, modt, w)


def _outproj_kernel(x_ref, a_ref, s_ref, r_ref, gt_ref, w_ref, o_ref, w_scr):
    i = pl.program_id(0)
    j = pl.program_id(1)

    @pl.when(i == 0)
    def _():
        w_scr[j] = w_ref[...].astype(BF16)

    y = _dot(a_ref[...], w_scr[j, 0:ATTN_W, :])
    y += _dot(s_ref[...], w_scr[j, ATTN_W:ATTN_W + SSM_W, :])
    y += _dot(r_ref[...], w_scr[j, ATTN_W + SSM_W:, :])
    o_ref[...] = x_ref[...] + gt_ref[0] * y


def _outproj(x, attn, ssm, ret, modt, w, l):
    ncol = D_MODEL // TN
    return pl.pallas_call(
        _outproj_kernel,
        grid=(N_MT, ncol),
        in_specs=[
            pl.BlockSpec((TM, TN), lambda i, j: (i, j)),
            pl.BlockSpec((TM, ATTN_W), lambda i, j: (i, 0)),
            pl.BlockSpec((TM, SSM_W), lambda i, j: (i, 0)),
            pl.BlockSpec((TM, RET_W), lambda i, j: (i, 0)),
            _mod_spec(l, 5, TN, lambda j: j),
            _resident_weight_spec(D_MODEL, l, ncol),
        ],
        out_specs=pl.BlockSpec((TM, TN), lambda i, j: (i, j)),
        out_shape=jax.ShapeDtypeStruct((N_TOK, D_MODEL), F32),
        scratch_shapes=[pltpu.VMEM((ncol, D_MODEL, TN), BF16)],
        compiler_params=_cparams(("arbitrary", "arbitrary")),
        name="outproj",
    )(x, attn, ssm, ret, modt, w)


def _head_rms(x, g):
    ms = jnp.mean(x * x, axis=-1, keepdims=True)
    return x * lax.rsqrt(ms + EPS) * g


def _rope(x, cos, sin_signed):
    lane = lax.broadcasted_iota(jnp.int32, x.shape, 1)
    partner = jnp.where((lane % 64) < 32, pltpu.roll(x, 96, 1), pltpu.roll(x, 32, 1))
    return x * cos + partner * sin_signed


SOFTMAX_EXP2_SCALE = HEAD_DIM ** -0.5 * math.log2(math.e)


def _attend_short(qs, k, v):
    s = _dot_nt(qs.astype(BF16), k)
    p = jnp.exp2((s - jnp.max(s, axis=-1, keepdims=True)) * SOFTMAX_EXP2_SCALE)
    l = jnp.sum(p, axis=-1, keepdims=True)
    return _dot(p.astype(BF16), v) / l


def _attn_kernel(q_ref, k_ref, v_ref, ck_ref, cv_ref, qg_ref, kg_ref, cos_ref, sin_ref,
                 o_ref, kc_ref, vc_ref, k_scr, vt_scr):
    i = pl.program_id(0)
    qg = qg_ref[...]
    kg = kg_ref[...]

    def heads(kv):
        return [slice((kv * Q_PER_KV + g) * HEAD_DIM, (kv * Q_PER_KV + g + 1) * HEAD_DIM) for g in range(Q_PER_KV)]

    @pl.when(i < N_PT)
    def _():
        def seq_body(sq, carry):
            rows = pl.ds(pl.multiple_of(sq * SEQ, SEQ), SEQ)
            v = v_ref[rows, :]
            for kv_ in range(N_KV_HEADS):
                vc_ref[sq, :, kv_, :] = v[:, kv_ * HEAD_DIM:(kv_ + 1) * HEAD_DIM]
            for kv in range(N_KV_HEADS):
                ksl = slice(kv * HEAD_DIM, (kv + 1) * HEAD_DIM)
                kn = _head_rms(k_ref[rows, ksl], kg)
                kc_ref[sq, :, kv, :] = kn
                qs = jnp.concatenate([_head_rms(q_ref[rows, hs], qg) for hs in heads(kv)], axis=0)
                o = _attend_short(qs, kn.astype(BF16), v[:, ksl].astype(BF16))
                for g, hs in enumerate(heads(kv)):
                    o_ref[rows, hs] = o[g * SEQ:(g + 1) * SEQ].astype(BF16)
            return carry

        lax.fori_loop(0, SEQ_PER_TILE, seq_body, 0)

    @pl.when(i >= N_PT)
    def _():
        k_scr[0:PAST_LEN, :] = ck_ref[...].astype(BF16)
        for kv in range(N_KV_HEADS):
            ksl = slice(kv * HEAD_DIM, (kv + 1) * HEAD_DIM)
            vt_scr[kv, :, 0:PAST_LEN] = cv_ref[:, ksl].T.astype(BF16)
            vt_scr[kv, :, PAST_LEN:] = v_ref[:, ksl].T.astype(BF16)
            kn = _rope(_head_rms(k_ref[:, ksl], kg), cos_ref[...], sin_ref[...])
            k_scr[PAST_LEN:, ksl] = kn.astype(BF16)

        def q_body(qb, carry):
            rows = pl.ds(pl.multiple_of(qb * ATT_TQ_ATTN, ATT_TQ_ATTN), ATT_TQ_ATTN)
            cos = cos_ref[rows, :]
            sin = sin_ref[rows, :]
            sts = []
            for kv in range(N_KV_HEADS):
                ksl = slice(kv * HEAD_DIM, (kv + 1) * HEAD_DIM)
                qs = jnp.concatenate([_rope(_head_rms(q_ref[rows, hs], qg), cos, sin) for hs in heads(kv)], axis=0)
                sts.append(_dot_nt(k_scr[:, ksl], qs.astype(BF16)))
            pls = []
            for st in sts:
                p = jnp.exp2((st - jnp.max(st, axis=0, keepdims=True)) * SOFTMAX_EXP2_SCALE)
                pls.append((p.astype(BF16), jnp.sum(p, axis=0, keepdims=True)))
            for kv, (p, l) in enumerate(pls):
                o = (_dot(vt_scr[kv], p) / l).T
                for g, hs in enumerate(heads(kv)):
                    o_ref[rows, hs] = o[g * ATT_TQ_ATTN:(g + 1) * ATT_TQ_ATTN].astype(BF16)
            return carry

        lax.fori_loop(0, DEC_SEQ // ATT_TQ_ATTN, q_body, 0)


def _attention(p, cache_k, cache_v, qg, kg, cos_t, sin_t, l):
    cache_spec = pl.BlockSpec((None, None, PAST_LEN, KV_W), lambda i: (_sample_idx(i), l, 0, 0))
    vec = pl.BlockSpec((None, 1, HEAD_DIM), lambda i: (l, 0, 0))
    tab = pl.BlockSpec((DEC_SEQ, HEAD_DIM), lambda i: (0, 0))
    new_cache = pl.BlockSpec((SEQ_PER_TILE, SEQ, N_KV_HEADS, HEAD_DIM), lambda i: (_prompt_idx(i), 0, 0, 0))
    return pl.pallas_call(
        _attn_kernel,
        grid=(N_MT,),
        in_specs=[
            pl.BlockSpec((TM, ATTN_W), lambda i: (i, 0)),
            pl.BlockSpec((TM, KV_W), lambda i: (i, COL_K)),
            pl.BlockSpec((TM, KV_W), lambda i: (i, COL_V)),
            cache_spec, cache_spec, vec, vec, tab, tab,
        ],
        out_specs=[pl.BlockSpec((TM, ATTN_W), lambda i: (i, 0)), new_cache, new_cache],
        out_shape=[
            jax.ShapeDtypeStruct((N_TOK, ATTN_W), BF16),
            jax.ShapeDtypeStruct((BATCH, SEQ, N_KV_HEADS, HEAD_DIM), F32),
            jax.ShapeDtypeStruct((BATCH, SEQ, N_KV_HEADS, HEAD_DIM), F32),
        ],
        scratch_shapes=[pltpu.VMEM((ATT_S, KV_W), BF16), pltpu.VMEM((N_KV_HEADS, HEAD_DIM, ATT_S), BF16)],
        compiler_params=_cparams(("arbitrary",)),
        name="attention",
    )(p, p, p, cache_k.reshape(DEC_BATCH, DEPTH, PAST_LEN, KV_W), cache_v.reshape(DEC_BATCH, DEPTH, PAST_LEN, KV_W),
      qg.reshape(DEPTH, 1, HEAD_DIM), kg.reshape(DEPTH, 1, HEAD_DIM), cos_t, sin_t)


def _log_sigmoid(x):
    return -(jnp.maximum(-x, 0.0) + jnp.log(1.0 + jnp.exp(-jnp.abs(x))))


def _decay_mask(qi, kj, lg_f, lg_b):
    dd = qi - kj
    log2e = math.log2(math.e)
    w = jnp.exp2(dd * jnp.where(dd > 0, lg_f * log2e, -lg_b * log2e))
    return jnp.where(dd == 0, 2.0, w)


def _retention_kernel(q_ref, k_ref, v_ref, g_ref, dl_ref, cos_ref, sin_ref, s0_ref, o_ref, sf_ref,
                      acc, qr_scr, kr_scr, vb_scr):
    i = pl.program_id(0)
    scale = HEAD_DIM ** -0.5
    lg = [[_log_sigmoid(dl_ref[d, h][0:1, :]) for h in range(RET_HEADS)] for d in range(2)]
    lg1 = [[lg[d][h][:, 0:1] for h in range(RET_HEADS)] for d in range(2)]
    hsl = [slice(h * HEAD_DIM, (h + 1) * HEAD_DIM) for h in range(RET_HEADS)]

    @pl.when(i < N_PT)
    def _():
        qi = lax.broadcasted_iota(jnp.int32, (SEQ, SEQ), 0).astype(F32)
        kj = lax.broadcasted_iota(jnp.int32, (SEQ, SEQ), 1).astype(F32)
        pos = lax.broadcasted_iota(jnp.int32, (SEQ, HEAD_DIM), 0).astype(F32)
        masks = [_decay_mask(qi, kj, lg1[0][h], lg1[1][h]) for h in range(RET_HEADS)]
        kdec_f = [jnp.exp((SEQ - 1.0 - pos) * lg[0][h]) for h in range(RET_HEADS)]
        kdec_b = [jnp.exp(pos * lg[1][h]) for h in range(RET_HEADS)]

        def seq_body(sq, carry):
            rows = pl.ds(pl.multiple_of(sq * SEQ, SEQ), SEQ)
            for h in range(RET_HEADS):
                q = q_ref[rows, hsl[h]].astype(BF16)
                k = k_ref[rows, hsl[h]] * scale
                v = v_ref[rows, hsl[h]].astype(BF16)
                a = (_dot_nt(q, k.astype(BF16)) * masks[h]).astype(BF16)
                acc[rows, hsl[h]] = _dot(a, v)
                sf_ref[sq, 0, h] = _dot((k * kdec_f[h]).T.astype(BF16), v)
                sf_ref[sq, 1, h] = _dot((k * kdec_b[h]).T.astype(BF16), v)
            return carry

        lax.fori_loop(0, SEQ_PER_TILE, seq_body, 0)

    @pl.when(i >= N_PT)
    def _():
        for h in range(RET_HEADS):
            qr_scr[:, hsl[h]] = _rope(q_ref[:, hsl[h]], cos_ref[...], sin_ref[...]).astype(BF16)
            kr_scr[:, hsl[h]] = _rope(k_ref[:, hsl[h]] * scale, cos_ref[...], sin_ref[...]).astype(BF16)
        vb_scr[...] = v_ref[...].astype(BF16)

        def q_body(qb, carry):
            row0 = qb * ATT_TQ
            rows = pl.ds(pl.multiple_of(row0, ATT_TQ), ATT_TQ)
            qi = (row0 + lax.broadcasted_iota(jnp.int32, (ATT_TQ, DEC_SEQ), 0)).astype(F32)
            kj = lax.broadcasted_iota(jnp.int32, (ATT_TQ, DEC_SEQ), 1).astype(F32)
            pos = (row0 + lax.broadcasted_iota(jnp.int32, (ATT_TQ, HEAD_DIM), 0)).astype(F32)
            for h in range(RET_HEADS):
                q = qr_scr[rows, hsl[h]]
                a = (_dot_nt(q, kr_scr[:, hsl[h]]) * _decay_mask(qi, kj, lg1[0][h], lg1[1][h])).astype(BF16)
                o = _dot(a, vb_scr[:, hsl[h]])
                o += _dot(q, s0_ref[0, h].astype(BF16)) * jnp.exp((pos + 1.0) * lg[0][h])
                o += _dot(q, s0_ref[1, h].astype(BF16)) * jnp.exp((DEC_SEQ - pos) * lg[1][h])
                acc[rows, hsl[h]] = o
            return carry

        lax.fori_loop(0, DEC_SEQ // ATT_TQ, q_body, 0)

    for h in range(RET_HEADS):
        o = acc[:, hsl[h]]
        o = o - jnp.mean(o, axis=-1, keepdims=True)
        o = o * lax.rsqrt(jnp.mean(o * o, axis=-1, keepdims=True) + EPS)
        o_ref[:, hsl[h]] = (o * _silu(g_ref[:, hsl[h]])).astype(BF16)


def _retention(p, dl, cos_t, sin_t, state_ret, l):
    tab = pl.BlockSpec((DEC_SEQ, HEAD_DIM), lambda i: (0, 0))
    st = (2, RET_HEADS, HEAD_DIM, HEAD_DIM)
    return pl.pallas_call(
        _retention_kernel,
        grid=(N_MT,),
        in_specs=[pl.BlockSpec((TM, RET_W), lambda i, k=k: (i, COL_R + k)) for k in range(4)] + [
            pl.BlockSpec((None, 2, RET_HEADS, 8, HEAD_DIM), lambda i: (l, 0, 0, 0, 0)),
            tab, tab,
            pl.BlockSpec((None, None) + st, lambda i: (_sample_idx(i), l, 0, 0, 0, 0)),
        ],
        out_specs=[
            pl.BlockSpec((TM, RET_W), lambda i: (i, 0)),
            pl.BlockSpec((SEQ_PER_TILE,) + st, lambda i: (_prompt_idx(i), 0, 0, 0, 0)),
        ],
        out_shape=[
            jax.ShapeDtypeStruct((N_TOK, RET_W), BF16),
            jax.ShapeDtypeStruct((BATCH,) + st, F32),
        ],
        scratch_shapes=[pltpu.VMEM((TM, RET_W), F32)] + [pltpu.VMEM((TM, RET_W), BF16)] * 3,
        compiler_params=_cparams(("arbitrary",)),
        name="retention",
    )(p, p, p, p, dl, cos_t, sin_t, state_ret)


def _s5_disc_kernel(are_ref, aim_ref, ldt_ref, bre_ref, bim_ref, cim_ref,
                    lre_ref, lim_ref, bbre_ref, bbim_ref, ncim_ref):
    ar = are_ref[...]
    ai = aim_ref[...]
    dt = jnp.exp(ldt_ref[...])
    mag = jnp.exp(ar * dt)
    lr = mag * jnp.cos(ai * dt)
    li = mag * jnp.sin(ai * dt)
    den = ar * ar + ai * ai
    nr = lr - 1.0
    cr = (nr * ar + li * ai) / den
    ci = (li * ar - nr * ai) / den
    br = bre_ref[...]
    bi = bim_ref[...]
    lre_ref[...] = lr
    lim_ref[...] = li
    bbre_ref[...] = cr * br - ci * bi
    bbim_ref[...] = cr * bi + ci * br
    ncim_ref[...] = -cim_ref[...]


def _s5_discretize(a_re, a_im, log_dt, b_re, b_im, c_im):
    rows = DEPTH * 2 * SSM_GROUPS
    cols = SSM_STATE * SSM_GROUP
    shp = (DEPTH, 2, SSM_GROUPS, SSM_STATE, SSM_GROUP)
    args = [
        jnp.broadcast_to(a_re[..., None], shp).reshape(rows, cols),
        jnp.broadcast_to(a_im[..., None], shp).reshape(rows, cols),
        jnp.broadcast_to(log_dt[..., None, None], shp).reshape(rows, cols),
        b_re.reshape(rows, cols), b_im.reshape(rows, cols), c_im.reshape(rows, cols),
    ]
    spec = pl.BlockSpec((rows, cols), lambda: (0, 0))
    outs = pl.pallas_call(
        _s5_disc_kernel,
        in_specs=[spec] * 6,
        out_specs=[spec] * 5,
        out_shape=[jax.ShapeDtypeStruct((rows, cols), F32)] * 5,
        name="s5_discretize",
    )(*args)
    lre, lim, bbre, bbim, ncim = outs
    lre = lre.reshape(shp)[..., 0]
    lim = lim.reshape(shp)[..., 0]
    return (lre, lim, bbre.reshape(shp), bbim.reshape(shp),
            ncim.reshape(DEPTH, 2, SSM_GROUPS, SSM_GROUP, SSM_STATE))


def _state_cols(x):
    lead = x.shape[:-3]
    x = x.reshape(lead + (N_STILE, 2, SSM_STATE, 2))
    x = jnp.moveaxis(x, -1, -3)
    return x.reshape(lead + (STATE_W,))


def _state_uncols(x):
    lead = x.shape[:-1]
    x = x.reshape(lead + (N_STILE, 2, 2, SSM_STATE))
    x = jnp.moveaxis(x, -3, -1)
    return x.reshape(lead + (SSM_GROUPS, SSM_STATE, 2))


def _s5_compact(lre, lim, bbre, bbim, c_re, ncim):
    n = DEPTH * 2
    bb = jnp.stack([bbre, bbim], axis=-1)
    bb = bb.reshape(n, N_STILE, 2, SSM_STATE, SSM_GROUP, 2)
    wb = jnp.transpose(bb, (0, 4, 1, 5, 2, 3)).reshape(DEPTH, 2, SSM_GROUP, STATE_W)
    cc = jnp.stack([c_re, ncim], axis=-1)
    cc = cc.reshape(n, N_STILE, 2, SSM_GROUP, SSM_STATE, 2)
    wc = jnp.transpose(cc, (0, 3, 1, 5, 2, 4)).reshape(DEPTH, 2, SSM_GROUP, STATE_W)
    lam = jnp.concatenate([lre.reshape(DEPTH, 2, N_STILE, 128), lim.reshape(DEPTH, 2, N_STILE, 128)], axis=2)
    return wb, wc, lam


def _group_mask(cols):
    row_g = lax.broadcasted_iota(jnp.int32, (256, cols), 0) // SSM_GROUP
    col = lax.broadcasted_iota(jnp.int32, (256, cols), 1)
    col_g = ((col // 256) % 8) * 2 + (col % 128) // SSM_STATE
    return row_g == col_g


def _s5_scan_kernel(u_ref, wb_ref, wc_ref, lam_ref, h0_ref, y_ref, hf_ref,
                    pm_scr, bm_scr, cm_scr, y_scr, state, ends, *group_scrs, segments):
    bu_scrs = group_scrs[:SSM_TILE_GROUPS]
    hs_scrs = group_scrs[SSM_TILE_GROUPS:]
    npass = 1 if segments == 1 else 2
    d = pl.program_id(0)
    ps = pl.program_id(1)
    ck = pl.program_id(2)
    nck = pl.num_programs(2)
    rows = SSM_TC * SSM_ROWS
    half = STATE_W // 2
    lam = lam_ref[...]

    @pl.when(jnp.logical_and(ps == 0, ck == 0))
    def _():
        a = lax.broadcasted_iota(jnp.int32, (rows, rows), 0)
        b = lax.broadcasted_iota(jnp.int32, (rows, rows), 1)
        same = jnp.logical_and(a // SSM_ROWS == b % SSM_TC, a % SSM_ROWS == b // SSM_TC)
        pm_scr[...] = jnp.where(same, 1.0, 0.0).astype(BF16)
        wb = jnp.tile(wb_ref[...], (256 // SSM_GROUP, 1))
        bm_scr[...] = jnp.where(_group_mask(STATE_W), wb, 0.0).astype(BF16)
        for n_ in range(2):
            wc = jnp.tile(wc_ref[:, n_ * half:(n_ + 1) * half], (256 // SSM_GROUP, 1))
            cm_scr[n_] = jnp.where(_group_mask(half), wc, 0.0).astype(BF16)

    @pl.when(ck == 0)
    def _():
        if npass == 1:
            state[...] = h0_ref[0]
        else:
            @pl.when(ps == 0)
            def _():
                state[...] = jnp.zeros_like(state)

            @pl.when(ps == 1)
            def _():
                ends[...] = state[...]
                state[...] = h0_ref[0]
                lr = lam[0:N_STILE]
                li = lam[N_STILE:]
                for _ in range(int(math.log2(SSM_SEG_LEN))):
                    lr, li = lr * lr - li * li, 2.0 * lr * li

                def carry(order, prev):
                    for sg in order:
                        for sq in range(SSM_ROWS // segments):
                            r = sq * segments + sg
                            q = r + prev
                            for j in range(N_STILE):
                                re = slice(j * 256, j * 256 + 128)
                                im = slice(j * 256 + 128, (j + 1) * 256)
                                pr = state[q:q + 1, re]
                                pi = state[q:q + 1, im]
                                ar = lr[j:j + 1]
                                ai = li[j:j + 1]
                                state[r:r + 1, re] = ends[q:q + 1, re] + ar * pr - ai * pi
                                state[r:r + 1, im] = ends[q:q + 1, im] + ar * pi + ai * pr

                @pl.when(d == 0)
                def _():
                    carry(range(1, segments), -1)

                @pl.when(d == 1)
                def _():
                    carry(range(segments - 2, -1, -1), 1)

    u = _dot(pm_scr[...], u_ref[...].reshape(rows, SSM_W).astype(BF16)).astype(BF16)

    def scan_step(emit):
        ngrp = len(bu_scrs)
        jt = N_STILE // ngrp
        gw = jt * 256

        def bproj(jb):
            k0 = (jb * jt // 8) * 256
            bu_scrs[jb][...] = _dot(u[:, k0:k0 + 256], bm_scr[:, jb * gw:(jb + 1) * gw])

        y_parts = [None, None]
        bproj(0)
        for jb in range(ngrp):
            if jb + 1 < ngrp:
                bproj(jb + 1)
            tiles = list(range(jb * jt, (jb + 1) * jt))
            bu, hs = bu_scrs[jb], hs_scrs[jb]
            lrs = [jnp.broadcast_to(lam[j:j + 1], (SSM_ROWS, 128)) for j in tiles]
            lis = [jnp.broadcast_to(lam[N_STILE + j:N_STILE + j + 1], (SSM_ROWS, 128)) for j in tiles]
            hr = [state[:, j * 256:j * 256 + 128] for j in tiles]
            hi = [state[:, j * 256 + 128:(j + 1) * 256] for j in tiles]
            for t in range(SSM_TC):
                tt = t + d * (SSM_TC - 1 - 2 * t)
                r = pl.ds(pl.multiple_of(tt * SSM_ROWS, SSM_ROWS), SSM_ROWS)
                for n_ in range(jt):
                    re = slice(n_ * 256, n_ * 256 + 128)
                    im = slice(n_ * 256 + 128, (n_ + 1) * 256)
                    nr = lrs[n_] * hr[n_] - lis[n_] * hi[n_] + bu[r, re]
                    ni = lrs[n_] * hi[n_] + lis[n_] * hr[n_] + bu[r, im]
                    if emit:
                        hs[r, re] = nr
                        hs[r, im] = ni
                    hr[n_], hi[n_] = nr, ni
            for n_, j in enumerate(tiles):
                state[:, j * 256:j * 256 + 128] = hr[n_]
                state[:, j * 256 + 128:(j + 1) * 256] = hi[n_]
            if emit:
                n_ = jb * gw // half
                off = jb * gw % half
                part = _dot_nt(hs[...].astype(BF16), cm_scr[n_, :, off:off + gw])
                y_parts[n_] = part if y_parts[n_] is None else y_parts[n_] + part
        if emit:
            for n_ in range(2):
                y_scr[2 * n_] = y_parts[n_][:, :128]
                y_scr[2 * n_ + 1] = y_parts[n_][:, 128:]
            for r in range(SSM_ROWS):
                for c_ in range(SSM_W // 128):
                    y_ref[r, :, c_ * 128:(c_ + 1) * 128] = y_scr[c_, pl.ds(r, SSM_TC, stride=SSM_ROWS), :]

    if npass == 1:
        scan_step(True)
    else:
        @pl.when(ps == 0)
        def _():
            scan_step(False)

        @pl.when(ps == 1)
        def _():
            scan_step(True)

    @pl.when(jnp.logical_and(ps == npass - 1, ck == nck - 1))
    def _():
        hf_ref[0] = state[...]


def _s5_scan(p3, row_block, wb, wc, lam, h0, l, *, segments):
    steps = p3.shape[1]
    nck = steps // SSM_TC
    npass = 1 if segments == 1 else 2

    def chunk(d, c):
        return c + d * (nck - 1 - 2 * c)

    def y_chunk(d, p, c):
        return jnp.where(p == npass - 1, chunk(d, c), chunk(d, 0))

    par = pl.BlockSpec((None, None, SSM_GROUP, STATE_W), lambda d, p, c: (l, d, 0, 0))
    st = pl.BlockSpec((1, SSM_ROWS, STATE_W), lambda d, p, c: (d, 0, 0))
    return pl.pallas_call(
        functools.partial(_s5_scan_kernel, segments=segments),
        grid=(2, npass, nck),
        in_specs=[
            pl.BlockSpec((SSM_ROWS, SSM_TC, SSM_W), lambda d, p, c: (row_block, chunk(d, c), COL_U)),
            par, par,
            pl.BlockSpec((None, None, 2 * N_STILE, 128), lambda d, p, c: (l, d, 0, 0)),
            st,
        ],
        out_specs=[
            pl.BlockSpec((None, SSM_ROWS, SSM_TC, SSM_W), lambda d, p, c: (d, 0, y_chunk(d, p, c), 0)),
            st,
        ],
        out_shape=[
            jax.ShapeDtypeStruct((2, SSM_ROWS, steps, SSM_W), F32),
            jax.ShapeDtypeStruct((2, SSM_ROWS, STATE_W), F32),
        ],
        scratch_shapes=[
            pltpu.VMEM((SSM_TC * SSM_ROWS, SSM_TC * SSM_ROWS), BF16),
            pltpu.VMEM((256, STATE_W), BF16),
            pltpu.VMEM((2, 256, STATE_W // 2), BF16),
            pltpu.VMEM((SSM_W // 128, SSM_TC * SSM_ROWS, 128), F32),
            pltpu.VMEM((SSM_ROWS, STATE_W), F32),
            pltpu.VMEM((SSM_ROWS, STATE_W), F32),
        ] + [pltpu.VMEM((SSM_TC * SSM_ROWS, STATE_W // SSM_TILE_GROUPS), F32)] * (2 * SSM_TILE_GROUPS),
        compiler_params=_cparams(("arbitrary", "arbitrary", "arbitrary")),
        name="s5_scan_seg%d" % segments,
    )(p3, wb, wc, lam, h0)


def _s5_glu_kernel(ypf_ref, ypb_ref, ysf_ref, ysb_ref, u_ref, d_ref, w_ref, o_ref):
    i = pl.program_id(0)
    w = w_ref[...].astype(BF16)

    def glu(y):
        y = y + d_ref[...] * u_ref[...]
        z = _dot(jax.nn.gelu(y).astype(BF16), w)
        o_ref[...] = (z[:, :SSM_W] * _sigmoid(z[:, SSM_W:])).astype(BF16)

    @pl.when(i < N_PT)
    def _():
        glu(ypf_ref[...] + ypb_ref[...])

    @pl.when(i >= N_PT)
    def _():
        glu(ysf_ref[...] + ysb_ref[...])


def _s5_glu(yp, ys, p, d, w, l):
    return pl.pallas_call(
        _s5_glu_kernel,
        grid=(N_MT,),
        in_specs=[
            pl.BlockSpec((None, TM, SSM_W), lambda i: (0, _prompt_idx(i), 0)),
            pl.BlockSpec((None, TM, SSM_W), lambda i: (1, _prompt_idx(i), 0)),
            pl.BlockSpec((None, TM, SSM_W), lambda i: (0, _sample_idx(i), 0)),
            pl.BlockSpec((None, TM, SSM_W), lambda i: (1, _sample_idx(i), 0)),
            pl.BlockSpec((TM, SSM_W), lambda i: (i, COL_U)),
            pl.BlockSpec((None, 1, SSM_W), lambda i: (l, 0, 0)),
            pl.BlockSpec((None, SSM_W, 2 * SSM_W), lambda i: (l, 0, 0)),
        ],
        out_specs=pl.BlockSpec((TM, SSM_W), lambda i: (i, 0)),
        out_shape=jax.ShapeDtypeStruct((N_TOK, SSM_W), BF16),
        compiler_params=_cparams(("arbitrary",)),
        name="s5_glu",
    )(yp, yp, ys, ys, p, d.reshape(DEPTH, 1, SSM_W), w)


def _s5_layer(p, s5p, st_l, ssm_d, w_ssm_glu, l):
    wb, wc, lam = s5p
    yp, hfin = _s5_scan(p.reshape(N_TOK // SEQ, SEQ, IN_W), 0, wb, wc, lam,
                        jnp.zeros((2, SSM_ROWS, STATE_W), F32), l, segments=1)
    h0 = _state_cols(jnp.moveaxis(st_l, 0, 1))
    h0 = jnp.stack([
        jnp.zeros((DEC_BATCH, SSM_SEGS, STATE_W), F32).at[:, 0].set(h0[0]),
        jnp.zeros((DEC_BATCH, SSM_SEGS, STATE_W), F32).at[:, SSM_SEGS - 1].set(h0[1]),
    ]).reshape(2, SSM_ROWS, STATE_W)
    ys, _ = _s5_scan(p.reshape(N_TOK // SSM_SEG_LEN, SSM_SEG_LEN, IN_W), N_PROMPT // SSM_SEG_LEN // SSM_ROWS,
                     wb, wc, lam, h0, l, segments=SSM_SEGS)
    out = _s5_glu(yp.reshape(2, N_PROMPT, SSM_W), ys.reshape(2, N_SAMPLE, SSM_W), p, ssm_d, w_ssm_glu, l)
    new_state = jnp.moveaxis(_state_uncols(hfin), 0, 1)
    return out, new_state


def _rope_tables():
    rows = DEC_SEQ // GRID_W
    r = jnp.repeat(jnp.arange(rows), GRID_W).astype(F32)
    col = jnp.tile(jnp.arange(GRID_W), rows).astype(F32)
    inv = ROPE_BASE ** (-jnp.arange(ROPE_FREQS, dtype=F32) / ROPE_FREQS)
    ang = jnp.stack([r, col], axis=-1)[:, :, None] * inv
    cos = jnp.cos(ang)
    sin = jnp.sin(ang)
    cos_t = jnp.concatenate([cos, cos], axis=-1).reshape(DEC_SEQ, HEAD_DIM)
    sin_t = jnp.concatenate([-sin, sin], axis=-1).reshape(DEC_SEQ, HEAD_DIM)
    return cos_t, sin_t


def kernel(x_prompt, x_sample, cache_k, cache_v, state_ssm, state_ret, c, c_ctx, w_mod, b_mod, norm_g,
           w_ffn_in, w_ffn_out, w_in, w_out, q_norm_g, k_norm_g, ssm_a_re, ssm_a_im, ssm_log_dt,
           ssm_b_re, ssm_b_im, ssm_c_re, ssm_c_im, ssm_d, w_ssm_glu, ret_decay_logit, final_norm_g):
    x = (x_prompt.reshape(N_PROMPT, D_MODEL), x_sample.reshape(N_SAMPLE, D_MODEL))
    cond8 = jnp.concatenate([c_ctx[None], c, jnp.zeros((5, D_MODEL), F32)], axis=0)
    mods = _adaln(cond8, w_mod, b_mod).reshape(DEPTH, 8, N_SUB * 3, D_MODEL)
    modt = jnp.transpose(mods[:, jnp.array(TILE_MOD, jnp.int32)], (0, 2, 1, 3))[:, :, :, None, :]
    gains = norm_g.reshape(DEPTH * N_SUB, 1, D_MODEL)
    cos_t, sin_t = _rope_tables()
    lre, lim, bbre, bbim, ncim = _s5_discretize(ssm_a_re, ssm_a_im, ssm_log_dt, ssm_b_re, ssm_b_im, ssm_c_im)
    s5p = _s5_compact(lre, lim, bbre, bbim, ssm_c_re, ncim)
    dl = jnp.broadcast_to(ret_decay_logit[:, :, :, None, None], (DEPTH, 2, RET_HEADS, 8, HEAD_DIM))
    ks_, vs_, hs_, ss_ = [], [], [], []
    for l in range(DEPTH):
        x = _ffn(x, gains, modt, w_ffn_in, w_ffn_out, l, 0)
        p = _inproj(x, gains, modt, w_in, l)
        attn, k_l, v_l = _attention(p, cache_k, cache_v, q_norm_g, k_norm_g, cos_t, sin_t, l)
        ssm, h_l = _s5_layer(p, s5p, state_ssm[:, l], ssm_d, w_ssm_glu, l)
        ret, s_l = _retention(p, dl, cos_t, sin_t, state_ret, l)
        x = _outproj(x, attn, ssm, ret, modt, w_out, l)
        x = _ffn(x, gains, modt, w_ffn_in, w_ffn_out, l, 1, final_g=final_norm_g if l == DEPTH - 1 else None)
        ks_.append(k_l)
        vs_.append(v_l)
        hs_.append(h_l)
        ss_.append(s_l)
    y_prompt = x[0].reshape(BATCH, SEQ, D_MODEL)
    y_sample = x[1].reshape(DEC_BATCH, DEC_SEQ, D_MODEL)
    return (y_prompt, y_sample, jnp.stack(ks_, axis=1), jnp.stack(vs_, axis=1),
            jnp.stack(hs_, axis=1), jnp.stack(ss_, axis=1))
```
